```python
import jax, jax.numpy as jnp
from jax import lax
import numpy as np

D_MODEL = 2048
BATCH = 2
SEQ = 8192
DEPTH = 4
DEC_BATCH = 32
DEC_SEQ = 32
PAST_LEN = 1024

CHUNK = 64
FOX_HEADS = 8
FOX_HD = 128
FOX_W = FOX_HEADS * FOX_HD
Q_BLOCK = 128
RWKV_HEADS = 16
RWKV_HD = 64
RWKV_W = RWKV_HEADS * RWKV_HD
RWKV_W_LORA = 64
RWKV_A_LORA = 64
RWKV_G_LORA = 128
RWKV_IN_W = 3 * RWKV_W + RWKV_W_LORA + RWKV_A_LORA + RWKV_G_LORA
GMLP_W = 1024
GMLP_GROUPS = 8
GMLP_GC = GMLP_W // GMLP_GROUPS
GMLP_CHUNK = 128
N_BRANCH = 3
IN_W = 3 * FOX_W + FOX_HEADS + RWKV_IN_W + 2 * GMLP_W + N_BRANCH * D_MODEL
D_FF = -(-(8 * D_MODEL) // (3 * 256)) * 256
RMS_EPS = 1e-6
LN_EPS = 1e-5
RWKV_GN_EPS = 64e-5
FOX_FORGET_INIT = 4.0
NEG_INF = -1e30

kernel_name = 'fox_rwkv7_gmlp_hybrid_stream_step'


def rms_norm(x, g):
    xf = x.astype(jnp.float32)
    y = xf * lax.rsqrt(jnp.mean(xf * xf, axis=-1, keepdims=True) + RMS_EPS)
    return (y * g.astype(jnp.float32)).astype(x.dtype)


def layer_norm(x, g, b, eps):
    xf = x.astype(jnp.float32)
    mu = jnp.mean(xf, axis=-1, keepdims=True)
    var = jnp.mean(jnp.square(xf - mu), axis=-1, keepdims=True)
    return ((xf - mu) * lax.rsqrt(var + eps) * g.astype(jnp.float32) + b.astype(jnp.float32)).astype(x.dtype)


def fox_attend_prompt(q, k, v, logf):
    B, T, H, D = q.shape
    nblk = T // Q_BLOCK
    ct = jnp.cumsum(logf, axis=1).transpose(0, 2, 1)
    qb = q.reshape(B, nblk, Q_BLOCK, H, D).transpose(1, 0, 2, 3, 4)
    cb = ct.reshape(B, H, nblk, Q_BLOCK).transpose(2, 0, 1, 3)
    kpos = jnp.arange(T)
    scale = FOX_HD ** -0.5

    def block(args):
        qi, ci, i = args
        s = jnp.einsum('bqhd,bkhd->bhqk', qi, k, preferred_element_type=jnp.float32) * scale
        s = s + ci[..., None] - ct[:, :, None, :]
        qpos = i * Q_BLOCK + jnp.arange(Q_BLOCK)
        s = jnp.where(kpos[None, :] <= qpos[:, None], s, NEG_INF)
        p = jax.nn.softmax(s, axis=-1)
        return jnp.einsum('bhqk,bkhd->bqhd', p.astype(v.dtype), v)

    o = lax.map(block, (qb, cb, jnp.arange(nblk)))
    return o.transpose(1, 0, 2, 3, 4).reshape(B, T, H * D)


def fox_attend_sample(q, k, v, logf, k_cache, v_cache, logf_cache):
    B, S, H, D = q.shape
    P = k_cache.shape[1]
    k_all = jnp.concatenate([k_cache.astype(k.dtype), k], axis=1)
    v_all = jnp.concatenate([v_cache.astype(v.dtype), v], axis=1)
    c = jnp.cumsum(jnp.concatenate([logf_cache.astype(jnp.float32), logf], axis=1), axis=1).transpose(0, 2, 1)
    s = jnp.einsum('bqhd,bkhd->bhqk', q, k_all, preferred_element_type=jnp.float32) * (FOX_HD ** -0.5)
    s = s + c[:, :, P:, None] - c[:, :, None, :]
    mask = jnp.arange(P + S)[None, :] <= (P + jnp.arange(S))[:, None]
    s = jnp.where(mask, s, NEG_INF)
    p = jax.nn.softmax(s, axis=-1)
    o = jnp.einsum('bhqk,bkhd->bqhd', p.astype(v_all.dtype), v_all)
    return o.reshape(B, S, H * D)


def rwkv_scan(r, w, k, v, a, b, state):
    def step(S, inp):
        r_t, w_t, k_t, v_t, a_t, b_t = inp
        sa = jnp.einsum('bhij,bhj->bhi', S, a_t)
        S = S * w_t[:, :, None, :] + sa[..., None] * b_t[:, :, None, :] + v_t[..., None] * k_t[:, :, None, :]
        y = jnp.einsum('bhij,bhj->bhi', S, r_t)
        return S, y
    xs = (r.swapaxes(0, 1), w.swapaxes(0, 1), k.swapaxes(0, 1), v.swapaxes(0, 1), a.swapaxes(0, 1), b.swapaxes(0, 1))
    S, ys = lax.scan(step, state, xs)
    return ys.swapaxes(0, 1), S


def rwkv_time_mix(p, prev, state, lw):
    f32 = jnp.float32
    B, T, _ = p.shape
    p_prev = jnp.concatenate([prev.astype(p.dtype), p[:, :-1]], axis=1)
    xs = p + (p_prev - p) * lw['rwkv_mu']
    o1 = 3 * RWKV_W
    r, k, v, dw, da, dg = jnp.split(xs, [RWKV_W, 2 * RWKV_W, o1, o1 + RWKV_W_LORA, o1 + RWKV_W_LORA + RWKV_A_LORA], axis=-1)
    r, k, v = r.astype(f32), k.astype(f32), v.astype(f32)
    wlog = -jax.nn.softplus(-(lw['rwkv_w0'] + jnp.tanh(dw) @ lw['rwkv_w_up']).astype(f32)) - 0.5
    decay = jnp.exp(-jnp.exp(wlog))
    a = jax.nn.sigmoid((lw['rwkv_a0'] + da @ lw['rwkv_a_up']).astype(f32))
    g = (jax.nn.sigmoid(dg) @ lw['rwkv_g_up']).astype(f32)

    def heads(t):
        return t.reshape(B, T, RWKV_HEADS, RWKV_HD)

    kk = heads(k * lw['rwkv_k_k'].astype(f32))
    kk = kk * lax.rsqrt(jnp.maximum(jnp.sum(kk * kk, axis=-1, keepdims=True), 1e-24))
    k = k * (1.0 + (a - 1.0) * lw['rwkv_k_a'].astype(f32))
    r_h, k_h, v_h, a_h = heads(r), heads(k), heads(v), heads(a)
    y, S = rwkv_scan(r_h, heads(decay), k_h, v_h, -kk, kk * a_h, state.astype(f32))
    mu = jnp.mean(y, axis=-1, keepdims=True)
    var = jnp.mean(jnp.square(y - mu), axis=-1, keepdims=True)
    yn = ((y - mu) * lax.rsqrt(var + RWKV_GN_EPS)).reshape(B, T, RWKV_W)
    yn = yn * lw['rwkv_gn_g'].astype(f32) + lw['rwkv_gn_b'].astype(f32)
    bonus = jnp.sum(r_h * k_h * lw['rwkv_r_k'].astype(f32), axis=-1, keepdims=True) * v_h
    out = (yn + bonus.reshape(B, T, RWKV_W)) * g
    return out.astype(p.dtype), S, p[:, -1:]


def gmlp_spatial_gate(gu, gv, lw, rows):
    B, T, _ = gu.shape
    u = jax.nn.gelu(gu)
    vn = layer_norm(jax.nn.gelu(gv), lw['gmlp_ln_g'], lw['gmlp_ln_b'], LN_EPS)
    n = T // rows
    vc = vn.reshape(B, n, rows, GMLP_GROUPS, GMLP_GC)
    w = jnp.tril(lw['gmlp_w_s'][:, :rows, :rows])
    bias = lw['gmlp_b_s'][:, :rows].T
    s = jnp.einsum('gij,bnjgc->bnigc', w, vc) + bias[None, None, :, :, None]
    return u * s.reshape(B, T, GMLP_W), vn


def trunk_layer(x, lw, cache):
    B, T, _ = x.shape
    h = rms_norm(x, lw['pre_mix_g'])
    proj = h @ lw['w_in']
    o_f = 3 * FOX_W + FOX_HEADS
    o_r = o_f + RWKV_IN_W
    fq, fk, fv, ff, rw, gu, gv, gates = jnp.split(
        proj, [FOX_W, 2 * FOX_W, 3 * FOX_W, o_f, o_r, o_r + GMLP_W, o_r + 2 * GMLP_W], axis=-1)
    q = fq.reshape(B, T, FOX_HEADS, FOX_HD)
    k = fk.reshape(B, T, FOX_HEADS, FOX_HD)
    v = fv.reshape(B, T, FOX_HEADS, FOX_HD)
    logf = jax.nn.log_sigmoid((ff + lw['fox_bf']).astype(jnp.float32))
    if cache is None:
        o_a = fox_attend_prompt(q, k, v, logf)
        prev = jnp.zeros((B, 1, RWKV_IN_W), x.dtype)
        s0 = jnp.zeros((B, RWKV_HEADS, RWKV_HD, RWKV_HD), jnp.float32)
        rows = GMLP_CHUNK
    else:
        o_a = fox_attend_sample(q, k, v, logf, cache['k'], cache['v'], cache['logf'])
        prev = cache['shift']
        s0 = cache['state']
        rows = T
    o_b, s_new, shift_new = rwkv_time_mix(rw, prev, s0, lw)
    o_c, vn = gmlp_spatial_gate(gu, gv, lw, rows)
    g_a, g_b, g_c = jnp.split(jax.nn.sigmoid(gates), N_BRANCH, axis=-1)
    m = (g_a * (o_a @ lw['w_br_fox']) + g_b * (o_b @ lw['w_br_rwkv'])
         + g_c * (o_c @ lw['w_br_gmlp']))
    x = x + rms_norm(m @ lw['w_out'], lw['post_mix_g'])
    h2 = rms_norm(x, lw['pre_ffn_g'])
    f = (jax.nn.silu(h2 @ lw['ffn_w1']) * (h2 @ lw['ffn_w3'])) @ lw['ffn_w2']
    x = x + rms_norm(f, lw['post_ffn_g'])
    return x, (k, v, logf, s_new, shift_new, vn)


def setup_inputs(seed: int = 0) -> dict:
    key = jax.random.key(seed)
    ks = iter(jax.random.split(key, 48))
    f32 = jnp.float32

    def nrm(shape, scale=1.0):
        return jax.random.normal(next(ks), shape, f32) * scale

    def gain(shape):
        return 1.0 + nrm(shape, 0.02)

    L = DEPTH
    return {
        'x_prompt': nrm((BATCH, SEQ, D_MODEL)),
        'x_sample': nrm((DEC_BATCH, DEC_SEQ, D_MODEL)),
        'cache_fox_k': nrm((L, DEC_BATCH, PAST_LEN, FOX_HEADS, FOX_HD)),
        'cache_fox_v': nrm((L, DEC_BATCH, PAST_LEN, FOX_HEADS, FOX_HD)),
        'cache_fox_logf': jax.nn.log_sigmoid(FOX_FORGET_INIT + nrm((L, DEC_BATCH, PAST_LEN, FOX_HEADS))),
        'state_rwkv': nrm((L, DEC_BATCH, RWKV_HEADS, RWKV_HD, RWKV_HD), 0.3),
        'state_rwkv_shift': nrm((L, DEC_BATCH, 1, RWKV_IN_W)),
        'pre_mix_g': gain((L, D_MODEL)),
        'w_in': nrm((L, D_MODEL, IN_W), D_MODEL ** -0.5),
        'fox_bf': FOX_FORGET_INIT + nrm((L, FOX_HEADS), 0.1),
        'rwkv_mu': jax.random.uniform(next(ks), (L, RWKV_IN_W), f32),
        'rwkv_w0': -1.0 + nrm((L, RWKV_W), 0.5),
        'rwkv_w_up': nrm((L, RWKV_W_LORA, RWKV_W), RWKV_W_LORA ** -0.5),
        'rwkv_a0': nrm((L, RWKV_W), 0.1),
        'rwkv_a_up': nrm((L, RWKV_A_LORA, RWKV_W), RWKV_A_LORA ** -0.5),
        'rwkv_g_up': nrm((L, RWKV_G_LORA, RWKV_W), RWKV_G_LORA ** -0.5),
        'rwkv_k_k': 0.85 + nrm((L, RWKV_W), 0.02),
        'rwkv_k_a': gain((L, RWKV_W)),
        'rwkv_r_k': nrm((L, RWKV_HEADS, RWKV_HD), 0.1),
        'rwkv_gn_g': gain((L, RWKV_W)),
        'rwkv_gn_b': nrm((L, RWKV_W), 0.02),
        'gmlp_ln_g': gain((L, GMLP_W)),
        'gmlp_ln_b': nrm((L, GMLP_W), 0.02),
        'gmlp_w_s': nrm((L, GMLP_GROUPS, GMLP_CHUNK, GMLP_CHUNK), GMLP_CHUNK ** -0.5),
        'gmlp_b_s': gain((L, GMLP_GROUPS, GMLP_CHUNK)),
        'w_br_fox': nrm((L, FOX_W, D_MODEL), FOX_W ** -0.5),
        'w_br_rwkv': nrm((L, RWKV_W, D_MODEL), RWKV_W ** -0.5),
        'w_br_gmlp': nrm((L, GMLP_W, D_MODEL), GMLP_W ** -0.5),
        'w_out': nrm((L, D_MODEL, D_MODEL), D_MODEL ** -0.5),
        'post_mix_g': gain((L, D_MODEL)),
        'pre_ffn_g': gain((L, D_MODEL)),
        'ffn_w1': nrm((L, D_MODEL, D_FF), D_MODEL ** -0.5),
        'ffn_w3': nrm((L, D_MODEL, D_FF), D_MODEL ** -0.5),
        'ffn_w2': nrm((L, D_FF, D_MODEL), D_FF ** -0.5),
        'post_ffn_g': gain((L, D_MODEL)),
    }


def reference(x_prompt, x_sample, cache_fox_k, cache_fox_v, cache_fox_logf, state_rwkv, state_rwkv_shift,
              pre_mix_g, w_in, fox_bf, rwkv_mu, rwkv_w0, rwkv_w_up, rwkv_a0, rwkv_a_up, rwkv_g_up,
              rwkv_k_k, rwkv_k_a, rwkv_r_k, rwkv_gn_g, rwkv_gn_b, gmlp_ln_g, gmlp_ln_b, gmlp_w_s, gmlp_b_s,
              w_br_fox, w_br_rwkv, w_br_gmlp, w_out, post_mix_g, pre_ffn_g, ffn_w1, ffn_w3, ffn_w2, post_ffn_g):
    xp, xs = x_prompt, x_sample
    kp, vp, lp, sp, shp = [], [], [], [], []
    ksm, vsm, lsm, ssm, shs, gvs = [], [], [], [], [], []
    for l in range(DEPTH):
        lw = {
            'pre_mix_g': pre_mix_g[l], 'w_in': w_in[l], 'fox_bf': fox_bf[l],
            'rwkv_mu': rwkv_mu[l], 'rwkv_w0': rwkv_w0[l], 'rwkv_w_up': rwkv_w_up[l],
            'rwkv_a0': rwkv_a0[l], 'rwkv_a_up': rwkv_a_up[l], 'rwkv_g_up': rwkv_g_up[l],
            'rwkv_k_k': rwkv_k_k[l], 'rwkv_k_a': rwkv_k_a[l], 'rwkv_r_k': rwkv_r_k[l],
            'rwkv_gn_g': rwkv_gn_g[l], 'rwkv_gn_b': rwkv_gn_b[l],
            'gmlp_ln_g': gmlp_ln_g[l], 'gmlp_ln_b': gmlp_ln_b[l], 'gmlp_w_s': gmlp_w_s[l], 'gmlp_b_s': gmlp_b_s[l],
            'w_br_fox': w_br_fox[l], 'w_br_rwkv': w_br_rwkv[l], 'w_br_gmlp': w_br_gmlp[l],
            'w_out': w_out[l], 'post_mix_g': post_mix_g[l], 'pre_ffn_g': pre_ffn_g[l],
            'ffn_w1': ffn_w1[l], 'ffn_w3': ffn_w3[l], 'ffn_w2': ffn_w2[l], 'post_ffn_g': post_ffn_g[l],
        }
        xp, (k_p, v_p, lf_p, s_p, sh_p, _vn_p) = trunk_layer(xp, lw, None)
        cache = {'k': cache_fox_k[l], 'v': cache_fox_v[l], 'logf': cache_fox_logf[l],
                 'state': state_rwkv[l], 'shift': state_rwkv_shift[l]}
        xs, (k_s, v_s, lf_s, s_s, sh_s, vn_s) = trunk_layer(xs, lw, cache)
        kp.append(k_p); vp.append(v_p); lp.append(lf_p); sp.append(s_p); shp.append(sh_p)
        ksm.append(k_s); vsm.append(v_s); lsm.append(lf_s); ssm.append(s_s); shs.append(sh_s); gvs.append(vn_s)
    fox_k_prompt = jnp.stack(kp)
    fox_v_prompt = jnp.stack(vp)
    fox_logf_prompt = jnp.stack(lp)
    rwkv_state_prompt = jnp.stack(sp)
    rwkv_shift_prompt = jnp.stack(shp)
    fox_k_sample = jnp.stack(ksm)
    fox_v_sample = jnp.stack(vsm)
    fox_logf_sample = jnp.stack(lsm)
    rwkv_state_sample = jnp.stack(ssm)
    rwkv_shift_sample = jnp.stack(shs)
    gmlp_v_sample = jnp.stack(gvs)
    return (xp, xs, fox_k_prompt, fox_v_prompt, fox_logf_prompt, rwkv_state_prompt, rwkv_shift_prompt,
            fox_k_sample, fox_v_sample, fox_logf_sample, rwkv_state_sample, rwkv_shift_sample, gmlp_v_sample)
```

```python
import functools

import jax
import jax.numpy as jnp
from jax import lax
from jax.experimental import pallas as pl
from jax.experimental.pallas import tpu as pltpu

F32 = jnp.float32
BF16 = jnp.bfloat16

RMS_EPS = 1e-6
LN_EPS = 1e-5
RWKV_GN_EPS = 64e-5
NEG_INF = -1e30

LANE = 128
FOX_HEADS = 8
FOX_HD = 128
FOX_W = FOX_HEADS * FOX_HD
RWKV_HEADS = 16
RWKV_HD = 64
RWKV_W = RWKV_HEADS * RWKV_HD
RWKV_PAIRS = RWKV_W // LANE
RWKV_LORA_OFF = 3 * RWKV_W
RWKV_IN_W = 3 * RWKV_W + 2 * LANE
GMLP_W = 1024
GMLP_GROUPS = 8
GMLP_GC = GMLP_W // GMLP_GROUPS
GMLP_CHUNK = 128
N_BRANCH = 3

VMEM_LIMIT = 48 * 1024 * 1024


def _params(*sem):
    return pltpu.CompilerParams(dimension_semantics=sem, vmem_limit_bytes=VMEM_LIMIT)


def _tile(n, pref):
    if n <= pref:
        return n
    t = pref
    while n % t:
        t -= 8
    return t


def _col_tile(n, pref):
    assert n % LANE == 0
    k = n // LANE
    best = 1
    for d in range(1, k + 1):
        if k % d == 0 and d * LANE <= pref:
            best = d
    return best * LANE


def _split2(x):
    hi = x.astype(BF16)
    lo = (x - hi.astype(F32)).astype(BF16)
    return hi, lo


def _split3(x):
    hi = x.astype(BF16)
    r = x - hi.astype(F32)
    mid = r.astype(BF16)
    lo = (r - mid.astype(F32)).astype(BF16)
    return hi, mid, lo


_NN = (((1,), (0,)), ((), ()))
_NT = (((1,), (1,)), ((), ()))
_TN = (((0,), (0,)), ((), ()))


def _dot(a, b, dims=_NN):
    return lax.dot_general(a, b, dims, preferred_element_type=F32)


def _dot3(x, y, dims=_NN):
    xh, xl = _split2(x)
    yh, yl = _split2(y)
    return _dot(xh, yh, dims) + _dot(xh, yl, dims) + _dot(xl, yh, dims)


def _dot_exact_rhs(x, e, parts):
    ps = _split3(x) if parts == 3 else _split2(x)
    out = _dot(ps[0], e)
    for p in ps[1:]:
        out = out + _dot(p, e)
    return out


def _sigmoid(x):
    return 1.0 / (1.0 + jnp.exp(-x))


def _log_sigmoid(x):
    return jnp.minimum(x, 0.0) - jnp.log(1.0 + jnp.exp(-jnp.abs(x)))


def _gelu_tanh(x):
    return 0.5 * x * (1.0 + jnp.tanh(0.7978845608028654 * (x + 0.044715 * (x * x * x))))


def _rms(x, g):
    ms = jnp.mean(x * x, axis=-1, keepdims=True)
    return x * lax.rsqrt(ms + RMS_EPS) * g


def _rmsnorm_cast_kernel(x_ref, g_ref, o_ref):
    o_ref[...] = _rms(x_ref[...], g_ref[...]).astype(o_ref.dtype)


def _rmsnorm_cast(x, g):
    m, d = x.shape
    tm = _tile(m, 512)
    return pl.pallas_call(
        _rmsnorm_cast_kernel,
        grid=(m // tm,),
        in_specs=[pl.BlockSpec((tm, d), lambda i: (i, 0)),
                  pl.BlockSpec((1, d), lambda i: (0, 0))],
        out_specs=pl.BlockSpec((tm, d), lambda i: (i, 0)),
        out_shape=jax.ShapeDtypeStruct((m, d), BF16),
        compiler_params=_params("parallel"),
        name="rmsnorm_cast",
    )(x, g.reshape(1, d))


def _matmul_kernel(a_ref, w_ref, o_ref):
    o_ref[...] = _dot(a_ref[...], w_ref[...]).astype(o_ref.dtype)


def _matmul(a, w, out_dtype=F32, tm_pref=1024, tn_pref=512):
    m, k = a.shape
    n = w.shape[1]
    tm = _tile(m, tm_pref)
    tn = _col_tile(n, tn_pref)
    return pl.pallas_call(
        _matmul_kernel,
        grid=(m // tm, n // tn),
        in_specs=[pl.BlockSpec((tm, k), lambda i, j: (i, 0)),
                  pl.BlockSpec((k, tn), lambda i, j: (0, j))],
        out_specs=pl.BlockSpec((tm, tn), lambda i, j: (i, j)),
        out_shape=jax.ShapeDtypeStruct((m, n), out_dtype),
        compiler_params=_params("parallel", "parallel"),
        name="proj_matmul",
    )(a, w)


def _logf_kernel(a_ref, w_ref, b_ref, o_ref):
    o_ref[...] = _log_sigmoid(_dot(a_ref[...], w_ref[...]) + b_ref[...])


def _forget_logits(h, w_ff, b_ff):
    m, k = h.shape
    tm = _tile(m, 1024)
    return pl.pallas_call(
        _logf_kernel,
        grid=(m // tm,),
        in_specs=[pl.BlockSpec((tm, k), lambda i: (i, 0)),
                  pl.BlockSpec((k, LANE), lambda i: (0, 0)),
                  pl.BlockSpec((1, LANE), lambda i: (0, 0))],
        out_specs=pl.BlockSpec((tm, LANE), lambda i: (i, 0)),
        out_shape=jax.ShapeDtypeStruct((m, LANE), F32),
        compiler_params=_params("parallel"),
        name="forget_logits",
    )(h, w_ff, b_ff)


def _cumsum_kernel(x_ref, o_ref, carry_ref, *, tb):
    @pl.when(pl.program_id(0) == 0)
    def _():
        carry_ref[...] = jnp.zeros_like(carry_ref)

    row = lax.broadcasted_iota(jnp.int32, (tb, tb), 0)
    col = lax.broadcasted_iota(jnp.int32, (tb, tb), 1)
    upper = jnp.where(row <= col, 1.0, 0.0).astype(BF16)
    c = _dot_exact_rhs(x_ref[...], upper, 3) + carry_ref[:, :1]
    o_ref[...] = c
    carry_ref[...] = jnp.broadcast_to(c[:, tb - 1:tb], carry_ref.shape)


def _cumsum_lanes(x, tb):
    r, t = x.shape
    return pl.pallas_call(
        functools.partial(_cumsum_kernel, tb=tb),
        grid=(t // tb,),
        in_specs=[pl.BlockSpec((r, tb), lambda i: (0, i))],
        out_specs=pl.BlockSpec((r, tb), lambda i: (0, i)),
        out_shape=jax.ShapeDtypeStruct((r, t), F32),
        scratch_shapes=[pltpu.VMEM((r, LANE), F32)],
        compiler_params=_params("arbitrary"),
        name="logf_cumsum",
    )(x)


def _fox_prompt_kernel(q_ref, k_ref, v_ref, cq_ref, ck_ref, o_ref, m_ref, l_ref, acc_ref, *, tq):
    qi = pl.program_id(1)
    ki = pl.program_id(2)
    scale = FOX_HD ** -0.5

    @pl.when(ki == 0)
    def _():
        m_ref[...] = jnp.full_like(m_ref, NEG_INF)
        l_ref[...] = jnp.zeros_like(l_ref)
        acc_ref[...] = jnp.zeros_like(acc_ref)

    def update(diagonal):
        if diagonal:
            row = lax.broadcasted_iota(jnp.int32, (tq, tq), 0)
            col = lax.broadcasted_iota(jnp.int32, (tq, tq), 1)
            causal = col <= row
        for h in range(FOX_HEADS):
            hs = slice(h * FOX_HD, (h + 1) * FOX_HD)
            q = q_ref[:, hs].astype(BF16)
            k = k_ref[:, hs].astype(BF16)
            s = _dot(q, k, _NT) * scale + (cq_ref[:, h:h + 1] - ck_ref[0, h:h + 1, :])
            if diagonal:
                s = jnp.where(causal, s, NEG_INF)
            m_prev = m_ref[h]
            m_new = jnp.maximum(m_prev, jnp.max(s, axis=-1, keepdims=True))
            alpha = jnp.exp(m_prev - m_new)
            p = jnp.exp(s - m_new[:, :1])
            l_ref[h] = alpha * l_ref[h] + jnp.sum(p, axis=-1, keepdims=True)
            acc_ref[:, hs] = alpha * acc_ref[:, hs] + _dot(p.astype(BF16), v_ref[:, hs].astype(BF16))
            m_ref[h] = m_new

    @pl.when(ki < qi)
    def _():
        update(False)

    @pl.when(ki == qi)
    def _():
        update(True)
        for h in range(FOX_HEADS):
            hs = slice(h * FOX_HD, (h + 1) * FOX_HD)
            o_ref[:, hs] = (acc_ref[:, hs] / l_ref[h]).astype(o_ref.dtype)


def _fox_prompt(q, k, v, c_tm, c_rows, batch, seq):
    tq = _tile(seq, 512)
    nq = seq // tq
    kv_map = lambda b, qi, ki: (b * nq + jnp.minimum(ki, qi), 0)
    return pl.pallas_call(
        functools.partial(_fox_prompt_kernel, tq=tq),
        grid=(batch, nq, nq),
        in_specs=[pl.BlockSpec((tq, FOX_W), lambda b, qi, ki: (b * nq + qi, 0)),
                  pl.BlockSpec((tq, FOX_W), kv_map),
                  pl.BlockSpec((tq, FOX_W), kv_map),
                  pl.BlockSpec((tq, FOX_HEADS), lambda b, qi, ki: (b * nq + qi, 0)),
                  pl.BlockSpec((1, FOX_HEADS, tq), lambda b, qi, ki: (b, 0, jnp.minimum(ki, qi)))],
        out_specs=pl.BlockSpec((tq, FOX_W), lambda b, qi, ki: (b * nq + qi, 0)),
        out_shape=jax.ShapeDtypeStruct((batch * seq, FOX_W), BF16),
        scratch_shapes=[pltpu.VMEM((FOX_HEADS, tq, LANE), F32),
                        pltpu.VMEM((FOX_HEADS, tq, LANE), F32),
                        pltpu.VMEM((tq, FOX_W), F32)],
        compiler_params=_params("parallel", "parallel", "arbitrary"),
        name="fox_prompt",
    )(q, k, v, c_tm, c_rows)


def _fox_sample_kernel(q_ref, kn_ref, vn_ref, kc_ref, vc_ref, cq_ref, ck_ref, o_ref, *, past, s_new):
    scale = FOX_HD ** -0.5
    row = lax.broadcasted_iota(jnp.int32, (s_new, s_new), 0)
    col = lax.broadcasted_iota(jnp.int32, (s_new, s_new), 1)
    causal = col <= row
    for h in range(FOX_HEADS):
        hs = slice(h * FOX_HD, (h + 1) * FOX_HD)
        q = q_ref[:, hs].astype(BF16)
        cq = cq_ref[:, h:h + 1]
        s_c = _dot(q, kc_ref[0, :, hs].astype(BF16), _NT) * scale + (cq - ck_ref[0, h:h + 1, :past])
        s_n = _dot(q, kn_ref[:, hs].astype(BF16), _NT) * scale + (cq - ck_ref[0, h:h + 1, past:past + s_new])
        s_n = jnp.where(causal, s_n, NEG_INF)
        m = jnp.maximum(jnp.max(s_c, axis=-1, keepdims=True), jnp.max(s_n, axis=-1, keepdims=True))
        p_c = jnp.exp(s_c - m)
        p_n = jnp.exp(s_n - m)
        l = jnp.sum(p_c, axis=-1, keepdims=True) + jnp.sum(p_n, axis=-1, keepdims=True)
        o = _dot(p_c.astype(BF16), vc_ref[0, :, hs].astype(BF16)) + _dot(p_n.astype(BF16), vn_ref[:, hs].astype(BF16))
        o_ref[:, hs] = (o / l).astype(o_ref.dtype)


def _fox_sample(q, k, v, k_cache, v_cache, c_tm, c_rows, batch, s_new, past):
    tpad = c_rows.shape[-1]
    return pl.pallas_call(
        functools.partial(_fox_sample_kernel, past=past, s_new=s_new),
        grid=(batch,),
        in_specs=[pl.BlockSpec((s_new, FOX_W), lambda b: (b, 0)),
                  pl.BlockSpec((s_new, FOX_W), lambda b: (b, 0)),
                  pl.BlockSpec((s_new, FOX_W), lambda b: (b, 0)),
                  pl.BlockSpec((1, past, FOX_W), lambda b: (b, 0, 0)),
                  pl.BlockSpec((1, past, FOX_W), lambda b: (b, 0, 0)),
                  pl.BlockSpec((s_new, FOX_HEADS), lambda b: (b, 0)),
                  pl.BlockSpec((1, FOX_HEADS, tpad), lambda b: (b, 0, 0))],
        out_specs=pl.BlockSpec((s_new, FOX_W), lambda b: (b, 0)),
        out_shape=jax.ShapeDtypeStruct((batch * s_new, FOX_W), BF16),
        compiler_params=_params("parallel"),
        name="fox_sample",
    )(q, k, v, k_cache, v_cache, c_tm, c_rows)


def _head_sum(x, e, et):
    return _dot_exact_rhs(_dot_exact_rhs(x, e, 2), et, 2)


def _rwkv_prep_kernel(p_ref, prev_ref, mu_ref, w0_ref, a0_ref, kk_ref, ka_ref, wup_ref, aup_ref, gup_ref,
                      e_ref, et_ref, r_ref, lw_ref, k_ref, v_ref, a_ref, b_ref, g_ref, carry_ref, *, tr):
    @pl.when(pl.program_id(1) == 0)
    def _():
        carry_ref[...] = prev_ref[0]

    p = p_ref[0]
    rows = lax.broadcasted_iota(jnp.int32, (tr, 1), 0)
    p_prev = jnp.where(rows == 0, carry_ref[...], pltpu.roll(p, shift=1, axis=0))
    carry_ref[...] = p[tr - 1:tr, :]
    xs = p + (p_prev - p) * mu_ref[...]

    r = xs[:, :RWKV_W]
    k = xs[:, RWKV_W:2 * RWKV_W]
    v = xs[:, 2 * RWKV_W:3 * RWKV_W]
    d_wa = xs[:, RWKV_LORA_OFF:RWKV_LORA_OFF + LANE]
    d_g = xs[:, RWKV_LORA_OFF + LANE:RWKV_LORA_OFF + 2 * LANE]

    z_w = w0_ref[...] + _dot3(jnp.tanh(d_wa), wup_ref[...])
    log_decay = -jnp.exp(_log_sigmoid(z_w) - 0.5)
    a = _sigmoid(a0_ref[...] + _dot3(d_wa, aup_ref[...]))
    g = _dot3(_sigmoid(d_g), gup_ref[...])

    kk = k * kk_ref[...]
    kk = kk * lax.rsqrt(jnp.maximum(_head_sum(kk * kk, e_ref[...], et_ref[...]), 1e-24))
    k2 = k * (1.0 + (a - 1.0) * ka_ref[...])

    r_ref[...] = r
    lw_ref[...] = log_decay
    k_ref[...] = k2
    v_ref[...] = v
    a_ref[...] = -kk
    b_ref[...] = kk * a
    g_ref[...] = g


def _rwkv_prep(rw, prev, lw, nseq, t):
    tr = _tile(t, 256)
    nb = t // tr
    wide = lambda s, i: (0, 0)
    row_spec = pl.BlockSpec((tr, RWKV_W), lambda s, i: (s * nb + i, 0))
    out = jax.ShapeDtypeStruct((nseq * t, RWKV_W), F32)
    return pl.pallas_call(
        functools.partial(_rwkv_prep_kernel, tr=tr),
        grid=(nseq, nb),
        in_specs=[pl.BlockSpec((1, tr, RWKV_IN_W), lambda s, i: (s, i, 0)),
                  pl.BlockSpec((1, 1, RWKV_IN_W), lambda s, i: (s, 0, 0)),
                  pl.BlockSpec((1, RWKV_IN_W), wide),
                  pl.BlockSpec((1, RWKV_W), wide),
                  pl.BlockSpec((1, RWKV_W), wide),
                  pl.BlockSpec((1, RWKV_W), wide),
                  pl.BlockSpec((1, RWKV_W), wide),
                  pl.BlockSpec((LANE, RWKV_W), wide),
                  pl.BlockSpec((LANE, RWKV_W), wide),
                  pl.BlockSpec((LANE, RWKV_W), wide),
                  pl.BlockSpec((RWKV_W, LANE), wide),
                  pl.BlockSpec((LANE, RWKV_W), wide)],
        out_specs=[row_spec] * 7,
        out_shape=[out] * 7,
        scratch_shapes=[pltpu.VMEM((1, RWKV_IN_W), F32)],
        compiler_params=_params("parallel", "arbitrary"),
        name="rwkv_prep",
    )(rw.reshape(nseq, t, RWKV_IN_W), prev, lw["mu"], lw["w0"], lw["a0"], lw["k_k"], lw["k_a"],
      lw["w_up"], lw["a_up"], lw["g_up"], lw["e"], lw["et"])


def _rwkv_chunk_kernel(r_ref, lw_ref, k_ref, v_ref, a_ref, b_ref, h0_ref, y_ref, hout_ref, h_ref, *, c):
    ci = pl.program_id(1)

    @pl.when(ci == 0)
    def _():
        h_ref[...] = h0_ref[0]

    c2 = 2 * c
    row = lax.broadcasted_iota(jnp.int32, (c2, c2), 0)
    col = lax.broadcasted_iota(jnp.int32, (c2, c2), 1)
    lower = col <= row
    strict = col < row
    eye = jnp.where(row == col, 1.0, 0.0)
    rc = lax.broadcasted_iota(jnp.int32, (c, c), 0)
    cc = lax.broadcasted_iota(jnp.int32, (c, c), 1)
    tri = jnp.where(cc <= rc, 1.0, 0.0).astype(BF16)
    lane = lax.broadcasted_iota(jnp.int32, (c, LANE), 1)
    first = lane < RWKV_HD
    lane_sq = lax.broadcasted_iota(jnp.int32, (LANE, LANE), 1)
    row_sq = lax.broadcasted_iota(jnp.int32, (LANE, LANE), 0)
    diag_sq = lane_sq == row_sq

    def stack(x):
        return jnp.concatenate([jnp.where(first, x, 0.0), jnp.where(first, 0.0, x)], axis=0)

    for pr in range(RWKV_PAIRS):
        ls = slice(pr * LANE, (pr + 1) * LANE)
        r = r_ref[:, ls]
        lw = lw_ref[:, ls]
        k = k_ref[:, ls]
        v = v_ref[:, ls]
        a = a_ref[:, ls]
        b = b_ref[:, ls]
        h = h_ref[pr]

        cum = _dot_exact_rhs_left(tri, lw)
        total = cum[c - 1:c, :]
        grow = jnp.exp(-cum)
        rest = jnp.exp(total - cum)
        r_s = stack(r * jnp.exp(cum))
        a_s = stack(a * jnp.exp(cum - lw))
        k_hat = stack(k * grow)
        b_hat = stack(b * grow)
        k_bar = stack(k * rest)
        b_bar = stack(b * rest)
        v_s = stack(v)

        a_ak = jnp.where(strict, _dot3(a_s, k_hat, _NT), 0.0)
        a_ab = jnp.where(strict, _dot3(a_s, b_hat, _NT), 0.0)
        a_rk = jnp.where(lower, _dot3(r_s, k_hat, _NT), 0.0)
        a_rb = jnp.where(lower, _dot3(r_s, b_hat, _NT), 0.0)

        inv = eye + a_ab
        power = a_ab
        steps = max(c.bit_length() - 2, 0)
        for _ in range(steps):
            power = _dot3(power, power)
            inv = inv + _dot3(inv, power)

        u = _dot3(inv, _dot3(a_s, h) + _dot3(a_ak, v_s))
        y_s = _dot3(r_s, h) + _dot3(a_rk, v_s) + _dot3(a_rb, u)
        y_ref[:, ls] = y_s[:c] + y_s[c:]
        decay_diag = jnp.where(diag_sq, jnp.broadcast_to(jnp.exp(total), (LANE, LANE)), 0.0)
        h_new = _dot3(decay_diag, h) + _dot3(k_bar, v_s, _TN) + _dot3(b_bar, u, _TN)
        h_ref[pr] = h_new

    @pl.when(ci == pl.num_programs(1) - 1)
    def _():
        hout_ref[0] = h_ref[...]


def _dot_exact_rhs_left(e, x):
    hi, mid, lo = _split3(x)
    return _dot(e, hi) + _dot(e, mid) + _dot(e, lo)


def _rwkv_chunks(r, lwd, k, v, a, b, h0, nseq, t, c):
    nc = t // c
    row_spec = pl.BlockSpec((c, RWKV_W), lambda s, i: (s * nc + i, 0))
    st_spec = pl.BlockSpec((1, RWKV_PAIRS, LANE, LANE), lambda s, i: (s, 0, 0, 0))
    return pl.pallas_call(
        functools.partial(_rwkv_chunk_kernel, c=c),
        grid=(nseq, nc),
        in_specs=[row_spec] * 6 + [st_spec],
        out_specs=[row_spec, st_spec],
        out_shape=[jax.ShapeDtypeStruct((nseq * t, RWKV_W), F32),
                   jax.ShapeDtypeStruct((nseq, RWKV_PAIRS, LANE, LANE), F32)],
        scratch_shapes=[pltpu.VMEM((RWKV_PAIRS, LANE, LANE), F32)],
        compiler_params=_params("parallel", "arbitrary"),
        name="rwkv_chunks",
    )(r, lwd, k, v, a, b, h0)


def _rwkv_out_kernel(y_ref, r_ref, k_ref, v_ref, g_ref, gng_ref, gnb_ref, rk_ref, e_ref, et_ref, o_ref):
    e = e_ref[...]
    et = et_ref[...]
    y = y_ref[...]
    mu = _head_sum(y, e, et) * (1.0 / RWKV_HD)
    d = y - mu
    var = _head_sum(d * d, e, et) * (1.0 / RWKV_HD)
    yn = d * lax.rsqrt(var + RWKV_GN_EPS) * gng_ref[...] + gnb_ref[...]
    bonus = _head_sum(r_ref[...] * k_ref[...] * rk_ref[...], e, et) * v_ref[...]
    o_ref[...] = ((yn + bonus) * g_ref[...]).astype(o_ref.dtype)


def _rwkv_out(y, r, k, v, g, lw):
    m = y.shape[0]
    tr = _tile(m, 256)
    row_spec = pl.BlockSpec((tr, RWKV_W), lambda i: (i, 0))
    wide = lambda i: (0, 0)
    return pl.pallas_call(
        _rwkv_out_kernel,
        grid=(m // tr,),
        in_specs=[row_spec] * 5 + [pl.BlockSpec((1, RWKV_W), wide)] * 3
                 + [pl.BlockSpec((RWKV_W, LANE), wide), pl.BlockSpec((LANE, RWKV_W), wide)],
        out_specs=row_spec,
        out_shape=jax.ShapeDtypeStruct((m, RWKV_W), BF16),
        compiler_params=_params("parallel"),
        name="rwkv_out",
    )(y, r, k, v, g, lw["gn_g"], lw["gn_b"], lw["r_k"], lw["e"], lw["et"])


def _gmlp_kernel(gugv_ref, lng_ref, lnb_ref, ws_ref, bs_ref, o_ref, *vn_out):
    u = _gelu_tanh(gugv_ref[:, :GMLP_W])
    gv = _gelu_tanh(gugv_ref[:, GMLP_W:])
    mu = jnp.mean(gv, axis=-1, keepdims=True)
    d = gv - mu
    var = jnp.mean(d * d, axis=-1, keepdims=True)
    vn = d * lax.rsqrt(var + LN_EPS) * lng_ref[...] + lnb_ref[...]
    if vn_out:
        vn_out[0][...] = vn
    vb = vn.astype(BF16)
    for g in range(GMLP_GROUPS):
        gs = slice(g * GMLP_GC, (g + 1) * GMLP_GC)
        s = _dot(ws_ref[g], vb[:, gs]) + bs_ref[:, g:g + 1]
        o_ref[:, gs] = (u[:, gs] * s).astype(o_ref.dtype)


def _gmlp(gugv, lw, w_eff, b_eff, emit_vn):
    m = gugv.shape[0]
    tr = GMLP_CHUNK
    row_spec = pl.BlockSpec((tr, GMLP_W), lambda i: (i, 0))
    out_shape = [jax.ShapeDtypeStruct((m, GMLP_W), BF16)]
    out_specs = [row_spec]
    if emit_vn:
        out_shape.append(jax.ShapeDtypeStruct((m, GMLP_W), F32))
        out_specs.append(row_spec)
    res = pl.pallas_call(
        _gmlp_kernel,
        grid=(m // tr,),
        in_specs=[pl.BlockSpec((tr, 2 * GMLP_W), lambda i: (i, 0)),
                  pl.BlockSpec((1, GMLP_W), lambda i: (0, 0)),
                  pl.BlockSpec((1, GMLP_W), lambda i: (0, 0)),
                  pl.BlockSpec((GMLP_GROUPS, tr, tr), lambda i: (0, 0, 0)),
                  pl.BlockSpec((tr, GMLP_GROUPS), lambda i: (0, 0))],
        out_specs=out_specs,
        out_shape=out_shape,
        compiler_params=_params("parallel"),
        name="gmlp_sgu",
    )(gugv, lw["ln_g"], lw["ln_b"], w_eff, b_eff)
    return res if emit_vn else (res[0], None)


def _merge_kernel(oa_ref, ob_ref, oc_ref, ga_ref, gb_ref, gc_ref, wa_ref, wb_ref, wc_ref, o_ref):
    m = _sigmoid(ga_ref[...]) * _dot(oa_ref[...], wa_ref[...])
    m = m + _sigmoid(gb_ref[...]) * _dot(ob_ref[...], wb_ref[...])
    m = m + _sigmoid(gc_ref[...]) * _dot(oc_ref[...], wc_ref[...])
    o_ref[...] = m.astype(o_ref.dtype)


def _merge(o_a, o_b, o_c, gates, w_a, w_b, w_c):
    m, kw = o_a.shape
    d = w_a.shape[1]
    tm = _tile(m, 512)
    tn = _col_tile(d, 512)
    nj = d // tn
    o_spec = pl.BlockSpec((tm, kw), lambda i, j: (i, 0))
    w_spec = pl.BlockSpec((kw, tn), lambda i, j: (0, j))
    return pl.pallas_call(
        _merge_kernel,
        grid=(m // tm, nj),
        in_specs=[o_spec, o_spec, o_spec,
                  pl.BlockSpec((tm, tn), lambda i, j: (i, j)),
                  pl.BlockSpec((tm, tn), lambda i, j: (i, nj + j)),
                  pl.BlockSpec((tm, tn), lambda i, j: (i, 2 * nj + j)),
                  w_spec, w_spec, w_spec],
        out_specs=pl.BlockSpec((tm, tn), lambda i, j: (i, j)),
        out_shape=jax.ShapeDtypeStruct((m, d), BF16),
        compiler_params=_params("parallel", "parallel"),
        name="branch_merge",
    )(o_a, o_b, o_c, gates, gates, gates, w_a, w_b, w_c)


def _out_proj_kernel(m_ref, w_ref, x_ref, g_ref, o_ref):
    o_ref[...] = x_ref[...] + _rms(_dot(m_ref[...], w_ref[...]), g_ref[...])


def _out_proj(mm, w_out, x, g):
    m, d = x.shape
    tm = _tile(m, 512)
    return pl.pallas_call(
        _out_proj_kernel,
        grid=(m // tm,),
        in_specs=[pl.BlockSpec((tm, d), lambda i: (i, 0)),
                  pl.BlockSpec((d, d), lambda i: (0, 0)),
                  pl.BlockSpec((tm, d), lambda i: (i, 0)),
                  pl.BlockSpec((1, d), lambda i: (0, 0))],
        out_specs=pl.BlockSpec((tm, d), lambda i: (i, 0)),
        out_shape=jax.ShapeDtypeStruct((m, d), F32),
        compiler_params=_params("parallel"),
        name="out_proj_residual",
    )(mm, w_out, x, g.reshape(1, d))


def _ffn_kernel(x_ref, gpre_ref, gpost_ref, w1_ref, w3_ref, w2_ref, o_ref, h_ref, acc_ref):
    j = pl.program_id(1)

    @pl.when(j == 0)
    def _():
        h_ref[...] = _rms(x_ref[...], gpre_ref[...]).astype(BF16)
        acc_ref[...] = jnp.zeros_like(acc_ref)

    h = h_ref[...]
    a = _dot(h, w1_ref[...])
    b = _dot(h, w3_ref[...])
    gated = (a * _sigmoid(a)) * b
    acc_ref[...] += _dot(gated.astype(BF16), w2_ref[...])

    @pl.when(j == pl.num_programs(1) - 1)
    def _():
        o_ref[...] = x_ref[...] + _rms(acc_ref[...], gpost_ref[...])


def _ffn(x, g_pre, g_post, w1, w3, w2):
    m, d = x.shape
    dff = w1.shape[1]
    tm = _tile(m, 512)
    tf = _col_tile(dff, 512)
    return pl.pallas_call(
        _ffn_kernel,
        grid=(m // tm, dff // tf),
        in_specs=[pl.BlockSpec((tm, d), lambda i, j: (i, 0)),
                  pl.BlockSpec((1, d), lambda i, j: (0, 0)),
                  pl.BlockSpec((1, d), lambda i, j: (0, 0)),
                  pl.BlockSpec((d, tf), lambda i, j: (0, j)),
                  pl.BlockSpec((d, tf), lambda i, j: (0, j)),
                  pl.BlockSpec((tf, d), lambda i, j: (j, 0))],
        out_specs=pl.BlockSpec((tm, d), lambda i, j: (i, 0)),
        out_shape=jax.ShapeDtypeStruct((m, d), F32),
        scratch_shapes=[pltpu.VMEM((tm, d), BF16), pltpu.VMEM((tm, d), F32)],
        compiler_params=_params("parallel", "arbitrary"),
        name="ffn_swiglu",
    )(x, g_pre.reshape(1, d), g_post.reshape(1, d), w1, w3, w2)


def _pad_rows(w, rows, offset):
    out = jnp.zeros((rows, w.shape[1]), w.dtype)
    return out.at[offset:offset + w.shape[0]].set(w)


def _layer_weights(l, d, w_in, fox_bf, rwkv_mu, rwkv_w0, rwkv_w_up, rwkv_a0, rwkv_a_up, rwkv_g_up, rwkv_k_k,
                   rwkv_k_a, rwkv_r_k, rwkv_gn_g, rwkv_gn_b, gmlp_ln_g, gmlp_ln_b, gmlp_w_s, gmlp_b_s,
                   w_br_fox, w_br_rwkv, w_br_gmlp, w_out, ffn_w1, ffn_w3, ffn_w2, s_new):
    w = w_in[l]
    o_ff = 3 * FOX_W
    o_r = o_ff + FOX_HEADS
    o_g = o_r + RWKV_IN_W
    o_gate = o_g + 2 * GMLP_W
    head_of_lane = jnp.arange(RWKV_W) // RWKV_HD
    e = (head_of_lane[:, None] == jnp.arange(LANE)[None, :]).astype(BF16)
    tril_full = jnp.tril(gmlp_w_s[l])
    blk = jnp.tril(gmlp_w_s[l][:, :s_new, :s_new])
    reps = GMLP_CHUNK // s_new
    eye = jnp.eye(reps, dtype=F32)
    w_sample = jnp.einsum("ab,gij->gaibj", eye, blk).reshape(GMLP_GROUPS, GMLP_CHUNK, GMLP_CHUNK)
    return {
        "wq": w[:, :FOX_W].astype(BF16),
        "wk": w[:, FOX_W:2 * FOX_W].astype(BF16),
        "wv": w[:, 2 * FOX_W:o_ff].astype(BF16),
        "w_ff": jnp.zeros((d, LANE), BF16).at[:, :FOX_HEADS].set(w[:, o_ff:o_r].astype(BF16)),
        "b_ff": jnp.zeros((1, LANE), F32).at[0, :FOX_HEADS].set(fox_bf[l]),
        "w_rwkv": w[:, o_r:o_g].astype(BF16),
        "w_gmlp": w[:, o_g:o_gate].astype(BF16),
        "w_gate": w[:, o_gate:].astype(BF16),
        "rwkv": {
            "mu": rwkv_mu[l].reshape(1, RWKV_IN_W),
            "w0": rwkv_w0[l].reshape(1, RWKV_W),
            "a0": rwkv_a0[l].reshape(1, RWKV_W),
            "k_k": rwkv_k_k[l].reshape(1, RWKV_W),
            "k_a": rwkv_k_a[l].reshape(1, RWKV_W),
            "r_k": rwkv_r_k[l].reshape(1, RWKV_W),
            "gn_g": rwkv_gn_g[l].reshape(1, RWKV_W),
            "gn_b": rwkv_gn_b[l].reshape(1, RWKV_W),
            "w_up": _pad_rows(rwkv_w_up[l], LANE, 0),
            "a_up": _pad_rows(rwkv_a_up[l], LANE, rwkv_w_up.shape[1]),
            "g_up": rwkv_g_up[l],
            "e": e,
            "et": e.T,
        },
        "gmlp": {"ln_g": gmlp_ln_g[l].reshape(1, GMLP_W), "ln_b": gmlp_ln_b[l].reshape(1, GMLP_W)},
        "gmlp_w_prompt": tril_full.astype(BF16),
        "gmlp_b_prompt": gmlp_b_s[l].T,
        "gmlp_w_sample": w_sample.astype(BF16),
        "gmlp_b_sample": jnp.tile(gmlp_b_s[l][:, :s_new].T, (reps, 1)),
        "w_br_fox": w_br_fox[l].astype(BF16),
        "w_br_rwkv": w_br_rwkv[l].astype(BF16),
        "w_br_gmlp": w_br_gmlp[l].astype(BF16),
        "w_out": w_out[l].astype(BF16),
        "ffn_w1": ffn_w1[l].astype(BF16),
        "ffn_w3": ffn_w3[l].astype(BF16),
        "ffn_w2": ffn_w2[l].astype(BF16),
    }


def _states_to_pairs(state):
    n = state.shape[0]
    h = jnp.swapaxes(state, -1, -2).reshape(n, RWKV_PAIRS, 2, RWKV_HD, RWKV_HD)
    eye = jnp.eye(2, dtype=state.dtype)
    return jnp.einsum("npakv,ab->npakbv", h, eye).reshape(n, RWKV_PAIRS, LANE, LANE)


def _pairs_to_states(hp):
    n = hp.shape[0]
    h = hp.reshape(n, RWKV_PAIRS, 2, RWKV_HD, 2, RWKV_HD)
    diag = jnp.stack([h[:, :, 0, :, 0, :], h[:, :, 1, :, 1, :]], axis=2)
    return jnp.swapaxes(diag, -1, -2).reshape(n, RWKV_HEADS, RWKV_HD, RWKV_HD)


def _trunk_layer(x, lw, pre_mix_g, post_mix_g, pre_ffn_g, post_ffn_g, nseq, t, cache):
    m, d = x.shape
    h = _rmsnorm_cast(x, pre_mix_g)
    q = _matmul(h, lw["wq"])
    k = _matmul(h, lw["wk"])
    v = _matmul(h, lw["wv"])
    logf = _forget_logits(h, lw["w_ff"], lw["b_ff"])[:, :FOX_HEADS]
    rw = _matmul(h, lw["w_rwkv"])
    gugv = _matmul(h, lw["w_gmlp"])
    gates = _matmul(h, lw["w_gate"])

    logf_rows = logf.reshape(nseq, t, FOX_HEADS).transpose(0, 2, 1)
    if cache is None:
        c_rows = _cumsum_lanes(logf_rows.reshape(nseq * FOX_HEADS, t), _tile(t, 256))
        c_rows = c_rows.reshape(nseq, FOX_HEADS, t)
        c_tm = c_rows.transpose(0, 2, 1).reshape(m, FOX_HEADS)
        o_a = _fox_prompt(q, k, v, c_tm, c_rows, nseq, t)
        prev = jnp.zeros((nseq, 1, RWKV_IN_W), F32)
        h0 = jnp.zeros((nseq, RWKV_PAIRS, LANE, LANE), F32)
        chunk = 64
        gm_w, gm_b = lw["gmlp_w_prompt"], lw["gmlp_b_prompt"]
    else:
        past = cache["k"].shape[1]
        total = past + t
        tpad = -(-total // LANE) * LANE
        lf_all = jnp.concatenate([cache["logf"].astype(F32).transpose(0, 2, 1), logf_rows], axis=-1)
        lf_all = jnp.pad(lf_all, ((0, 0), (0, 0), (0, tpad - total)))
        c_rows = _cumsum_lanes(lf_all.reshape(nseq * FOX_HEADS, tpad), LANE).reshape(nseq, FOX_HEADS, tpad)
        c_tm = c_rows[:, :, past:total].transpose(0, 2, 1).reshape(m, FOX_HEADS)
        o_a = _fox_sample(q, k, v, cache["k"].reshape(nseq, past, FOX_W), cache["v"].reshape(nseq, past, FOX_W),
                          c_tm, c_rows, nseq, t, past)
        prev = cache["shift"]
        h0 = _states_to_pairs(cache["state"])
        chunk = t
        gm_w, gm_b = lw["gmlp_w_sample"], lw["gmlp_b_sample"]

    r, lwd, k2, v2, a_s, b_s, g = _rwkv_prep(rw, prev, lw["rwkv"], nseq, t)
    y, h_fin = _rwkv_chunks(r, lwd, k2, v2, a_s, b_s, h0, nseq, t, chunk)
    o_b = _rwkv_out(y, r, k2, v2, g, lw["rwkv"])
    s_new = _pairs_to_states(h_fin)
    shift_new = rw.reshape(nseq, t, RWKV_IN_W)[:, t - 1:t, :]

    o_c, vn = _gmlp(gugv, lw["gmlp"], gm_w, gm_b, cache is not None)

    mm = _merge(o_a, o_b, o_c, gates, lw["w_br_fox"], lw["w_br_rwkv"], lw["w_br_gmlp"])
    x = _out_proj(mm, lw["w_out"], x, post_mix_g)
    x = _ffn(x, pre_ffn_g, post_ffn_g, lw["ffn_w1"], lw["ffn_w3"], lw["ffn_w2"])
    return x, (k, v, logf, s_new, shift_new, vn)


def kernel(x_prompt, x_sample, cache_fox_k, cache_fox_v, cache_fox_logf, state_rwkv, state_rwkv_shift, pre_mix_g, w_in, fox_bf, rwkv_mu, rwkv_w0, rwkv_w_up, rwkv_a0, rwkv_a_up, rwkv_g_up, rwkv_k_k, rwkv_k_a, rwkv_r_k, rwkv_gn_g, rwkv_gn_b, gmlp_ln_g, gmlp_ln_b, gmlp_w_s, gmlp_b_s, w_br_fox, w_br_rwkv, w_br_gmlp, w_out, post_mix_g, pre_ffn_g, ffn_w1, ffn_w3, ffn_w2, post_ffn_g):
    batch, seq, d = x_prompt.shape
    dec_batch, dec_seq, _ = x_sample.shape
    depth = w_in.shape[0]
    assert rwkv_w_up.shape[1] + rwkv_a_up.shape[1] == LANE and rwkv_g_up.shape[1] == LANE
    assert seq % GMLP_CHUNK == 0 and GMLP_CHUNK % dec_seq == 0 and (dec_batch * dec_seq) % GMLP_CHUNK == 0

    xp = x_prompt.reshape(batch * seq, d)
    xs = x_sample.reshape(dec_batch * dec_seq, d)
    outs_p = [[] for _ in range(5)]
    outs_s = [[] for _ in range(6)]
    for l in range(depth):
        lw = _layer_weights(l, d, w_in, fox_bf, rwkv_mu, rwkv_w0, rwkv_w_up, rwkv_a0, rwkv_a_up, rwkv_g_up,
                            rwkv_k_k, rwkv_k_a, rwkv_r_k, rwkv_gn_g, rwkv_gn_b, gmlp_ln_g, gmlp_ln_b, gmlp_w_s,
                            gmlp_b_s, w_br_fox, w_br_rwkv, w_br_gmlp, w_out, ffn_w1, ffn_w3, ffn_w2, dec_seq)
        norms = (pre_mix_g[l], post_mix_g[l], pre_ffn_g[l], post_ffn_g[l])
        xp, (k_p, v_p, lf_p, s_p, sh_p, _) = _trunk_layer(xp, lw, *norms, batch, seq, None)
        cache = {"k": cache_fox_k[l], "v": cache_fox_v[l], "logf": cache_fox_logf[l],
                 "state": state_rwkv[l], "shift": state_rwkv_shift[l]}
        xs, (k_s, v_s, lf_s, s_s, sh_s, vn_s) = _trunk_layer(xs, lw, *norms, dec_batch, dec_seq, cache)
        for acc, val in zip(outs_p, (k_p.reshape(batch, seq, FOX_HEADS, FOX_HD), v_p.reshape(batch, seq, FOX_HEADS, FOX_HD),
                                     lf_p.reshape(batch, seq, FOX_HEADS), s_p, sh_p)):
            acc.append(val)
        for acc, val in zip(outs_s, (k_s.reshape(dec_batch, dec_seq, FOX_HEADS, FOX_HD),
                                     v_s.reshape(dec_batch, dec_seq, FOX_HEADS, FOX_HD),
                                     lf_s.reshape(dec_batch, dec_seq, FOX_HEADS), s_s, sh_s,
                                     vn_s.reshape(dec_batch, dec_seq, GMLP_W))):
            acc.append(val)
    return (xp.reshape(batch, seq, d), xs.reshape(dec_batch, dec_seq, d),
            *[jnp.stack(o) for o in outs_p], *[jnp.stack(o) for o in outs_s])
```

```python
import functools
import math

import jax
import jax.numpy as jnp
from jax import lax
from jax.experimental import pallas as pl
from jax.experimental.pallas import tpu as pltpu

F32 = jnp.float32
BF16 = jnp.bfloat16

RMS_EPS = 1e-6
LN_EPS = 1e-5
RWKV_GN_EPS = 64e-5
NEG_INF = -1e30
LOG2E = math.log2(math.e)

LANE = 128
FOX_HEADS = 8
FOX_HD = 128
FOX_W = FOX_HEADS * FOX_HD
FOX_Q_SCALE = FOX_HD ** -0.5 * LOG2E
RWKV_HEADS = 16
RWKV_HD = 64
RWKV_W = RWKV_HEADS * RWKV_HD
RWKV_PAIRS = RWKV_W // LANE
RWKV_LORA_OFF = 3 * RWKV_W
RWKV_IN_W = 3 * RWKV_W + 2 * LANE
GMLP_W = 1024
GMLP_GROUPS = 8
GMLP_GC = GMLP_W // GMLP_GROUPS
GMLP_CHUNK = 128
N_BRANCH = 3

VMEM_LIMIT = 48 * 1024 * 1024


def _params(*sem):
    return pltpu.CompilerParams(dimension_semantics=sem, vmem_limit_bytes=VMEM_LIMIT)


def _tile(n, pref):
    if n <= pref:
        return n
    t = pref
    while n % t:
        t -= 8
    return t


def _col_tile(n, pref):
    assert n % LANE == 0
    k = n // LANE
    best = 1
    for d in range(1, k + 1):
        if k % d == 0 and d * LANE <= pref:
            best = d
    return best * LANE


def _split2(x):
    hi = x.astype(BF16)
    lo = (x - hi.astype(F32)).astype(BF16)
    return hi, lo


def _split3(x):
    hi = x.astype(BF16)
    r = x - hi.astype(F32)
    mid = r.astype(BF16)
    lo = (r - mid.astype(F32)).astype(BF16)
    return hi, mid, lo


_NN = (((1,), (0,)), ((), ()))
_NT = (((1,), (1,)), ((), ()))
_TN = (((0,), (0,)), ((), ()))


def _dot(a, b, dims=_NN):
    return lax.dot_general(a, b, dims, preferred_element_type=F32)


def _dot3(x, y, dims=_NN):
    xh, xl = _split2(x)
    yh, yl = _split2(y)
    return _dot(xh, yh, dims) + _dot(xh, yl, dims) + _dot(xl, yh, dims)


def _dot_exact_rhs(x, e, parts):
    ps = _split3(x) if parts == 3 else _split2(x)
    out = _dot(ps[0], e)
    for p in ps[1:]:
        out = out + _dot(p, e)
    return out


def _sigmoid(x):
    return 1.0 / (1.0 + jnp.exp(-x))


def _log_sigmoid(x):
    return jnp.minimum(x, 0.0) - jnp.log(1.0 + jnp.exp(-jnp.abs(x)))


def _gelu_tanh(x):
    return 0.5 * x * (1.0 + jnp.tanh(0.7978845608028654 * (x + 0.044715 * (x * x * x))))


def _rms(x, g):
    ms = jnp.mean(x * x, axis=-1, keepdims=True)
    return x * lax.rsqrt(ms + RMS_EPS) * g


def _rmsnorm_cast_kernel(x_ref, g_ref, o_ref):
    o_ref[...] = _rms(x_ref[...], g_ref[...]).astype(o_ref.dtype)


def _rmsnorm_cast(x, g):
    m, d = x.shape
    tm = _tile(m, 512)
    return pl.pallas_call(
        _rmsnorm_cast_kernel,
        grid=(m // tm,),
        in_specs=[pl.BlockSpec((tm, d), lambda i: (i, 0)),
                  pl.BlockSpec((1, d), lambda i: (0, 0))],
        out_specs=pl.BlockSpec((tm, d), lambda i: (i, 0)),
        out_shape=jax.ShapeDtypeStruct((m, d), BF16),
        compiler_params=_params("parallel"),
        name="rmsnorm_cast",
    )(x, g.reshape(1, d))


def _matmul_kernel(a_ref, w_ref, *o_refs, scale):
    acc = _dot(a_ref[...], w_ref[...])
    if scale is not None:
        acc = acc * scale
    for o_ref in o_refs:
        o_ref[...] = acc.astype(o_ref.dtype)


def _matmul(a, w, out_dtypes=(F32,), scale=None, tm_pref=1024, tn_pref=512):
    m, k = a.shape
    n = w.shape[1]
    tm = _tile(m, tm_pref)
    tn = _col_tile(n, tn_pref)
    o_spec = pl.BlockSpec((tm, tn), lambda i, j: (i, j))
    res = pl.pallas_call(
        functools.partial(_matmul_kernel, scale=scale),
        grid=(m // tm, n // tn),
        in_specs=[pl.BlockSpec((tm, k), lambda i, j: (i, 0)),
                  pl.BlockSpec((k, tn), lambda i, j: (0, j))],
        out_specs=[o_spec] * len(out_dtypes),
        out_shape=[jax.ShapeDtypeStruct((m, n), dt) for dt in out_dtypes],
        compiler_params=_params("parallel", "parallel"),
        name="proj_matmul",
    )(a, w)
    return res[0] if len(out_dtypes) == 1 else res


def _logf_kernel(a_ref, w_ref, b_ref, o_ref):
    o_ref[...] = _log_sigmoid(_dot(a_ref[...], w_ref[...]) + b_ref[...])


def _forget_logits(h, w_ff, b_ff):
    m, k = h.shape
    tm = _tile(m, 1024)
    return pl.pallas_call(
        _logf_kernel,
        grid=(m // tm,),
        in_specs=[pl.BlockSpec((tm, k), lambda i: (i, 0)),
                  pl.BlockSpec((k, LANE), lambda i: (0, 0)),
                  pl.BlockSpec((1, LANE), lambda i: (0, 0))],
        out_specs=pl.BlockSpec((tm, LANE), lambda i: (i, 0)),
        out_shape=jax.ShapeDtypeStruct((m, LANE), F32),
        compiler_params=_params("parallel"),
        name="forget_logits",
    )(h, w_ff, b_ff)


def _cumsum_kernel(x_ref, o_ref, carry_ref, *, tb, scale):
    @pl.when(pl.program_id(0) == 0)
    def _():
        carry_ref[...] = jnp.zeros_like(carry_ref)

    row = lax.broadcasted_iota(jnp.int32, (tb, tb), 0)
    col = lax.broadcasted_iota(jnp.int32, (tb, tb), 1)
    upper = jnp.where(row <= col, 1.0, 0.0).astype(BF16)
    c = _dot_exact_rhs(x_ref[...], upper, 3) + carry_ref[:, :1]
    o_ref[...] = c * scale
    carry_ref[...] = jnp.broadcast_to(c[:, tb - 1:tb], carry_ref.shape)


def _cumsum_lanes(x, tb, scale):
    r, t = x.shape
    return pl.pallas_call(
        functools.partial(_cumsum_kernel, tb=tb, scale=scale),
        grid=(t // tb,),
        in_specs=[pl.BlockSpec((r, tb), lambda i: (0, i))],
        out_specs=pl.BlockSpec((r, tb), lambda i: (0, i)),
        out_shape=jax.ShapeDtypeStruct((r, t), F32),
        scratch_shapes=[pltpu.VMEM((r, LANE), F32)],
        compiler_params=_params("arbitrary"),
        name="logf_cumsum",
    )(x)


def _fox_bias_kernel(c_ref, xq_ref, xk_ref):
    tr = c_ref.shape[0]
    lane = lax.broadcasted_iota(jnp.int32, (tr, LANE), 1)
    for h in range(FOX_HEADS):
        hi, mid, lo = [p.astype(F32) for p in _split3(c_ref[:, h:h + 1])]
        ones = jnp.where(lane < 6, 1.0, 0.0)
        xq = jnp.where(lane == 0, hi, jnp.where(lane == 1, mid, jnp.where(lane == 2, lo, ones)))
        xk = jnp.where(lane == 3, -hi, jnp.where(lane == 4, -mid, jnp.where(lane == 5, -lo, ones)))
        xq_ref[:, h * LANE:(h + 1) * LANE] = xq.astype(BF16)
        xk_ref[:, h * LANE:(h + 1) * LANE] = xk.astype(BF16)


def _fox_bias_columns(c_tm):
    m = c_tm.shape[0]
    tr = _tile(m, 512)
    spec = pl.BlockSpec((tr, FOX_W), lambda i: (i, 0))
    return pl.pallas_call(
        _fox_bias_kernel,
        grid=(m // tr,),
        in_specs=[pl.BlockSpec((tr, FOX_HEADS), lambda i: (i, 0))],
        out_specs=[spec, spec],
        out_shape=[jax.ShapeDtypeStruct((m, FOX_W), BF16)] * 2,
        compiler_params=_params("parallel"),
        name="fox_bias_columns",
    )(c_tm)


def _fox_prompt_kernel(q_ref, xq_ref, k_ref, xk_ref, v_ref, o_ref, m_ref, l_ref, acc_ref, *, tq):
    qi = pl.program_id(1)
    ki = pl.program_id(2)

    @pl.when(ki == 0)
    def _():
        m_ref[...] = jnp.full_like(m_ref, NEG_INF)
        l_ref[...] = jnp.zeros_like(l_ref)
        acc_ref[...] = jnp.zeros_like(acc_ref)

    def update(diagonal):
        if diagonal:
            row = lax.broadcasted_iota(jnp.int32, (tq, tq), 0)
            col = lax.broadcasted_iota(jnp.int32, (tq, tq), 1)
            causal = col <= row
        heads = range(FOX_HEADS)
        cols = [slice(h * FOX_HD, (h + 1) * FOX_HD) for h in heads]
        s = [_dot(jnp.concatenate([q_ref[:, hs], xq_ref[:, hs]], axis=1),
                  jnp.concatenate([k_ref[:, hs], xk_ref[:, hs]], axis=1), _NT) for hs in cols]
        if diagonal:
            s = [jnp.where(causal, x, NEG_INF) for x in s]
        m_prev = [m_ref[h] for h in heads]
        m_new = [jnp.maximum(mp, jnp.max(x, axis=-1, keepdims=True)) for mp, x in zip(m_prev, s)]
        p = [jnp.exp2(x - jnp.concatenate([mn] * (tq // LANE), axis=1)) for x, mn in zip(s, m_new)]
        p_sum = [jnp.sum(x, axis=-1, keepdims=True) for x in p]
        pv = [_dot(x.astype(BF16), v_ref[:, hs]) for x, hs in zip(p, cols)]
        for h in heads:
            alpha = jnp.exp2(m_prev[h] - m_new[h])
            l_ref[h] = alpha * l_ref[h] + p_sum[h]
            acc_ref[:, cols[h]] = alpha * acc_ref[:, cols[h]] + pv[h]
            m_ref[h] = m_new[h]

    @pl.when(ki < qi)
    def _():
        update(False)

    @pl.when(ki == qi)
    def _():
        update(True)
        for h in range(FOX_HEADS):
            hs = slice(h * FOX_HD, (h + 1) * FOX_HD)
            o_ref[:, hs] = (acc_ref[:, hs] / l_ref[h]).astype(o_ref.dtype)


def _fox_prompt(q, xq, k, xk, v, batch, seq):
    tq = _tile(seq, 512)
    assert tq % LANE == 0
    nq = seq // tq
    q_map = lambda b, qi, ki: (b * nq + qi, 0)
    kv_map = lambda b, qi, ki: (b * nq + jnp.minimum(ki, qi), 0)
    return pl.pallas_call(
        functools.partial(_fox_prompt_kernel, tq=tq),
        grid=(batch, nq, nq),
        in_specs=[pl.BlockSpec((tq, FOX_W), q_map),
                  pl.BlockSpec((tq, FOX_W), q_map),
                  pl.BlockSpec((tq, FOX_W), kv_map),
                  pl.BlockSpec((tq, FOX_W), kv_map),
                  pl.BlockSpec((tq, FOX_W), kv_map)],
        out_specs=pl.BlockSpec((tq, FOX_W), q_map),
        out_shape=jax.ShapeDtypeStruct((batch * seq, FOX_W), BF16),
        scratch_shapes=[pltpu.VMEM((FOX_HEADS, tq, LANE), F32),
                        pltpu.VMEM((FOX_HEADS, tq, LANE), F32),
                        pltpu.VMEM((tq, FOX_W), F32)],
        compiler_params=_params("parallel", "parallel", "arbitrary"),
        name="fox_prompt",
    )(q, xq, k, xk, v)


def _fox_sample_kernel(q_ref, kn_ref, vn_ref, kc_ref, vc_ref, cq_ref, ck_ref, o_ref, *, past, s_new):
    row = lax.broadcasted_iota(jnp.int32, (s_new, s_new), 0)
    col = lax.broadcasted_iota(jnp.int32, (s_new, s_new), 1)
    causal = col <= row
    for h in range(FOX_HEADS):
        hs = slice(h * FOX_HD, (h + 1) * FOX_HD)
        q = q_ref[:, hs]
        cq = cq_ref[:, h:h + 1]
        s_c = _dot(q, kc_ref[0, 0, :, hs].astype(BF16), _NT) + (cq - ck_ref[0, h:h + 1, :past])
        s_n = _dot(q, kn_ref[:, hs], _NT) + (cq - ck_ref[0, h:h + 1, past:past + s_new])
        s_n = jnp.where(causal, s_n, NEG_INF)
        m = jnp.maximum(jnp.max(s_c, axis=-1, keepdims=True), jnp.max(s_n, axis=-1, keepdims=True))
        p_c = jnp.exp2(s_c - m)
        p_n = jnp.exp2(s_n - m)
        l = jnp.sum(p_c, axis=-1, keepdims=True) + jnp.sum(p_n, axis=-1, keepdims=True)
        o = _dot(p_c.astype(BF16), vc_ref[0, 0, :, hs].astype(BF16)) + _dot(p_n.astype(BF16), vn_ref[:, hs])
        o_ref[:, hs] = (o / l).astype(o_ref.dtype)


def _fox_sample(q, k, v, k_cache, v_cache, layer, c_tm, c_rows, batch, s_new, past):
    tpad = c_rows.shape[-1]
    new_spec = pl.BlockSpec((s_new, FOX_W), lambda b: (b, 0))
    cache_spec = pl.BlockSpec((1, 1, past, FOX_W), lambda b: (layer, b, 0, 0))
    return pl.pallas_call(
        functools.partial(_fox_sample_kernel, past=past, s_new=s_new),
        grid=(batch,),
        in_specs=[new_spec, new_spec, new_spec, cache_spec, cache_spec,
                  pl.BlockSpec((s_new, FOX_HEADS), lambda b: (b, 0)),
                  pl.BlockSpec((1, FOX_HEADS, tpad), lambda b: (b, 0, 0))],
        out_specs=new_spec,
        out_shape=jax.ShapeDtypeStruct((batch * s_new, FOX_W), BF16),
        compiler_params=_params("parallel"),
        name="fox_sample",
    )(q, k, v, k_cache, v_cache, c_tm, c_rows)


def _head_sum(x, e, et):
    return _dot_exact_rhs(_dot_exact_rhs(x, e, 2), et, 2)


def _rwkv_prep_kernel(p_ref, prev_ref, mu_ref, w0_ref, a0_ref, kk_ref, ka_ref, wup_ref, aup_ref, gup_ref,
                      e_ref, et_ref, r_ref, lw_ref, k_ref, v_ref, a_ref, b_ref, g_ref, carry_ref, *, tr):
    @pl.when(pl.program_id(1) == 0)
    def _():
        carry_ref[...] = prev_ref[0]

    p = p_ref[0]
    rows = lax.broadcasted_iota(jnp.int32, (tr, 1), 0)
    p_prev = jnp.where(rows == 0, carry_ref[...], pltpu.roll(p, shift=1, axis=0))
    carry_ref[...] = p[tr - 1:tr, :]
    xs = p + (p_prev - p) * mu_ref[...]

    r = xs[:, :RWKV_W]
    k = xs[:, RWKV_W:2 * RWKV_W]
    v = xs[:, 2 * RWKV_W:3 * RWKV_W]
    d_wa = xs[:, RWKV_LORA_OFF:RWKV_LORA_OFF + LANE]
    d_g = xs[:, RWKV_LORA_OFF + LANE:RWKV_LORA_OFF + 2 * LANE]

    z_w = w0_ref[...] + _dot3(jnp.tanh(d_wa), wup_ref[...])
    log_decay = -jnp.exp(_log_sigmoid(z_w) - 0.5)
    a = _sigmoid(a0_ref[...] + _dot3(d_wa, aup_ref[...]))
    g = _dot3(_sigmoid(d_g), gup_ref[...])

    kk = k * kk_ref[...]
    kk = kk * lax.rsqrt(jnp.maximum(_head_sum(kk * kk, e_ref[...], et_ref[...]), 1e-24))
    k2 = k * (1.0 + (a - 1.0) * ka_ref[...])

    r_ref[...] = r
    lw_ref[...] = log_decay
    k_ref[...] = k2
    v_ref[...] = v
    a_ref[...] = -kk
    b_ref[...] = kk * a
    g_ref[...] = g


def _rwkv_prep(rw, prev, lw, nseq, t):
    tr = _tile(t, 256)
    nb = t // tr
    wide = lambda s, i: (0, 0)
    row_spec = pl.BlockSpec((tr, RWKV_W), lambda s, i: (s * nb + i, 0))
    out = jax.ShapeDtypeStruct((nseq * t, RWKV_W), F32)
    return pl.pallas_call(
        functools.partial(_rwkv_prep_kernel, tr=tr),
        grid=(nseq, nb),
        in_specs=[pl.BlockSpec((1, tr, RWKV_IN_W), lambda s, i: (s, i, 0)),
                  pl.BlockSpec((1, 1, RWKV_IN_W), lambda s, i: (s, 0, 0)),
                  pl.BlockSpec((1, RWKV_IN_W), wide),
                  pl.BlockSpec((1, RWKV_W), wide),
                  pl.BlockSpec((1, RWKV_W), wide),
                  pl.BlockSpec((1, RWKV_W), wide),
                  pl.BlockSpec((1, RWKV_W), wide),
                  pl.BlockSpec((LANE, RWKV_W), wide),
                  pl.BlockSpec((LANE, RWKV_W), wide),
                  pl.BlockSpec((LANE, RWKV_W), wide),
                  pl.BlockSpec((RWKV_W, LANE), wide),
                  pl.BlockSpec((LANE, RWKV_W), wide)],
        out_specs=[row_spec] * 7,
        out_shape=[out] * 7,
        scratch_shapes=[pltpu.VMEM((1, RWKV_IN_W), F32)],
        compiler_params=_params("parallel", "arbitrary"),
        name="rwkv_prep",
    )(rw.reshape(nseq, t, RWKV_IN_W), prev, lw["mu"], lw["w0"], lw["a0"], lw["k_k"], lw["k_a"],
      lw["w_up"], lw["a_up"], lw["g_up"], lw["e"], lw["et"])


def _rwkv_chunk_kernel(r_ref, lw_ref, k_ref, v_ref, a_ref, b_ref, s0_ref, y_ref, sout_ref, s_ref, *, c):
    ci = pl.program_id(1)

    @pl.when(ci == 0)
    def _():
        s_ref[...] = s0_ref[0]

    c2 = 2 * c
    row = lax.broadcasted_iota(jnp.int32, (c2, c2), 0)
    col = lax.broadcasted_iota(jnp.int32, (c2, c2), 1)
    lower = col <= row
    strict = col < row
    eye = jnp.where(row == col, 1.0, 0.0)
    rc = lax.broadcasted_iota(jnp.int32, (c, c), 0)
    cc = lax.broadcasted_iota(jnp.int32, (c, c), 1)
    tri = jnp.where(cc <= rc, 1.0, 0.0).astype(BF16)
    lane = lax.broadcasted_iota(jnp.int32, (c, LANE), 1)
    first = lane < RWKV_HD
    steps = max(c.bit_length() - 2, 0)

    def stack(x):
        return jnp.concatenate([jnp.where(first, x, 0.0), jnp.where(first, 0.0, x)], axis=0)

    pairs = range(RWKV_PAIRS)
    lanes = [slice(pr * LANE, (pr + 1) * LANE) for pr in pairs]
    s_mats = [s_ref[pr] for pr in pairs]
    lws = [lw_ref[:, ls] for ls in lanes]
    cum3 = [_dot(tri, jnp.concatenate(_split3(lw), axis=1)) for lw in lws]
    cums = [x[:, :LANE] + x[:, LANE:2 * LANE] + x[:, 2 * LANE:] for x in cum3]
    totals = [cum[c - 1:c, :] for cum in cums]

    lhs, rhs, bar, v_s = [], [], [], []
    for ls, lw, cum, total in zip(lanes, lws, cums, totals):
        grow = jnp.exp(-cum)
        rest = jnp.exp(total - cum)
        k = k_ref[:, ls]
        b = b_ref[:, ls]
        lhs.append(jnp.concatenate([stack(a_ref[:, ls] * jnp.exp(cum - lw)),
                                    stack(r_ref[:, ls] * jnp.exp(cum))], axis=0).astype(BF16))
        rhs.append(jnp.concatenate([stack(k * grow), stack(b * grow)], axis=0).astype(BF16))
        bar.append(jnp.concatenate([stack(k * rest), stack(b * rest)], axis=0).astype(BF16))
        v_s.append(stack(v_ref[:, ls]))

    amats = [_dot(x, y, _NT) for x, y in zip(lhs, rhs)]
    sxs = [_dot(x, s.astype(BF16), _NT) for x, s in zip(lhs, s_mats)]
    a_ab = [jnp.where(strict, m[:c2, c2:], 0.0) for m in amats]
    invs = [eye + n for n in a_ab]
    powers = [_dot(n.astype(BF16), n.astype(BF16)) for n in a_ab]
    for i in range(steps):
        if i < steps - 1:
            xs = [_dot(jnp.concatenate([inv, pw], axis=0).astype(BF16), pw.astype(BF16))
                  for inv, pw in zip(invs, powers)]
            invs = [inv + x[:c2] for inv, x in zip(invs, xs)]
            powers = [x[c2:] for x in xs]
        else:
            invs = [inv + _dot(inv.astype(BF16), pw.astype(BF16)) for inv, pw in zip(invs, powers)]

    akv = [_dot(jnp.where(strict, m[:c2, :c2], 0.0).astype(BF16), v.astype(BF16)) for m, v in zip(amats, v_s)]
    us = [_dot(inv.astype(BF16), (sx[:c2] + x).astype(BF16)) for inv, sx, x in zip(invs, sxs, akv)]
    vus = [jnp.concatenate([v, u], axis=0).astype(BF16) for v, u in zip(v_s, us)]
    for pr in pairs:
        m = amats[pr]
        a_r = jnp.concatenate([jnp.where(lower, m[c2:, :c2], 0.0), jnp.where(lower, m[c2:, c2:], 0.0)], axis=1)
        y_s = sxs[pr][c2:] + _dot(a_r.astype(BF16), vus[pr])
        y_ref[:, lanes[pr]] = y_s[:c] + y_s[c:]
        s_ref[pr] = s_mats[pr] * jnp.exp(totals[pr]) + _dot(vus[pr], bar[pr], _TN)

    @pl.when(ci == pl.num_programs(1) - 1)
    def _():
        sout_ref[0] = s_ref[...]


def _rwkv_chunks(r, lwd, k, v, a, b, s0, nseq, t, c):
    nc = t // c
    row_spec = pl.BlockSpec((c, RWKV_W), lambda s, i: (s * nc + i, 0))
    st_spec = pl.BlockSpec((1, RWKV_PAIRS, LANE, LANE), lambda s, i: (s, 0, 0, 0))
    return pl.pallas_call(
        functools.partial(_rwkv_chunk_kernel, c=c),
        grid=(nseq, nc),
        in_specs=[row_spec] * 6 + [st_spec],
        out_specs=[row_spec, st_spec],
        out_shape=[jax.ShapeDtypeStruct((nseq * t, RWKV_W), F32),
                   jax.ShapeDtypeStruct((nseq, RWKV_PAIRS, LANE, LANE), F32)],
        scratch_shapes=[pltpu.VMEM((RWKV_PAIRS, LANE, LANE), F32)],
        compiler_params=_params("parallel", "arbitrary"),
        name="rwkv_chunks",
    )(r, lwd, k, v, a, b, s0)


def _rwkv_out_kernel(y_ref, r_ref, k_ref, v_ref, g_ref, gng_ref, gnb_ref, rk_ref, e_ref, et_ref, o_ref):
    e = e_ref[...]
    et = et_ref[...]
    y = y_ref[...]
    mu = _head_sum(y, e, et) * (1.0 / RWKV_HD)
    d = y - mu
    var = _head_sum(d * d, e, et) * (1.0 / RWKV_HD)
    yn = d * lax.rsqrt(var + RWKV_GN_EPS) * gng_ref[...] + gnb_ref[...]
    bonus = _head_sum(r_ref[...] * k_ref[...] * rk_ref[...], e, et) * v_ref[...]
    o_ref[...] = ((yn + bonus) * g_ref[...]).astype(o_ref.dtype)


def _rwkv_out(y, r, k, v, g, lw):
    m = y.shape[0]
    tr = _tile(m, 256)
    row_spec = pl.BlockSpec((tr, RWKV_W), lambda i: (i, 0))
    wide = lambda i: (0, 0)
    return pl.pallas_call(
        _rwkv_out_kernel,
        grid=(m // tr,),
        in_specs=[row_spec] * 5 + [pl.BlockSpec((1, RWKV_W), wide)] * 3
                 + [pl.BlockSpec((RWKV_W, LANE), wide), pl.BlockSpec((LANE, RWKV_W), wide)],
        out_specs=row_spec,
        out_shape=jax.ShapeDtypeStruct((m, RWKV_W), BF16),
        compiler_params=_params("parallel"),
        name="rwkv_out",
    )(y, r, k, v, g, lw["gn_g"], lw["gn_b"], lw["r_k"], lw["e"], lw["et"])


def _gmlp_kernel(gugv_ref, lng_ref, lnb_ref, ws_ref, bs_ref, o_ref, *vn_out):
    u = _gelu_tanh(gugv_ref[:, :GMLP_W])
    gv = _gelu_tanh(gugv_ref[:, GMLP_W:])
    mu = jnp.mean(gv, axis=-1, keepdims=True)
    d = gv - mu
    var = jnp.mean(d * d, axis=-1, keepdims=True)
    vn = d * lax.rsqrt(var + LN_EPS) * lng_ref[...] + lnb_ref[...]
    if vn_out:
        vn_out[0][...] = vn
    vb = vn.astype(BF16)
    for g in range(GMLP_GROUPS):
        gs = slice(g * GMLP_GC, (g + 1) * GMLP_GC)
        s = _dot(ws_ref[g], vb[:, gs]) + bs_ref[:, g:g + 1]
        o_ref[:, gs] = (u[:, gs] * s).astype(o_ref.dtype)


def _gmlp(gugv, lw, w_eff, b_eff, emit_vn):
    m = gugv.shape[0]
    tr = GMLP_CHUNK
    row_spec = pl.BlockSpec((tr, GMLP_W), lambda i: (i, 0))
    out_shape = [jax.ShapeDtypeStruct((m, GMLP_W), BF16)]
    out_specs = [row_spec]
    if emit_vn:
        out_shape.append(jax.ShapeDtypeStruct((m, GMLP_W), F32))
        out_specs.append(row_spec)
    res = pl.pallas_call(
        _gmlp_kernel,
        grid=(m // tr,),
        in_specs=[pl.BlockSpec((tr, 2 * GMLP_W), lambda i: (i, 0)),
                  pl.BlockSpec((1, GMLP_W), lambda i: (0, 0)),
                  pl.BlockSpec((1, GMLP_W), lambda i: (0, 0)),
                  pl.BlockSpec((GMLP_GROUPS, tr, tr), lambda i: (0, 0, 0)),
                  pl.BlockSpec((tr, GMLP_GROUPS), lambda i: (0, 0))],
        out_specs=out_specs,
        out_shape=out_shape,
        compiler_params=_params("parallel"),
        name="gmlp_sgu",
    )(gugv, lw["ln_g"], lw["ln_b"], w_eff, b_eff)
    return res if emit_vn else (res[0], None)


def _merge_kernel(oa_ref, ob_ref, oc_ref, ga_ref, gb_ref, gc_ref, wa_ref, wb_ref, wc_ref, o_ref):
    m = _sigmoid(ga_ref[...]) * _dot(oa_ref[...], wa_ref[...])
    m = m + _sigmoid(gb_ref[...]) * _dot(ob_ref[...], wb_ref[...])
    m = m + _sigmoid(gc_ref[...]) * _dot(oc_ref[...], wc_ref[...])
    o_ref[...] = m.astype(o_ref.dtype)


def _merge(o_a, o_b, o_c, gates, w_a, w_b, w_c):
    m, kw = o_a.shape
    d = w_a.shape[1]
    tm = _tile(m, 512)
    tn = _col_tile(d, 512)
    nj = d // tn
    o_spec = pl.BlockSpec((tm, kw), lambda i, j: (i, 0))
    w_spec = pl.BlockSpec((kw, tn), lambda i, j: (0, j))
    return pl.pallas_call(
        _merge_kernel,
        grid=(m // tm, nj),
        in_specs=[o_spec, o_spec, o_spec,
                  pl.BlockSpec((tm, tn), lambda i, j: (i, j)),
                  pl.BlockSpec((tm, tn), lambda i, j: (i, nj + j)),
                  pl.BlockSpec((tm, tn), lambda i, j: (i, 2 * nj + j)),
                  w_spec, w_spec, w_spec],
        out_specs=pl.BlockSpec((tm, tn), lambda i, j: (i, j)),
        out_shape=jax.ShapeDtypeStruct((m, d), BF16),
        compiler_params=_params("parallel", "parallel"),
        name="branch_merge",
    )(o_a, o_b, o_c, gates, gates, gates, w_a, w_b, w_c)


def _out_proj_kernel(m_ref, w_ref, x_ref, g_ref, o_ref):
    o_ref[...] = x_ref[...] + _rms(_dot(m_ref[...], w_ref[...]), g_ref[...])


def _out_proj(mm, w_out, x, g):
    m, d = x.shape
    tm = _tile(m, 512)
    return pl.pallas_call(
        _out_proj_kernel,
        grid=(m // tm,),
        in_specs=[pl.BlockSpec((tm, d), lambda i: (i, 0)),
                  pl.BlockSpec((d, d), lambda i: (0, 0)),
                  pl.BlockSpec((tm, d), lambda i: (i, 0)),
                  pl.BlockSpec((1, d), lambda i: (0, 0))],
        out_specs=pl.BlockSpec((tm, d), lambda i: (i, 0)),
        out_shape=jax.ShapeDtypeStruct((m, d), F32),
        compiler_params=_params("parallel"),
        name="out_proj_residual",
    )(mm, w_out, x, g.reshape(1, d))


def _ffn_kernel(x_ref, gpre_ref, gpost_ref, w1_ref, w3_ref, w2_ref, o_ref, h_ref, acc_ref):
    j = pl.program_id(1)

    @pl.when(j == 0)
    def _():
        h_ref[...] = _rms(x_ref[...], gpre_ref[...]).astype(BF16)
        acc_ref[...] = jnp.zeros_like(acc_ref)

    h = h_ref[...]
    a = _dot(h, w1_ref[...])
    b = _dot(h, w3_ref[...])
    gated = (a * _sigmoid(a)) * b
    acc_ref[...] += _dot(gated.astype(BF16), w2_ref[...])

    @pl.when(j == pl.num_programs(1) - 1)
    def _():
        o_ref[...] = x_ref[...] + _rms(acc_ref[...], gpost_ref[...])


def _ffn(x, g_pre, g_post, w1, w3, w2):
    m, d = x.shape
    dff = w1.shape[1]
    tm = _tile(m, 512)
    tf = _col_tile(dff, 512)
    return pl.pallas_call(
        _ffn_kernel,
        grid=(m // tm, dff // tf),
        in_specs=[pl.BlockSpec((tm, d), lambda i, j: (i, 0)),
                  pl.BlockSpec((1, d), lambda i, j: (0, 0)),
                  pl.BlockSpec((1, d), lambda i, j: (0, 0)),
                  pl.BlockSpec((d, tf), lambda i, j: (0, j)),
                  pl.BlockSpec((d, tf), lambda i, j: (0, j)),
                  pl.BlockSpec((tf, d), lambda i, j: (j, 0))],
        out_specs=pl.BlockSpec((tm, d), lambda i, j: (i, 0)),
        out_shape=jax.ShapeDtypeStruct((m, d), F32),
        scratch_shapes=[pltpu.VMEM((tm, d), BF16), pltpu.VMEM((tm, d), F32)],
        compiler_params=_params("parallel", "arbitrary"),
        name="ffn_swiglu",
    )(x, g_pre.reshape(1, d), g_post.reshape(1, d), w1, w3, w2)


def _pad_rows(w, rows, offset):
    out = jnp.zeros((rows, w.shape[1]), w.dtype)
    return out.at[offset:offset + w.shape[0]].set(w)


def _layer_weights(l, d, w_in, fox_bf, rwkv_mu, rwkv_w0, rwkv_w_up, rwkv_a0, rwkv_a_up, rwkv_g_up, rwkv_k_k,
                   rwkv_k_a, rwkv_r_k, rwkv_gn_g, rwkv_gn_b, gmlp_ln_g, gmlp_ln_b, gmlp_w_s, gmlp_b_s,
                   w_br_fox, w_br_rwkv, w_br_gmlp, w_out, ffn_w1, ffn_w3, ffn_w2, s_new):
    w = w_in[l]
    o_ff = 3 * FOX_W
    o_r = o_ff + FOX_HEADS
    o_g = o_r + RWKV_IN_W
    o_gate = o_g + 2 * GMLP_W
    head_of_lane = jnp.arange(RWKV_W) // RWKV_HD
    e = (head_of_lane[:, None] == jnp.arange(LANE)[None, :]).astype(BF16)
    tril_full = jnp.tril(gmlp_w_s[l])
    blk = jnp.tril(gmlp_w_s[l][:, :s_new, :s_new])
    reps = GMLP_CHUNK // s_new
    eye = jnp.eye(reps, dtype=F32)
    w_sample = jnp.einsum("ab,gij->gaibj", eye, blk).reshape(GMLP_GROUPS, GMLP_CHUNK, GMLP_CHUNK)
    return {
        "wq": w[:, :FOX_W].astype(BF16),
        "wk": w[:, FOX_W:2 * FOX_W].astype(BF16),
        "wv": w[:, 2 * FOX_W:o_ff].astype(BF16),
        "w_ff": jnp.zeros((d, LANE), BF16).at[:, :FOX_HEADS].set(w[:, o_ff:o_r].astype(BF16)),
        "b_ff": jnp.zeros((1, LANE), F32).at[0, :FOX_HEADS].set(fox_bf[l]),
        "w_rwkv": w[:, o_r:o_g].astype(BF16),
        "w_gmlp": w[:, o_g:o_gate].astype(BF16),
        "w_gate": w[:, o_gate:].astype(BF16),
        "rwkv": {
            "mu": rwkv_mu[l].reshape(1, RWKV_IN_W),
            "w0": rwkv_w0[l].reshape(1, RWKV_W),
            "a0": rwkv_a0[l].reshape(1, RWKV_W),
            "k_k": rwkv_k_k[l].reshape(1, RWKV_W),
            "k_a": rwkv_k_a[l].reshape(1, RWKV_W),
            "r_k": rwkv_r_k[l].reshape(1, RWKV_W),
            "gn_g": rwkv_gn_g[l].reshape(1, RWKV_W),
            "gn_b": rwkv_gn_b[l].reshape(1, RWKV_W),
            "w_up": _pad_rows(rwkv_w_up[l], LANE, 0),
            "a_up": _pad_rows(rwkv_a_up[l], LANE, rwkv_w_up.shape[1]),
            "g_up": rwkv_g_up[l],
            "e": e,
            "et": e.T,
        },
        "gmlp": {"ln_g": gmlp_ln_g[l].reshape(1, GMLP_W), "ln_b": gmlp_ln_b[l].reshape(1, GMLP_W)},
        "gmlp_w_prompt": tril_full.astype(BF16),
        "gmlp_b_prompt": gmlp_b_s[l].T,
        "gmlp_w_sample": w_sample.astype(BF16),
        "gmlp_b_sample": jnp.tile(gmlp_b_s[l][:, :s_new].T, (reps, 1)),
        "w_br_fox": w_br_fox[l].astype(BF16),
        "w_br_rwkv": w_br_rwkv[l].astype(BF16),
        "w_br_gmlp": w_br_gmlp[l].astype(BF16),
        "w_out": w_out[l].astype(BF16),
        "ffn_w1": ffn_w1[l].astype(BF16),
        "ffn_w3": ffn_w3[l].astype(BF16),
        "ffn_w2": ffn_w2[l].astype(BF16),
    }


def _states_to_pairs(state):
    n = state.shape[0]
    s = state.reshape(n, RWKV_PAIRS, 2, RWKV_HD, RWKV_HD)
    eye = jnp.eye(2, dtype=state.dtype)
    return jnp.einsum("npavk,ab->npavbk", s, eye).reshape(n, RWKV_PAIRS, LANE, LANE)


def _pairs_to_states(sp):
    n = sp.shape[0]
    s = sp.reshape(n, RWKV_PAIRS, 2, RWKV_HD, 2, RWKV_HD)
    diag = jnp.stack([s[:, :, 0, :, 0, :], s[:, :, 1, :, 1, :]], axis=2)
    return diag.reshape(n, RWKV_HEADS, RWKV_HD, RWKV_HD)


def _trunk_layer(x, lw, pre_mix_g, post_mix_g, pre_ffn_g, post_ffn_g, nseq, t, layer, cache):
    m, d = x.shape
    h = _rmsnorm_cast(x, pre_mix_g)
    q = _matmul(h, lw["wq"], (BF16,), scale=FOX_Q_SCALE)
    k, k_bf = _matmul(h, lw["wk"], (F32, BF16))
    v, v_bf = _matmul(h, lw["wv"], (F32, BF16))
    logf = _forget_logits(h, lw["w_ff"], lw["b_ff"])[:, :FOX_HEADS]
    rw = _matmul(h, lw["w_rwkv"])
    gugv = _matmul(h, lw["w_gmlp"])
    gates = _matmul(h, lw["w_gate"])

    logf_rows = logf.reshape(nseq, t, FOX_HEADS).transpose(0, 2, 1)
    if cache is None:
        c_rows = _cumsum_lanes(logf_rows.reshape(nseq * FOX_HEADS, t), _tile(t, 256), LOG2E)
        c_tm = c_rows.reshape(nseq, FOX_HEADS, t).transpose(0, 2, 1).reshape(m, FOX_HEADS)
        xq, xk = _fox_bias_columns(c_tm)
        o_a = _fox_prompt(q, xq, k_bf, xk, v_bf, nseq, t)
        prev = jnp.zeros((nseq, 1, RWKV_IN_W), F32)
        s0 = jnp.zeros((nseq, RWKV_PAIRS, LANE, LANE), F32)
        chunk = 64
        gm_w, gm_b = lw["gmlp_w_prompt"], lw["gmlp_b_prompt"]
    else:
        depth, _, past = cache["k"].shape[:3]
        total = past + t
        tpad = -(-total // LANE) * LANE
        lf_all = jnp.concatenate([cache["logf"][layer].astype(F32).transpose(0, 2, 1), logf_rows], axis=-1)
        lf_all = jnp.pad(lf_all, ((0, 0), (0, 0), (0, tpad - total)))
        c_rows = _cumsum_lanes(lf_all.reshape(nseq * FOX_HEADS, tpad), LANE, LOG2E).reshape(nseq, FOX_HEADS, tpad)
        c_tm = c_rows[:, :, past:total].transpose(0, 2, 1).reshape(m, FOX_HEADS)
        o_a = _fox_sample(q, k_bf, v_bf, cache["k"].reshape(depth, nseq, past, FOX_W),
                          cache["v"].reshape(depth, nseq, past, FOX_W), layer, c_tm, c_rows, nseq, t, past)
        prev = cache["shift"][layer]
        s0 = _states_to_pairs(cache["state"][layer])
        chunk = t
        gm_w, gm_b = lw["gmlp_w_sample"], lw["gmlp_b_sample"]

    r, lwd, k2, v2, a_s, b_s, g = _rwkv_prep(rw, prev, lw["rwkv"], nseq, t)
    y, s_fin = _rwkv_chunks(r, lwd, k2, v2, a_s, b_s, s0, nseq, t, chunk)
    o_b = _rwkv_out(y, r, k2, v2, g, lw["rwkv"])
    s_new = _pairs_to_states(s_fin)
    shift_new = rw.reshape(nseq, t, RWKV_IN_W)[:, t - 1:t, :]

    o_c, vn = _gmlp(gugv, lw["gmlp"], gm_w, gm_b, cache is not None)

    mm = _merge(o_a, o_b, o_c, gates, lw["w_br_fox"], lw["w_br_rwkv"], lw["w_br_gmlp"])
    x = _out_proj(mm, lw["w_out"], x, post_mix_g)
    x = _ffn(x, pre_ffn_g, post_ffn_g, lw["ffn_w1"], lw["ffn_w3"], lw["ffn_w2"])
    return x, (k, v, logf, s_new, shift_new, vn)


def kernel(x_prompt, x_sample, cache_fox_k, cache_fox_v, cache_fox_logf, state_rwkv, state_rwkv_shift, pre_mix_g, w_in, fox_bf, rwkv_mu, rwkv_w0, rwkv_w_up, rwkv_a0, rwkv_a_up, rwkv_g_up, rwkv_k_k, rwkv_k_a, rwkv_r_k, rwkv_gn_g, rwkv_gn_b, gmlp_ln_g, gmlp_ln_b, gmlp_w_s, gmlp_b_s, w_br_fox, w_br_rwkv, w_br_gmlp, w_out, post_mix_g, pre_ffn_g, ffn_w1, ffn_w3, ffn_w2, post_ffn_g):
    batch, seq, d = x_prompt.shape
    dec_batch, dec_seq, _ = x_sample.shape
    depth = w_in.shape[0]
    assert rwkv_w_up.shape[1] + rwkv_a_up.shape[1] == LANE and rwkv_g_up.shape[1] == LANE
    assert seq % GMLP_CHUNK == 0 and GMLP_CHUNK % dec_seq == 0 and (dec_batch * dec_seq) % GMLP_CHUNK == 0

    xp = x_prompt.reshape(batch * seq, d)
    xs = x_sample.reshape(dec_batch * dec_seq, d)
    cache = {"k": cache_fox_k, "v": cache_fox_v, "logf": cache_fox_logf,
             "state": state_rwkv, "shift": state_rwkv_shift}
    outs_p = [[] for _ in range(5)]
    outs_s = [[] for _ in range(6)]
    for l in range(depth):
        lw = _layer_weights(l, d, w_in, fox_bf, rwkv_mu, rwkv_w0, rwkv_w_up, rwkv_a0, rwkv_a_up, rwkv_g_up,
                            rwkv_k_k, rwkv_k_a, rwkv_r_k, rwkv_gn_g, rwkv_gn_b, gmlp_ln_g, gmlp_ln_b, gmlp_w_s,
                            gmlp_b_s, w_br_fox, w_br_rwkv, w_br_gmlp, w_out, ffn_w1, ffn_w3, ffn_w2, dec_seq)
        norms = (pre_mix_g[l], post_mix_g[l], pre_ffn_g[l], post_ffn_g[l])
        xp, (k_p, v_p, lf_p, s_p, sh_p, _) = _trunk_layer(xp, lw, *norms, batch, seq, l, None)
        xs, (k_s, v_s, lf_s, s_s, sh_s, vn_s) = _trunk_layer(xs, lw, *norms, dec_batch, dec_seq, l, cache)
        for acc, val in zip(outs_p, (k_p.reshape(batch, seq, FOX_HEADS, FOX_HD), v_p.reshape(batch, seq, FOX_HEADS, FOX_HD),
                                     lf_p.reshape(batch, seq, FOX_HEADS), s_p, sh_p)):
            acc.append(val)
        for acc, val in zip(outs_s, (k_s.reshape(dec_batch, dec_seq, FOX_HEADS, FOX_HD),
                                     v_s.reshape(dec_batch, dec_seq, FOX_HEADS, FOX_HD),
                                     lf_s.reshape(dec_batch, dec_seq, FOX_HEADS), s_s, sh_s,
                                     vn_s.reshape(dec_batch, dec_seq, GMLP_W))):
            acc.append(val)
    return (xp.reshape(batch, seq, d), xs.reshape(dec_batch, dec_seq, d),
            *[jnp.stack(o) for o in outs_p], *[jnp.stack(o) for o in outs_s])
```

```python
import functools
import math

import jax
import jax.numpy as jnp
from jax import lax
from jax.experimental import pallas as pl
from jax.experimental.pallas import tpu as pltpu

F32 = jnp.float32
BF16 = jnp.bfloat16

RMS_EPS = 1e-6
LN_EPS = 1e-5
RWKV_GN_EPS = 64e-5
NEG_INF = -1e30
LOG2E = math.log2(math.e)

LANE = 128
FOX_HEADS = 8
FOX_HD = 128
FOX_W = FOX_HEADS * FOX_HD
FOX_Q_SCALE = FOX_HD ** -0.5 * LOG2E
RWKV_HEADS = 16
RWKV_HD = 64
RWKV_W = RWKV_HEADS * RWKV_HD
RWKV_PAIRS = RWKV_W // LANE
RWKV_LORA_OFF = 3 * RWKV_W
RWKV_IN_W = 3 * RWKV_W + 2 * LANE
GMLP_W = 1024
GMLP_GROUPS = 8
GMLP_GC = GMLP_W // GMLP_GROUPS
GMLP_CHUNK = 128
N_BRANCH = 3

VMEM_LIMIT = 48 * 1024 * 1024


def _params(*sem):
    return pltpu.CompilerParams(dimension_semantics=sem, vmem_limit_bytes=VMEM_LIMIT)


def _tile(n, pref):
    if n <= pref:
        return n
    t = pref
    while n % t:
        t -= 8
    return t


def _col_tile(n, pref):
    assert n % LANE == 0
    k = n // LANE
    best = 1
    for d in range(1, k + 1):
        if k % d == 0 and d * LANE <= pref:
            best = d
    return best * LANE


def _split2(x):
    hi = x.astype(BF16)
    lo = (x - hi.astype(F32)).astype(BF16)
    return hi, lo


def _split3(x):
    hi = x.astype(BF16)
    r = x - hi.astype(F32)
    mid = r.astype(BF16)
    lo = (r - mid.astype(F32)).astype(BF16)
    return hi, mid, lo


_NN = (((1,), (0,)), ((), ()))
_NT = (((1,), (1,)), ((), ()))
_TN = (((0,), (0,)), ((), ()))


def _dot(a, b, dims=_NN):
    return lax.dot_general(a, b, dims, preferred_element_type=F32)


def _dot3(x, y, dims=_NN):
    xh, xl = _split2(x)
    yh, yl = _split2(y)
    return _dot(xh, yh, dims) + _dot(xh, yl, dims) + _dot(xl, yh, dims)


def _dot_exact_rhs(x, e, parts):
    ps = _split3(x) if parts == 3 else _split2(x)
    out = _dot(ps[0], e)
    for p in ps[1:]:
        out = out + _dot(p, e)
    return out


def _sigmoid(x):
    return 1.0 / (1.0 + jnp.exp(-x))


def _log_sigmoid(x):
    return jnp.minimum(x, 0.0) - jnp.log(1.0 + jnp.exp(-jnp.abs(x)))


def _gelu_tanh(x):
    return 0.5 * x * (1.0 + jnp.tanh(0.7978845608028654 * (x + 0.044715 * (x * x * x))))


def _rms(x, g):
    ms = jnp.mean(x * x, axis=-1, keepdims=True)
    return x * lax.rsqrt(ms + RMS_EPS) * g


def _rmsnorm_cast_kernel(x_ref, g_ref, o_ref):
    o_ref[...] = _rms(x_ref[...], g_ref[...]).astype(o_ref.dtype)


def _rmsnorm_cast(x, g):
    m, d = x.shape
    tm = _tile(m, 512)
    return pl.pallas_call(
        _rmsnorm_cast_kernel,
        grid=(m // tm,),
        in_specs=[pl.BlockSpec((tm, d), lambda i: (i, 0)),
                  pl.BlockSpec((1, d), lambda i: (0, 0))],
        out_specs=pl.BlockSpec((tm, d), lambda i: (i, 0)),
        out_shape=jax.ShapeDtypeStruct((m, d), BF16),
        compiler_params=_params("parallel"),
        name="rmsnorm_cast",
    )(x, g.reshape(1, d))


def _matmul_kernel(a_ref, w_ref, *o_refs, scale):
    acc = _dot(a_ref[...], w_ref[...])
    if scale is not None:
        acc = acc * scale
    for o_ref in o_refs:
        o_ref[...] = acc.astype(o_ref.dtype)


def _matmul(a, w, out_dtypes=(F32,), scale=None, tm_pref=1024, tn_pref=512):
    m, k = a.shape
    n = w.shape[1]
    tm = _tile(m, tm_pref)
    tn = _col_tile(n, tn_pref)
    o_spec = pl.BlockSpec((tm, tn), lambda i, j: (i, j))
    res = pl.pallas_call(
        functools.partial(_matmul_kernel, scale=scale),
        grid=(m // tm, n // tn),
        in_specs=[pl.BlockSpec((tm, k), lambda i, j: (i, 0)),
                  pl.BlockSpec((k, tn), lambda i, j: (0, j))],
        out_specs=[o_spec] * len(out_dtypes),
        out_shape=[jax.ShapeDtypeStruct((m, n), dt) for dt in out_dtypes],
        compiler_params=_params("parallel", "parallel"),
        name="proj_matmul",
    )(a, w)
    return res[0] if len(out_dtypes) == 1 else res


def _kv_proj_kernel(a_ref, w_ref, o_ref, obf_ref):
    acc = _dot(a_ref[...], w_ref[...])
    obf_ref[...] = acc.astype(obf_ref.dtype)
    for h in range(FOX_HEADS):
        o_ref[:, h, :] = acc[:, h * FOX_HD:(h + 1) * FOX_HD]


def _kv_proj(a, w):
    m, k = a.shape
    tm = _tile(m, 512)
    return pl.pallas_call(
        _kv_proj_kernel,
        grid=(m // tm,),
        in_specs=[pl.BlockSpec((tm, k), lambda i: (i, 0)),
                  pl.BlockSpec((k, FOX_W), lambda i: (0, 0))],
        out_specs=[pl.BlockSpec((tm, FOX_HEADS, FOX_HD), lambda i: (i, 0, 0)),
                   pl.BlockSpec((tm, FOX_W), lambda i: (i, 0))],
        out_shape=[jax.ShapeDtypeStruct((m, FOX_HEADS, FOX_HD), F32),
                   jax.ShapeDtypeStruct((m, FOX_W), BF16)],
        compiler_params=_params("parallel"),
        name="kv_proj",
    )(a, w)


def _logf_kernel(a_ref, w_ref, b_ref, o_ref):
    o_ref[...] = _log_sigmoid(_dot(a_ref[...], w_ref[...]) + b_ref[...])


def _forget_logits(h, w_ff, b_ff):
    m, k = h.shape
    tm = _tile(m, 1024)
    return pl.pallas_call(
        _logf_kernel,
        grid=(m // tm,),
        in_specs=[pl.BlockSpec((tm, k), lambda i: (i, 0)),
                  pl.BlockSpec((k, LANE), lambda i: (0, 0)),
                  pl.BlockSpec((1, LANE), lambda i: (0, 0))],
        out_specs=pl.BlockSpec((tm, LANE), lambda i: (i, 0)),
        out_shape=jax.ShapeDtypeStruct((m, LANE), F32),
        compiler_params=_params("parallel"),
        name="forget_logits",
    )(h, w_ff, b_ff)


def _cumsum_kernel(x_ref, o_ref, carry_ref, *, tb, scale):
    @pl.when(pl.program_id(0) == 0)
    def _():
        carry_ref[...] = jnp.zeros_like(carry_ref)

    row = lax.broadcasted_iota(jnp.int32, (tb, tb), 0)
    col = lax.broadcasted_iota(jnp.int32, (tb, tb), 1)
    upper = jnp.where(row <= col, 1.0, 0.0).astype(BF16)
    c = _dot_exact_rhs(x_ref[...], upper, 3) + carry_ref[:, :1]
    o_ref[...] = c * scale
    carry_ref[...] = jnp.broadcast_to(c[:, tb - 1:tb], carry_ref.shape)


def _cumsum_lanes(x, tb, scale):
    r, t = x.shape
    return pl.pallas_call(
        functools.partial(_cumsum_kernel, tb=tb, scale=scale),
        grid=(t // tb,),
        in_specs=[pl.BlockSpec((r, tb), lambda i: (0, i))],
        out_specs=pl.BlockSpec((r, tb), lambda i: (0, i)),
        out_shape=jax.ShapeDtypeStruct((r, t), F32),
        scratch_shapes=[pltpu.VMEM((r, LANE), F32)],
        compiler_params=_params("arbitrary"),
        name="logf_cumsum",
    )(x)


def _fox_bias_kernel(c_ref, xq_ref, xk_ref):
    tr = c_ref.shape[0]
    lane = lax.broadcasted_iota(jnp.int32, (tr, LANE), 1)
    for h in range(FOX_HEADS):
        hi, mid, lo = [p.astype(F32) for p in _split3(c_ref[:, h:h + 1])]
        ones = jnp.where(lane < 6, 1.0, 0.0)
        xq = jnp.where(lane == 0, hi, jnp.where(lane == 1, mid, jnp.where(lane == 2, lo, ones)))
        xk = jnp.where(lane == 3, -hi, jnp.where(lane == 4, -mid, jnp.where(lane == 5, -lo, ones)))
        xq_ref[:, h * LANE:(h + 1) * LANE] = xq.astype(BF16)
        xk_ref[:, h * LANE:(h + 1) * LANE] = xk.astype(BF16)


def _fox_bias_columns(c_tm):
    m = c_tm.shape[0]
    tr = _tile(m, 512)
    spec = pl.BlockSpec((tr, FOX_W), lambda i: (i, 0))
    return pl.pallas_call(
        _fox_bias_kernel,
        grid=(m // tr,),
        in_specs=[pl.BlockSpec((tr, FOX_HEADS), lambda i: (i, 0))],
        out_specs=[spec, spec],
        out_shape=[jax.ShapeDtypeStruct((m, FOX_W), BF16)] * 2,
        compiler_params=_params("parallel"),
        name="fox_bias_columns",
    )(c_tm)


def _fox_prompt_kernel(qi_ref, ki_ref, q_ref, xq_ref, k_ref, xk_ref, v_ref, o_ref, m_ref, l_ref, acc_ref, *, tq):
    step = pl.program_id(1)
    qi = qi_ref[step]
    ki = ki_ref[step]

    @pl.when(ki == 0)
    def _():
        m_ref[...] = jnp.full_like(m_ref, NEG_INF)
        l_ref[...] = jnp.zeros_like(l_ref)
        acc_ref[...] = jnp.zeros_like(acc_ref)

    def update(diagonal):
        if diagonal:
            row = lax.broadcasted_iota(jnp.int32, (tq, tq), 0)
            col = lax.broadcasted_iota(jnp.int32, (tq, tq), 1)
            causal = col <= row
        heads = range(FOX_HEADS)
        cols = [slice(h * FOX_HD, (h + 1) * FOX_HD) for h in heads]
        s = [_dot(jnp.concatenate([q_ref[:, hs], xq_ref[:, hs]], axis=1),
                  jnp.concatenate([k_ref[:, hs], xk_ref[:, hs]], axis=1), _NT) for hs in cols]
        if diagonal:
            s = [jnp.where(causal, x, NEG_INF) for x in s]
        m_prev = [m_ref[h] for h in heads]
        m_new = [jnp.maximum(mp, jnp.max(x, axis=-1, keepdims=True)) for mp, x in zip(m_prev, s)]
        p = [jnp.exp2(x - jnp.concatenate([mn] * (tq // LANE), axis=1)) for x, mn in zip(s, m_new)]
        p_sum = [jnp.sum(x, axis=-1, keepdims=True) for x in p]
        pv = [_dot(x.astype(BF16), v_ref[:, hs]) for x, hs in zip(p, cols)]
        for h in heads:
            alpha = jnp.exp2(m_prev[h] - m_new[h])
            l_ref[h] = alpha * l_ref[h] + p_sum[h]
            acc_ref[:, cols[h]] = alpha * acc_ref[:, cols[h]] + pv[h]
            m_ref[h] = m_new[h]

    @pl.when(ki < qi)
    def _():
        update(False)

    @pl.when(ki == qi)
    def _():
        update(True)
        for h in range(FOX_HEADS):
            hs = slice(h * FOX_HD, (h + 1) * FOX_HD)
            o_ref[:, hs] = (acc_ref[:, hs] / l_ref[h]).astype(o_ref.dtype)


def _fox_prompt(q, xq, k, xk, v, batch, seq):
    tq = _tile(seq, 512)
    assert tq % LANE == 0
    nq = seq // tq
    blocks = [(qi, ki) for qi in range(nq) for ki in range(qi + 1)]
    qi_tab = jnp.asarray([b[0] for b in blocks], jnp.int32)
    ki_tab = jnp.asarray([b[1] for b in blocks], jnp.int32)
    q_map = lambda b, s, qt, kt: (b * nq + qt[s], 0)
    kv_map = lambda b, s, qt, kt: (b * nq + kt[s], 0)
    return pl.pallas_call(
        functools.partial(_fox_prompt_kernel, tq=tq),
        grid_spec=pltpu.PrefetchScalarGridSpec(
            num_scalar_prefetch=2,
            grid=(batch, len(blocks)),
            in_specs=[pl.BlockSpec((tq, FOX_W), q_map),
                      pl.BlockSpec((tq, FOX_W), q_map),
                      pl.BlockSpec((tq, FOX_W), kv_map),
                      pl.BlockSpec((tq, FOX_W), kv_map),
                      pl.BlockSpec((tq, FOX_W), kv_map)],
            out_specs=pl.BlockSpec((tq, FOX_W), q_map),
            scratch_shapes=[pltpu.VMEM((FOX_HEADS, tq, LANE), F32),
                            pltpu.VMEM((FOX_HEADS, tq, LANE), F32),
                            pltpu.VMEM((tq, FOX_W), F32)]),
        out_shape=jax.ShapeDtypeStruct((batch * seq, FOX_W), BF16),
        compiler_params=_params("parallel", "arbitrary"),
        name="fox_prompt",
    )(qi_tab, ki_tab, q, xq, k, xk, v)


def _fox_sample_kernel(q_ref, kn_ref, vn_ref, cq_ref, ck_ref, kc_ref, vc_ref, o_ref, *, past, s_new):
    kc = pltpu.einshape("thd->htd", kc_ref[0, 0])
    vc = pltpu.einshape("thd->htd", vc_ref[0, 0])
    row = lax.broadcasted_iota(jnp.int32, (s_new, s_new), 0)
    col = lax.broadcasted_iota(jnp.int32, (s_new, s_new), 1)
    causal = col <= row
    for h in range(FOX_HEADS):
        hs = slice(h * FOX_HD, (h + 1) * FOX_HD)
        q = q_ref[:, hs]
        cq = cq_ref[:, h:h + 1]
        s_c = _dot(q, kc[h].astype(BF16), _NT) + (cq - ck_ref[0, h:h + 1, :past])
        s_n = _dot(q, kn_ref[:, hs], _NT) + (cq - ck_ref[0, h:h + 1, past:past + s_new])
        s_n = jnp.where(causal, s_n, NEG_INF)
        m = jnp.maximum(jnp.max(s_c, axis=-1, keepdims=True), jnp.max(s_n, axis=-1, keepdims=True))
        p_c = jnp.exp2(s_c - m)
        p_n = jnp.exp2(s_n - m)
        l = jnp.sum(p_c, axis=-1, keepdims=True) + jnp.sum(p_n, axis=-1, keepdims=True)
        o = _dot(p_c.astype(BF16), vc[h].astype(BF16)) + _dot(p_n.astype(BF16), vn_ref[:, hs])
        o_ref[:, hs] = (o / l).astype(o_ref.dtype)


def _fox_sample(q, k, v, k_cache, v_cache, layer, c_tm, c_rows, batch, s_new, past):
    tpad = c_rows.shape[-1]
    new_spec = pl.BlockSpec((s_new, FOX_W), lambda b: (b, 0))
    cache_spec = pl.BlockSpec((1, 1, past, FOX_HEADS, FOX_HD), lambda b: (layer, b, 0, 0, 0))
    return pl.pallas_call(
        functools.partial(_fox_sample_kernel, past=past, s_new=s_new),
        grid=(batch,),
        in_specs=[new_spec, new_spec, new_spec,
                  pl.BlockSpec((s_new, FOX_HEADS), lambda b: (b, 0)),
                  pl.BlockSpec((1, FOX_HEADS, tpad), lambda b: (b, 0, 0)), cache_spec, cache_spec],
        out_specs=new_spec,
        out_shape=jax.ShapeDtypeStruct((batch * s_new, FOX_W), BF16),
        compiler_params=_params("parallel"),
        name="fox_sample",
    )(q, k, v, c_tm, c_rows, k_cache, v_cache)


def _head_sum(x, e, et):
    return _dot_exact_rhs(_dot_exact_rhs(x, e, 2), et, 2)


def _rwkv_prep_kernel(p_ref, prev_ref, mu_ref, w0_ref, a0_ref, kk_ref, ka_ref, wup_ref, aup_ref, gup_ref,
                      e_ref, et_ref, r_ref, lw_ref, k_ref, v_ref, a_ref, b_ref, g_ref, carry_ref, *, tr):
    @pl.when(pl.program_id(1) == 0)
    def _():
        carry_ref[...] = prev_ref[0]

    p = p_ref[0]
    rows = lax.broadcasted_iota(jnp.int32, (tr, 1), 0)
    p_prev = jnp.where(rows == 0, carry_ref[...], pltpu.roll(p, shift=1, axis=0))
    carry_ref[...] = p[tr - 1:tr, :]
    xs = p + (p_prev - p) * mu_ref[...]

    r = xs[:, :RWKV_W]
    k = xs[:, RWKV_W:2 * RWKV_W]
    v = xs[:, 2 * RWKV_W:3 * RWKV_W]
    d_wa = xs[:, RWKV_LORA_OFF:RWKV_LORA_OFF + LANE]
    d_g = xs[:, RWKV_LORA_OFF + LANE:RWKV_LORA_OFF + 2 * LANE]

    z_w = w0_ref[...] + _dot3(jnp.tanh(d_wa), wup_ref[...])
    log_decay = -jnp.exp(_log_sigmoid(z_w) - 0.5)
    a = _sigmoid(a0_ref[...] + _dot3(d_wa, aup_ref[...]))
    g = _dot3(_sigmoid(d_g), gup_ref[...])

    kk = k * kk_ref[...]
    kk = kk * lax.rsqrt(jnp.maximum(_head_sum(kk * kk, e_ref[...], et_ref[...]), 1e-24))
    k2 = k * (1.0 + (a - 1.0) * ka_ref[...])

    r_ref[...] = r
    lw_ref[...] = log_decay
    k_ref[...] = k2
    v_ref[...] = v
    a_ref[...] = -kk
    b_ref[...] = kk * a
    g_ref[...] = g


def _rwkv_prep(rw, prev, lw, nseq, t):
    tr = _tile(t, 256)
    nb = t // tr
    wide = lambda s, i: (0, 0)
    row_spec = pl.BlockSpec((tr, RWKV_W), lambda s, i: (s * nb + i, 0))
    out = jax.ShapeDtypeStruct((nseq * t, RWKV_W), F32)
    return pl.pallas_call(
        functools.partial(_rwkv_prep_kernel, tr=tr),
        grid=(nseq, nb),
        in_specs=[pl.BlockSpec((1, tr, RWKV_IN_W), lambda s, i: (s, i, 0)),
                  pl.BlockSpec((1, 1, RWKV_IN_W), lambda s, i: (s, 0, 0)),
                  pl.BlockSpec((1, RWKV_IN_W), wide),
                  pl.BlockSpec((1, RWKV_W), wide),
                  pl.BlockSpec((1, RWKV_W), wide),
                  pl.BlockSpec((1, RWKV_W), wide),
                  pl.BlockSpec((1, RWKV_W), wide),
                  pl.BlockSpec((LANE, RWKV_W), wide),
                  pl.BlockSpec((LANE, RWKV_W), wide),
                  pl.BlockSpec((LANE, RWKV_W), wide),
                  pl.BlockSpec((RWKV_W, LANE), wide),
                  pl.BlockSpec((LANE, RWKV_W), wide)],
        out_specs=[row_spec] * 7,
        out_shape=[out] * 7,
        scratch_shapes=[pltpu.VMEM((1, RWKV_IN_W), F32)],
        compiler_params=_params("parallel", "arbitrary"),
        name="rwkv_prep",
    )(rw.reshape(nseq, t, RWKV_IN_W), prev, lw["mu"], lw["w0"], lw["a0"], lw["k_k"], lw["k_a"],
      lw["w_up"], lw["a_up"], lw["g_up"], lw["e"], lw["et"])


def _rwkv_chunk_kernel(r_ref, lw_ref, k_ref, v_ref, a_ref, b_ref, g_ref, gng_ref, gnb_ref, rk_ref, s0_ref,
                       o_ref, sout_ref, s_ref, *, c):
    ci = pl.program_id(1)

    @pl.when(ci == 0)
    def _():
        s_ref[...] = s0_ref[0]

    c2 = 2 * c
    row = lax.broadcasted_iota(jnp.int32, (c2, c2), 0)
    col = lax.broadcasted_iota(jnp.int32, (c2, c2), 1)
    lower = col <= row
    strict = col < row
    eye = jnp.where(row == col, 1.0, 0.0)
    rc = lax.broadcasted_iota(jnp.int32, (c, c), 0)
    cc = lax.broadcasted_iota(jnp.int32, (c, c), 1)
    tri = jnp.where(cc <= rc, 1.0, 0.0).astype(BF16)
    lane = lax.broadcasted_iota(jnp.int32, (c, LANE), 1)
    first = lane < RWKV_HD
    steps = max(c.bit_length() - 2, 0)

    def stack(x):
        return jnp.concatenate([jnp.where(first, x, 0.0), jnp.where(first, 0.0, x)], axis=0)

    own = (lax.broadcasted_iota(jnp.int32, (c2, LANE), 1) // RWKV_HD
           == lax.broadcasted_iota(jnp.int32, (c2, LANE), 0) // c)

    pairs = range(RWKV_PAIRS)
    lanes = [slice(pr * LANE, (pr + 1) * LANE) for pr in pairs]
    s_mats = [s_ref[pr] for pr in pairs]
    lws = [lw_ref[:, ls] for ls in lanes]
    cum3 = [_dot(tri, jnp.concatenate(_split3(lw), axis=1)) for lw in lws]
    cums = [x[:, :LANE] + x[:, LANE:2 * LANE] + x[:, 2 * LANE:] for x in cum3]
    totals = [cum[c - 1:c, :] for cum in cums]

    lhs, rhs, bar, v_s = [], [], [], []
    for ls, lw, cum, total in zip(lanes, lws, cums, totals):
        grow = jnp.exp(-cum)
        rest = jnp.exp(total - cum)
        k = k_ref[:, ls]
        b = b_ref[:, ls]
        lhs.append(jnp.concatenate([stack(a_ref[:, ls] * jnp.exp(cum - lw)),
                                    stack(r_ref[:, ls] * jnp.exp(cum))], axis=0).astype(BF16))
        rhs.append(jnp.concatenate([stack(k * grow), stack(b * grow)], axis=0).astype(BF16))
        bar.append(jnp.concatenate([stack(k * rest), stack(b * rest)], axis=0).astype(BF16))
        v_s.append(stack(v_ref[:, ls]))

    amats = [_dot(x, y, _NT) for x, y in zip(lhs, rhs)]
    sxs = [_dot(x, s.astype(BF16), _NT) for x, s in zip(lhs, s_mats)]
    a_ab = [jnp.where(strict, m[:c2, c2:], 0.0) for m in amats]
    invs = [eye + n for n in a_ab]
    powers = [_dot(n.astype(BF16), n.astype(BF16)) for n in a_ab]
    for i in range(steps):
        if i < steps - 1:
            xs = [_dot(jnp.concatenate([inv, pw], axis=0).astype(BF16), pw.astype(BF16))
                  for inv, pw in zip(invs, powers)]
            invs = [inv + x[:c2] for inv, x in zip(invs, xs)]
            powers = [x[c2:] for x in xs]
        else:
            invs = [inv + _dot(inv.astype(BF16), pw.astype(BF16)) for inv, pw in zip(invs, powers)]

    akv = [_dot(jnp.where(strict, m[:c2, :c2], 0.0).astype(BF16), v.astype(BF16)) for m, v in zip(amats, v_s)]
    us = [_dot(inv.astype(BF16), (sx[:c2] + x).astype(BF16)) for inv, sx, x in zip(invs, sxs, akv)]
    vus = [jnp.concatenate([v, u], axis=0).astype(BF16) for v, u in zip(v_s, us)]
    for pr in pairs:
        m = amats[pr]
        a_r = jnp.concatenate([jnp.where(lower, m[c2:, :c2], 0.0), jnp.where(lower, m[c2:, c2:], 0.0)], axis=1)
        y_s = sxs[pr][c2:] + _dot(a_r.astype(BF16), vus[pr])
        s_ref[pr] = s_mats[pr] * jnp.exp(totals[pr]) + _dot(vus[pr], bar[pr], _TN)
        ls = lanes[pr]
        mu = jnp.sum(y_s, axis=-1, keepdims=True) * (1.0 / RWKV_HD)
        d = jnp.where(own, y_s - mu, 0.0)
        var = jnp.sum(d * d, axis=-1, keepdims=True) * (1.0 / RWKV_HD)
        yn_s = d * lax.rsqrt(var + RWKV_GN_EPS)
        bonus_s = jnp.sum(stack(r_ref[:, ls] * k_ref[:, ls] * rk_ref[:, ls]), axis=-1, keepdims=True) * v_s[pr]
        yn = (yn_s[:c] + yn_s[c:]) * gng_ref[:, ls] + gnb_ref[:, ls]
        o_ref[:, ls] = ((yn + (bonus_s[:c] + bonus_s[c:])) * g_ref[:, ls]).astype(o_ref.dtype)

    @pl.when(ci == pl.num_programs(1) - 1)
    def _():
        sout_ref[0] = s_ref[...]


def _rwkv_chunks(r, lwd, k, v, a, b, g, lw, s0, nseq, t, c):
    nc = t // c
    row_spec = pl.BlockSpec((c, RWKV_W), lambda s, i: (s * nc + i, 0))
    par_spec = pl.BlockSpec((1, RWKV_W), lambda s, i: (0, 0))
    st_spec = pl.BlockSpec((1, RWKV_PAIRS, LANE, LANE), lambda s, i: (s, 0, 0, 0))
    return pl.pallas_call(
        functools.partial(_rwkv_chunk_kernel, c=c),
        grid=(nseq, nc),
        in_specs=[row_spec] * 7 + [par_spec] * 3 + [st_spec],
        out_specs=[row_spec, st_spec],
        out_shape=[jax.ShapeDtypeStruct((nseq * t, RWKV_W), BF16),
                   jax.ShapeDtypeStruct((nseq, RWKV_PAIRS, LANE, LANE), F32)],
        scratch_shapes=[pltpu.VMEM((RWKV_PAIRS, LANE, LANE), F32)],
        compiler_params=_params("parallel", "arbitrary"),
        name="rwkv_chunks",
    )(r, lwd, k, v, a, b, g, lw["gn_g"], lw["gn_b"], lw["r_k"], s0)


def _gmlp_kernel(gugv_ref, lng_ref, lnb_ref, ws_ref, bs_ref, o_ref, *vn_out):
    u = _gelu_tanh(gugv_ref[:, :GMLP_W])
    gv = _gelu_tanh(gugv_ref[:, GMLP_W:])
    mu = jnp.mean(gv, axis=-1, keepdims=True)
    d = gv - mu
    var = jnp.mean(d * d, axis=-1, keepdims=True)
    vn = d * lax.rsqrt(var + LN_EPS) * lng_ref[...] + lnb_ref[...]
    if vn_out:
        vn_out[0][...] = vn
    vb = vn.astype(BF16)
    for g in range(GMLP_GROUPS):
        gs = slice(g * GMLP_GC, (g + 1) * GMLP_GC)
        s = _dot(ws_ref[g], vb[:, gs]) + bs_ref[:, g:g + 1]
        o_ref[:, gs] = (u[:, gs] * s).astype(o_ref.dtype)


def _gmlp(gugv, lw, w_eff, b_eff, emit_vn):
    m = gugv.shape[0]
    tr = GMLP_CHUNK
    row_spec = pl.BlockSpec((tr, GMLP_W), lambda i: (i, 0))
    out_shape = [jax.ShapeDtypeStruct((m, GMLP_W), BF16)]
    out_specs = [row_spec]
    if emit_vn:
        out_shape.append(jax.ShapeDtypeStruct((m, GMLP_W), F32))
        out_specs.append(row_spec)
    res = pl.pallas_call(
        _gmlp_kernel,
        grid=(m // tr,),
        in_specs=[pl.BlockSpec((tr, 2 * GMLP_W), lambda i: (i, 0)),
                  pl.BlockSpec((1, GMLP_W), lambda i: (0, 0)),
                  pl.BlockSpec((1, GMLP_W), lambda i: (0, 0)),
                  pl.BlockSpec((GMLP_GROUPS, tr, tr), lambda i: (0, 0, 0)),
                  pl.BlockSpec((tr, GMLP_GROUPS), lambda i: (0, 0))],
        out_specs=out_specs,
        out_shape=out_shape,
        compiler_params=_params("parallel"),
        name="gmlp_sgu",
    )(gugv, lw["ln_g"], lw["ln_b"], w_eff, b_eff)
    return res if emit_vn else (res[0], None)


def _merge_kernel(oa_ref, ob_ref, oc_ref, ga_ref, gb_ref, gc_ref, wa_ref, wb_ref, wc_ref, o_ref):
    m = _sigmoid(ga_ref[...]) * _dot(oa_ref[...], wa_ref[...])
    m = m + _sigmoid(gb_ref[...]) * _dot(ob_ref[...], wb_ref[...])
    m = m + _sigmoid(gc_ref[...]) * _dot(oc_ref[...], wc_ref[...])
    o_ref[...] = m.astype(o_ref.dtype)


def _merge(o_a, o_b, o_c, gates, w_a, w_b, w_c):
    m, kw = o_a.shape
    d = w_a.shape[1]
    tm = _tile(m, 512)
    tn = _col_tile(d, 1024)
    nj = d // tn
    o_spec = pl.BlockSpec((tm, kw), lambda i, j: (i, 0))
    w_spec = pl.BlockSpec((kw, tn), lambda i, j: (0, j))
    return pl.pallas_call(
        _merge_kernel,
        grid=(m // tm, nj),
        in_specs=[o_spec, o_spec, o_spec,
                  pl.BlockSpec((tm, tn), lambda i, j: (i, j)),
                  pl.BlockSpec((tm, tn), lambda i, j: (i, nj + j)),
                  pl.BlockSpec((tm, tn), lambda i, j: (i, 2 * nj + j)),
                  w_spec, w_spec, w_spec],
        out_specs=pl.BlockSpec((tm, tn), lambda i, j: (i, j)),
        out_shape=jax.ShapeDtypeStruct((m, d), BF16),
        compiler_params=_params("parallel", "parallel"),
        name="branch_merge",
    )(o_a, o_b, o_c, gates, gates, gates, w_a, w_b, w_c)


def _out_proj_kernel(m_ref, w_ref, x_ref, g_ref, o_ref):
    o_ref[...] = x_ref[...] + _rms(_dot(m_ref[...], w_ref[...]), g_ref[...])


def _out_proj(mm, w_out, x, g):
    m, d = x.shape
    tm = _tile(m, 512)
    return pl.pallas_call(
        _out_proj_kernel,
        grid=(m // tm,),
        in_specs=[pl.BlockSpec((tm, d), lambda i: (i, 0)),
                  pl.BlockSpec((d, d), lambda i: (0, 0)),
                  pl.BlockSpec((tm, d), lambda i: (i, 0)),
                  pl.BlockSpec((1, d), lambda i: (0, 0))],
        out_specs=pl.BlockSpec((tm, d), lambda i: (i, 0)),
        out_shape=jax.ShapeDtypeStruct((m, d), F32),
        compiler_params=_params("parallel"),
        name="out_proj_residual",
    )(mm, w_out, x, g.reshape(1, d))


def _ffn_kernel(x_ref, gpre_ref, gpost_ref, w1_ref, w3_ref, w2_ref, o_ref, h_ref, acc_ref):
    j = pl.program_id(1)

    @pl.when(j == 0)
    def _():
        h_ref[...] = _rms(x_ref[...], gpre_ref[...]).astype(BF16)
        acc_ref[...] = jnp.zeros_like(acc_ref)

    h = h_ref[...]
    a = _dot(h, w1_ref[...])
    b = _dot(h, w3_ref[...])
    gated = (a * _sigmoid(a)) * b
    acc_ref[...] += _dot(gated.astype(BF16), w2_ref[...])

    @pl.when(j == pl.num_programs(1) - 1)
    def _():
        o_ref[...] = x_ref[...] + _rms(acc_ref[...], gpost_ref[...])


def _ffn(x, g_pre, g_post, w1, w3, w2):
    m, d = x.shape
    dff = w1.shape[1]
    tm = _tile(m, 512)
    tf = _col_tile(dff, 512)
    return pl.pallas_call(
        _ffn_kernel,
        grid=(m // tm, dff // tf),
        in_specs=[pl.BlockSpec((tm, d), lambda i, j: (i, 0)),
                  pl.BlockSpec((1, d), lambda i, j: (0, 0)),
                  pl.BlockSpec((1, d), lambda i, j: (0, 0)),
                  pl.BlockSpec((d, tf), lambda i, j: (0, j)),
                  pl.BlockSpec((d, tf), lambda i, j: (0, j)),
                  pl.BlockSpec((tf, d), lambda i, j: (j, 0))],
        out_specs=pl.BlockSpec((tm, d), lambda i, j: (i, 0)),
        out_shape=jax.ShapeDtypeStruct((m, d), F32),
        scratch_shapes=[pltpu.VMEM((tm, d), BF16), pltpu.VMEM((tm, d), F32)],
        compiler_params=_params("parallel", "arbitrary"),
        name="ffn_swiglu",
    )(x, g_pre.reshape(1, d), g_post.reshape(1, d), w1, w3, w2)


def _pad_rows(w, rows, offset):
    out = jnp.zeros((rows, w.shape[1]), w.dtype)
    return out.at[offset:offset + w.shape[0]].set(w)


def _layer_weights(l, d, w_in, fox_bf, rwkv_mu, rwkv_w0, rwkv_w_up, rwkv_a0, rwkv_a_up, rwkv_g_up, rwkv_k_k,
                   rwkv_k_a, rwkv_r_k, rwkv_gn_g, rwkv_gn_b, gmlp_ln_g, gmlp_ln_b, gmlp_w_s, gmlp_b_s,
                   w_br_fox, w_br_rwkv, w_br_gmlp, w_out, ffn_w1, ffn_w3, ffn_w2, s_new):
    w = w_in[l]
    o_ff = 3 * FOX_W
    o_r = o_ff + FOX_HEADS
    o_g = o_r + RWKV_IN_W
    o_gate = o_g + 2 * GMLP_W
    head_of_lane = jnp.arange(RWKV_W) // RWKV_HD
    e = (head_of_lane[:, None] == jnp.arange(LANE)[None, :]).astype(BF16)
    tril_full = jnp.tril(gmlp_w_s[l])
    blk = jnp.tril(gmlp_w_s[l][:, :s_new, :s_new])
    reps = GMLP_CHUNK // s_new
    eye = jnp.eye(reps, dtype=F32)
    w_sample = jnp.einsum("ab,gij->gaibj", eye, blk).reshape(GMLP_GROUPS, GMLP_CHUNK, GMLP_CHUNK)
    return {
        "wq": w[:, :FOX_W].astype(BF16),
        "wk": w[:, FOX_W:2 * FOX_W].astype(BF16),
        "wv": w[:, 2 * FOX_W:o_ff].astype(BF16),
        "w_ff": jnp.zeros((d, LANE), BF16).at[:, :FOX_HEADS].set(w[:, o_ff:o_r].astype(BF16)),
        "b_ff": jnp.zeros((1, LANE), F32).at[0, :FOX_HEADS].set(fox_bf[l]),
        "w_rwkv": w[:, o_r:o_g].astype(BF16),
        "w_gmlp": w[:, o_g:o_gate].astype(BF16),
        "w_gate": w[:, o_gate:].astype(BF16),
        "rwkv": {
            "mu": rwkv_mu[l].reshape(1, RWKV_IN_W),
            "w0": rwkv_w0[l].reshape(1, RWKV_W),
            "a0": rwkv_a0[l].reshape(1, RWKV_W),
            "k_k": rwkv_k_k[l].reshape(1, RWKV_W),
            "k_a": rwkv_k_a[l].reshape(1, RWKV_W),
            "r_k": rwkv_r_k[l].reshape(1, RWKV_W),
            "gn_g": rwkv_gn_g[l].reshape(1, RWKV_W),
            "gn_b": rwkv_gn_b[l].reshape(1, RWKV_W),
            "w_up": _pad_rows(rwkv_w_up[l], LANE, 0),
            "a_up": _pad_rows(rwkv_a_up[l], LANE, rwkv_w_up.shape[1]),
            "g_up": rwkv_g_up[l],
            "e": e,
            "et": e.T,
        },
        "gmlp": {"ln_g": gmlp_ln_g[l].reshape(1, GMLP_W), "ln_b": gmlp_ln_b[l].reshape(1, GMLP_W)},
        "gmlp_w_prompt": tril_full.astype(BF16),
        "gmlp_b_prompt": gmlp_b_s[l].T,
        "gmlp_w_sample": w_sample.astype(BF16),
        "gmlp_b_sample": jnp.tile(gmlp_b_s[l][:, :s_new].T, (reps, 1)),
        "w_br_fox": w_br_fox[l].astype(BF16),
        "w_br_rwkv": w_br_rwkv[l].astype(BF16),
        "w_br_gmlp": w_br_gmlp[l].astype(BF16),
        "w_out": w_out[l].astype(BF16),
        "ffn_w1": ffn_w1[l].astype(BF16),
        "ffn_w3": ffn_w3[l].astype(BF16),
        "ffn_w2": ffn_w2[l].astype(BF16),
    }


def _states_to_pairs(state):
    n = state.shape[0]
    s = state.reshape(n, RWKV_PAIRS, 2, RWKV_HD, RWKV_HD)
    eye = jnp.eye(2, dtype=state.dtype)
    return jnp.einsum("npavk,ab->npavbk", s, eye).reshape(n, RWKV_PAIRS, LANE, LANE)


def _pairs_to_states(sp):
    n = sp.shape[0]
    s = sp.reshape(n, RWKV_PAIRS, 2, RWKV_HD, 2, RWKV_HD)
    diag = jnp.stack([s[:, :, 0, :, 0, :], s[:, :, 1, :, 1, :]], axis=2)
    return diag.reshape(n, RWKV_HEADS, RWKV_HD, RWKV_HD)


def _trunk_layer(x, lw, pre_mix_g, post_mix_g, pre_ffn_g, post_ffn_g, nseq, t, layer, cache):
    m, d = x.shape
    h = _rmsnorm_cast(x, pre_mix_g)
    q = _matmul(h, lw["wq"], (BF16,), scale=FOX_Q_SCALE, tn_pref=1024)
    k, k_bf = _kv_proj(h, lw["wk"])
    v, v_bf = _kv_proj(h, lw["wv"])
    logf = _forget_logits(h, lw["w_ff"], lw["b_ff"])[:, :FOX_HEADS]
    rw = _matmul(h, lw["w_rwkv"], tn_pref=1664)
    gugv = _matmul(h, lw["w_gmlp"], tn_pref=1024)
    gates = _matmul(h, lw["w_gate"], tn_pref=1536)

    logf_rows = logf.reshape(nseq, t, FOX_HEADS).transpose(0, 2, 1)
    if cache is None:
        c_rows = _cumsum_lanes(logf_rows.reshape(nseq * FOX_HEADS, t), _tile(t, 256), LOG2E)
        c_tm = c_rows.reshape(nseq, FOX_HEADS, t).transpose(0, 2, 1).reshape(m, FOX_HEADS)
        xq, xk = _fox_bias_columns(c_tm)
        o_a = _fox_prompt(q, xq, k_bf, xk, v_bf, nseq, t)
        prev = jnp.zeros((nseq, 1, RWKV_IN_W), F32)
        s0 = jnp.zeros((nseq, RWKV_PAIRS, LANE, LANE), F32)
        chunk = 64
        gm_w, gm_b = lw["gmlp_w_prompt"], lw["gmlp_b_prompt"]
    else:
        past = cache["k"].shape[2]
        total = past + t
        tpad = -(-total // LANE) * LANE
        lf_all = jnp.concatenate([cache["logf"][layer].astype(F32).transpose(0, 2, 1), logf_rows], axis=-1)
        lf_all = jnp.pad(lf_all, ((0, 0), (0, 0), (0, tpad - total)))
        c_rows = _cumsum_lanes(lf_all.reshape(nseq * FOX_HEADS, tpad), LANE, LOG2E).reshape(nseq, FOX_HEADS, tpad)
        c_tm = c_rows[:, :, past:total].transpose(0, 2, 1).reshape(m, FOX_HEADS)
        o_a = _fox_sample(q, k_bf, v_bf, cache["k"], cache["v"], layer, c_tm, c_rows, nseq, t, past)
        prev = cache["shift"][layer]
        s0 = _states_to_pairs(cache["state"][layer])
        chunk = t
        gm_w, gm_b = lw["gmlp_w_sample"], lw["gmlp_b_sample"]

    r, lwd, k2, v2, a_s, b_s, g = _rwkv_prep(rw, prev, lw["rwkv"], nseq, t)
    o_b, s_fin = _rwkv_chunks(r, lwd, k2, v2, a_s, b_s, g, lw["rwkv"], s0, nseq, t, chunk)
    s_new = _pairs_to_states(s_fin)
    shift_new = rw.reshape(nseq, t, RWKV_IN_W)[:, t - 1:t, :]

    o_c, vn = _gmlp(gugv, lw["gmlp"], gm_w, gm_b, cache is not None)

    mm = _merge(o_a, o_b, o_c, gates, lw["w_br_fox"], lw["w_br_rwkv"], lw["w_br_gmlp"])
    x = _out_proj(mm, lw["w_out"], x, post_mix_g)
    x = _ffn(x, pre_ffn_g, post_ffn_g, lw["ffn_w1"], lw["ffn_w3"], lw["ffn_w2"])
    return x, (k, v, logf, s_new, shift_new, vn)


def kernel(x_prompt, x_sample, cache_fox_k, cache_fox_v, cache_fox_logf, state_rwkv, state_rwkv_shift, pre_mix_g, w_in, fox_bf, rwkv_mu, rwkv_w0, rwkv_w_up, rwkv_a0, rwkv_a_up, rwkv_g_up, rwkv_k_k, rwkv_k_a, rwkv_r_k, rwkv_gn_g, rwkv_gn_b, gmlp_ln_g, gmlp_ln_b, gmlp_w_s, gmlp_b_s, w_br_fox, w_br_rwkv, w_br_gmlp, w_out, post_mix_g, pre_ffn_g, ffn_w1, ffn_w3, ffn_w2, post_ffn_g):
    batch, seq, d = x_prompt.shape
    dec_batch, dec_seq, _ = x_sample.shape
    depth = w_in.shape[0]
    assert rwkv_w_up.shape[1] + rwkv_a_up.shape[1] == LANE and rwkv_g_up.shape[1] == LANE
    assert seq % GMLP_CHUNK == 0 and GMLP_CHUNK % dec_seq == 0 and (dec_batch * dec_seq) % GMLP_CHUNK == 0

    xp = x_prompt.reshape(batch * seq, d)
    xs = x_sample.reshape(dec_batch * dec_seq, d)
    cache = {"k": cache_fox_k, "v": cache_fox_v, "logf": cache_fox_logf,
             "state": state_rwkv, "shift": state_rwkv_shift}
    outs_p = [[] for _ in range(5)]
    outs_s = [[] for _ in range(6)]
    for l in range(depth):
        lw = _layer_weights(l, d, w_in, fox_bf, rwkv_mu, rwkv_w0, rwkv_w_up, rwkv_a0, rwkv_a_up, rwkv_g_up,
                            rwkv_k_k, rwkv_k_a, rwkv_r_k, rwkv_gn_g, rwkv_gn_b, gmlp_ln_g, gmlp_ln_b, gmlp_w_s,
                            gmlp_b_s, w_br_fox, w_br_rwkv, w_br_gmlp, w_out, ffn_w1, ffn_w3, ffn_w2, dec_seq)
        norms = (pre_mix_g[l], post_mix_g[l], pre_ffn_g[l], post_ffn_g[l])
        xp, (k_p, v_p, lf_p, s_p, sh_p, _) = _trunk_layer(xp, lw, *norms, batch, seq, l, None)
        xs, (k_s, v_s, lf_s, s_s, sh_s, vn_s) = _trunk_layer(xs, lw, *norms, dec_batch, dec_seq, l, cache)
        for acc, val in zip(outs_p, (k_p.reshape(batch, seq, FOX_HEADS, FOX_HD), v_p.reshape(batch, seq, FOX_HEADS, FOX_HD),
                                     lf_p.reshape(batch, seq, FOX_HEADS), s_p, sh_p)):
            acc.append(val)
        for acc, val in zip(outs_s, (k_s.reshape(dec_batch, dec_seq, FOX_HEADS, FOX_HD),
                                     v_s.reshape(dec_batch, dec_seq, FOX_HEADS, FOX_HD),
                                     lf_s.reshape(dec_batch, dec_seq, FOX_HEADS), s_s, sh_s,
                                     vn_s.reshape(dec_batch, dec_seq, GMLP_W))):
            acc.append(val)
    return (xp.reshape(batch, seq, d), xs.reshape(dec_batch, dec_seq, d),
            *[jnp.stack(o) for o in outs_p], *[jnp.stack(o) for o in outs_s])
```

```python
import functools
import math

import jax
import jax.numpy as jnp
from jax import lax
from jax.experimental import pallas as pl
from jax.experimental.pallas import tpu as pltpu

F32 = jnp.float32
BF16 = jnp.bfloat16

RMS_EPS = 1e-6
LN_EPS = 1e-5
RWKV_GN_EPS = 64e-5
NEG_INF = -1e30
LOG2E = math.log2(math.e)

LANE = 128
FOX_HEADS = 8
FOX_HD = 128
FOX_W = FOX_HEADS * FOX_HD
FOX_Q_SCALE = FOX_HD ** -0.5 * LOG2E
RWKV_HEADS = 16
RWKV_HD = 64
RWKV_W = RWKV_HEADS * RWKV_HD
RWKV_PAIRS = RWKV_W // LANE
RWKV_LORA_OFF = 3 * RWKV_W
RWKV_IN_W = 3 * RWKV_W + 2 * LANE
GMLP_W = 1024
GMLP_GROUPS = 8
GMLP_GC = GMLP_W // GMLP_GROUPS
GMLP_CHUNK = 128
N_BRANCH = 3

VMEM_LIMIT = 48 * 1024 * 1024


def _params(*sem):
    return pltpu.CompilerParams(dimension_semantics=sem, vmem_limit_bytes=VMEM_LIMIT)


def _tile(n, pref):
    if n <= pref:
        return n
    t = pref
    while n % t:
        t -= 8
    return t


def _col_tile(n, pref):
    assert n % LANE == 0
    k = n // LANE
    best = 1
    for d in range(1, k + 1):
        if k % d == 0 and d * LANE <= pref:
            best = d
    return best * LANE


def _split3(x):
    hi = x.astype(BF16)
    r = x - hi.astype(F32)
    mid = r.astype(BF16)
    lo = (r - mid.astype(F32)).astype(BF16)
    return hi, mid, lo


_NN = (((1,), (0,)), ((), ()))
_NT = (((1,), (1,)), ((), ()))
_TN = (((0,), (0,)), ((), ()))


def _dot(a, b, dims=_NN):
    return lax.dot_general(a, b, dims, preferred_element_type=F32)


def _dot_exact_rhs(x, e):
    hi, mid, lo = _split3(x)
    return _dot(hi, e) + _dot(mid, e) + _dot(lo, e)


def _sigmoid(x):
    return 1.0 / (1.0 + jnp.exp(-x))


def _log_sigmoid(x):
    return jnp.minimum(x, 0.0) - jnp.log(1.0 + jnp.exp(-jnp.abs(x)))


def _gelu_tanh(x):
    return 0.5 * x * (1.0 + jnp.tanh(0.7978845608028654 * (x + 0.044715 * (x * x * x))))


def _rms(x, g):
    ms = jnp.mean(x * x, axis=-1, keepdims=True)
    return x * lax.rsqrt(ms + RMS_EPS) * g


def _rmsnorm_cast_kernel(x_ref, g_ref, o_ref):
    o_ref[...] = _rms(x_ref[...], g_ref[...]).astype(o_ref.dtype)


def _rmsnorm_cast(x, g):
    m, d = x.shape
    tm = _tile(m, 512)
    return pl.pallas_call(
        _rmsnorm_cast_kernel,
        grid=(m // tm,),
        in_specs=[pl.BlockSpec((tm, d), lambda i: (i, 0)),
                  pl.BlockSpec((1, d), lambda i: (0, 0))],
        out_specs=pl.BlockSpec((tm, d), lambda i: (i, 0)),
        out_shape=jax.ShapeDtypeStruct((m, d), BF16),
        compiler_params=_params("parallel"),
        name="rmsnorm_cast",
    )(x, g.reshape(1, d))


def _matmul_kernel(a_ref, w_ref, *o_refs, scale):
    acc = _dot(a_ref[...], w_ref[...])
    if scale is not None:
        acc = acc * scale
    for o_ref in o_refs:
        o_ref[...] = acc.astype(o_ref.dtype)


def _matmul(a, w, out_dtypes=(F32,), scale=None, tm_pref=1024, tn_pref=512):
    m, k = a.shape
    n = w.shape[1]
    tm = _tile(m, tm_pref)
    tn = _col_tile(n, tn_pref)
    o_spec = pl.BlockSpec((tm, tn), lambda i, j: (i, j))
    res = pl.pallas_call(
        functools.partial(_matmul_kernel, scale=scale),
        grid=(m // tm, n // tn),
        in_specs=[pl.BlockSpec((tm, k), lambda i, j: (i, 0)),
                  pl.BlockSpec((k, tn), lambda i, j: (0, j))],
        out_specs=[o_spec] * len(out_dtypes),
        out_shape=[jax.ShapeDtypeStruct((m, n), dt) for dt in out_dtypes],
        compiler_params=_params("parallel", "parallel"),
        name="proj_matmul",
    )(a, w)
    return res[0] if len(out_dtypes) == 1 else res


def _kv_proj_kernel(a_ref, w_ref, o_ref, obf_ref):
    acc = _dot(a_ref[...], w_ref[...])
    obf_ref[...] = acc.astype(obf_ref.dtype)
    for h in range(FOX_HEADS):
        o_ref[:, h, :] = acc[:, h * FOX_HD:(h + 1) * FOX_HD]


def _kv_proj(a, w):
    m, k = a.shape
    tm = _tile(m, 512)
    return pl.pallas_call(
        _kv_proj_kernel,
        grid=(m // tm,),
        in_specs=[pl.BlockSpec((tm, k), lambda i: (i, 0)),
                  pl.BlockSpec((k, FOX_W), lambda i: (0, 0))],
        out_specs=[pl.BlockSpec((tm, FOX_HEADS, FOX_HD), lambda i: (i, 0, 0)),
                   pl.BlockSpec((tm, FOX_W), lambda i: (i, 0))],
        out_shape=[jax.ShapeDtypeStruct((m, FOX_HEADS, FOX_HD), F32),
                   jax.ShapeDtypeStruct((m, FOX_W), BF16)],
        compiler_params=_params("parallel"),
        name="kv_proj",
    )(a, w)


def _logf_kernel(a_ref, w_ref, b_ref, o_ref):
    o_ref[...] = _log_sigmoid(_dot(a_ref[...], w_ref[...]) + b_ref[...])


def _forget_logits(h, w_ff, b_ff):
    m, k = h.shape
    tm = _tile(m, 1024)
    return pl.pallas_call(
        _logf_kernel,
        grid=(m // tm,),
        in_specs=[pl.BlockSpec((tm, k), lambda i: (i, 0)),
                  pl.BlockSpec((k, LANE), lambda i: (0, 0)),
                  pl.BlockSpec((1, LANE), lambda i: (0, 0))],
        out_specs=pl.BlockSpec((tm, LANE), lambda i: (i, 0)),
        out_shape=jax.ShapeDtypeStruct((m, LANE), F32),
        compiler_params=_params("parallel"),
        name="forget_logits",
    )(h, w_ff, b_ff)


def _cumsum_kernel(x_ref, o_ref, carry_ref, *, tb, scale):
    @pl.when(pl.program_id(0) == 0)
    def _():
        carry_ref[...] = jnp.zeros_like(carry_ref)

    row = lax.broadcasted_iota(jnp.int32, (tb, tb), 0)
    col = lax.broadcasted_iota(jnp.int32, (tb, tb), 1)
    upper = jnp.where(row <= col, 1.0, 0.0).astype(BF16)
    c = _dot_exact_rhs(x_ref[...], upper) + carry_ref[:, :1]
    o_ref[...] = c * scale
    carry_ref[...] = jnp.broadcast_to(c[:, tb - 1:tb], carry_ref.shape)


def _cumsum_lanes(x, tb, scale):
    r, t = x.shape
    return pl.pallas_call(
        functools.partial(_cumsum_kernel, tb=tb, scale=scale),
        grid=(t // tb,),
        in_specs=[pl.BlockSpec((r, tb), lambda i: (0, i))],
        out_specs=pl.BlockSpec((r, tb), lambda i: (0, i)),
        out_shape=jax.ShapeDtypeStruct((r, t), F32),
        scratch_shapes=[pltpu.VMEM((r, LANE), F32)],
        compiler_params=_params("arbitrary"),
        name="logf_cumsum",
    )(x)


def _fox_bias_kernel(c_ref, xq_ref, xk_ref):
    tr = c_ref.shape[0]
    lane = lax.broadcasted_iota(jnp.int32, (tr, LANE), 1)
    for h in range(FOX_HEADS):
        hi, mid, lo = [p.astype(F32) for p in _split3(c_ref[:, h:h + 1])]
        ones = jnp.where(lane < 6, 1.0, 0.0)
        xq = jnp.where(lane == 0, hi, jnp.where(lane == 1, mid, jnp.where(lane == 2, lo, ones)))
        xk = jnp.where(lane == 3, -hi, jnp.where(lane == 4, -mid, jnp.where(lane == 5, -lo, ones)))
        xq_ref[:, h * LANE:(h + 1) * LANE] = xq.astype(BF16)
        xk_ref[:, h * LANE:(h + 1) * LANE] = xk.astype(BF16)


def _fox_bias_columns(c_tm):
    m = c_tm.shape[0]
    tr = _tile(m, 512)
    spec = pl.BlockSpec((tr, FOX_W), lambda i: (i, 0))
    return pl.pallas_call(
        _fox_bias_kernel,
        grid=(m // tr,),
        in_specs=[pl.BlockSpec((tr, FOX_HEADS), lambda i: (i, 0))],
        out_specs=[spec, spec],
        out_shape=[jax.ShapeDtypeStruct((m, FOX_W), BF16)] * 2,
        compiler_params=_params("parallel"),
        name="fox_bias_columns",
    )(c_tm)


def _fox_prompt_kernel(qi_ref, ki_ref, q_ref, xq_ref, k_ref, xk_ref, v_ref, o_ref, m_ref, l_ref, acc_ref, *, tq):
    step = pl.program_id(1)
    qi = qi_ref[step]
    ki = ki_ref[step]

    @pl.when(ki == 0)
    def _():
        m_ref[...] = jnp.full_like(m_ref, NEG_INF)
        l_ref[...] = jnp.zeros_like(l_ref)
        acc_ref[...] = jnp.zeros_like(acc_ref)

    def update(diagonal):
        if diagonal:
            row = lax.broadcasted_iota(jnp.int32, (tq, tq), 0)
            col = lax.broadcasted_iota(jnp.int32, (tq, tq), 1)
            causal = col <= row
        heads = range(FOX_HEADS)
        cols = [slice(h * FOX_HD, (h + 1) * FOX_HD) for h in heads]
        s = [_dot(jnp.concatenate([q_ref[:, hs], xq_ref[:, hs]], axis=1),
                  jnp.concatenate([k_ref[:, hs], xk_ref[:, hs]], axis=1), _NT) for hs in cols]
        if diagonal:
            s = [jnp.where(causal, x, NEG_INF) for x in s]
        m_prev = [m_ref[h] for h in heads]
        m_new = [jnp.maximum(mp, jnp.max(x, axis=-1, keepdims=True)) for mp, x in zip(m_prev, s)]
        p = [jnp.exp2(x - jnp.concatenate([mn] * (tq // LANE), axis=1)) for x, mn in zip(s, m_new)]
        p_sum = [jnp.sum(x, axis=-1, keepdims=True) for x in p]
        pv = [_dot(x.astype(BF16), v_ref[:, hs]) for x, hs in zip(p, cols)]
        for h in heads:
            alpha = jnp.exp2(m_prev[h] - m_new[h])
            l_ref[h] = alpha * l_ref[h] + p_sum[h]
            acc_ref[:, cols[h]] = alpha * acc_ref[:, cols[h]] + pv[h]
            m_ref[h] = m_new[h]

    @pl.when(ki < qi)
    def _():
        update(False)

    @pl.when(ki == qi)
    def _():
        update(True)
        for h in range(FOX_HEADS):
            hs = slice(h * FOX_HD, (h + 1) * FOX_HD)
            o_ref[:, hs] = (acc_ref[:, hs] / l_ref[h]).astype(o_ref.dtype)


def _fox_prompt(q, xq, k, xk, v, batch, seq):
    tq = _tile(seq, 512)
    assert tq % LANE == 0
    nq = seq // tq
    blocks = [(qi, ki) for qi in range(nq) for ki in range(qi + 1)]
    qi_tab = jnp.asarray([b[0] for b in blocks], jnp.int32)
    ki_tab = jnp.asarray([b[1] for b in blocks], jnp.int32)
    q_map = lambda b, s, qt, kt: (b * nq + qt[s], 0)
    kv_map = lambda b, s, qt, kt: (b * nq + kt[s], 0)
    return pl.pallas_call(
        functools.partial(_fox_prompt_kernel, tq=tq),
        grid_spec=pltpu.PrefetchScalarGridSpec(
            num_scalar_prefetch=2,
            grid=(batch, len(blocks)),
            in_specs=[pl.BlockSpec((tq, FOX_W), q_map),
                      pl.BlockSpec((tq, FOX_W), q_map),
                      pl.BlockSpec((tq, FOX_W), kv_map),
                      pl.BlockSpec((tq, FOX_W), kv_map),
                      pl.BlockSpec((tq, FOX_W), kv_map)],
            out_specs=pl.BlockSpec((tq, FOX_W), q_map),
            scratch_shapes=[pltpu.VMEM((FOX_HEADS, tq, LANE), F32),
                            pltpu.VMEM((FOX_HEADS, tq, LANE), F32),
                            pltpu.VMEM((tq, FOX_W), F32)]),
        out_shape=jax.ShapeDtypeStruct((batch * seq, FOX_W), BF16),
        compiler_params=_params("parallel", "arbitrary"),
        name="fox_prompt",
    )(qi_tab, ki_tab, q, xq, k, xk, v)


def _fox_sample_kernel(q_ref, kn_ref, vn_ref, cq_ref, ck_ref, kc_ref, vc_ref, o_ref, *, past, s_new):
    kc = pltpu.einshape("thd->htd", kc_ref[0, 0])
    vc = pltpu.einshape("thd->htd", vc_ref[0, 0])
    row = lax.broadcasted_iota(jnp.int32, (s_new, s_new), 0)
    col = lax.broadcasted_iota(jnp.int32, (s_new, s_new), 1)
    causal = col <= row
    for h in range(FOX_HEADS):
        hs = slice(h * FOX_HD, (h + 1) * FOX_HD)
        q = q_ref[:, hs]
        cq = cq_ref[:, h:h + 1]
        s_c = _dot(q, kc[h].astype(BF16), _NT) + (cq - ck_ref[0, h:h + 1, :past])
        s_n = _dot(q, kn_ref[:, hs], _NT) + (cq - ck_ref[0, h:h + 1, past:past + s_new])
        s_n = jnp.where(causal, s_n, NEG_INF)
        m = jnp.maximum(jnp.max(s_c, axis=-1, keepdims=True), jnp.max(s_n, axis=-1, keepdims=True))
        p_c = jnp.exp2(s_c - m)
        p_n = jnp.exp2(s_n - m)
        l = jnp.sum(p_c, axis=-1, keepdims=True) + jnp.sum(p_n, axis=-1, keepdims=True)
        o = _dot(p_c.astype(BF16), vc[h].astype(BF16)) + _dot(p_n.astype(BF16), vn_ref[:, hs])
        o_ref[:, hs] = (o / l).astype(o_ref.dtype)


def _fox_sample(q, k, v, k_cache, v_cache, layer, c_tm, c_rows, batch, s_new, past):
    tpad = c_rows.shape[-1]
    new_spec = pl.BlockSpec((s_new, FOX_W), lambda b: (b, 0))
    cache_spec = pl.BlockSpec((1, 1, past, FOX_HEADS, FOX_HD), lambda b: (layer, b, 0, 0, 0))
    return pl.pallas_call(
        functools.partial(_fox_sample_kernel, past=past, s_new=s_new),
        grid=(batch,),
        in_specs=[new_spec, new_spec, new_spec,
                  pl.BlockSpec((s_new, FOX_HEADS), lambda b: (b, 0)),
                  pl.BlockSpec((1, FOX_HEADS, tpad), lambda b: (b, 0, 0)), cache_spec, cache_spec],
        out_specs=new_spec,
        out_shape=jax.ShapeDtypeStruct((batch * s_new, FOX_W), BF16),
        compiler_params=_params("parallel"),
        name="fox_sample",
    )(q, k, v, c_tm, c_rows, k_cache, v_cache)


def _rwkv_prep_kernel(p_ref, prev_ref, mu_ref, w0_ref, a0_ref, kk_ref, ka_ref, wup_ref, aup_ref, gup_ref,
                      r_ref, lw_ref, k_ref, v_ref, kk_out_ref, a_ref, g_ref, carry_ref, *, tr):
    @pl.when(pl.program_id(1) == 0)
    def _():
        carry_ref[...] = prev_ref[0]

    p = p_ref[0]
    rows = lax.broadcasted_iota(jnp.int32, (tr, 1), 0)
    p_prev = jnp.where(rows == 0, carry_ref[...], pltpu.roll(p, shift=1, axis=0))
    carry_ref[...] = p[tr - 1:tr, :]
    xs = p + (p_prev - p) * mu_ref[...]

    r = xs[:, :RWKV_W]
    k = xs[:, RWKV_W:2 * RWKV_W]
    v = xs[:, 2 * RWKV_W:3 * RWKV_W]
    d_wa = xs[:, RWKV_LORA_OFF:RWKV_LORA_OFF + LANE]
    d_g = xs[:, RWKV_LORA_OFF + LANE:RWKV_LORA_OFF + 2 * LANE]

    z_w = w0_ref[...] + _dot(jnp.tanh(d_wa).astype(BF16), wup_ref[...])
    log_decay = -jnp.exp(_log_sigmoid(z_w) - 0.5)
    a = _sigmoid(a0_ref[...] + _dot(d_wa.astype(BF16), aup_ref[...]))
    g = _dot(_sigmoid(d_g).astype(BF16), gup_ref[...])

    r_ref[...] = r.astype(r_ref.dtype)
    lw_ref[...] = log_decay
    k_ref[...] = (k * (1.0 + (a - 1.0) * ka_ref[...])).astype(k_ref.dtype)
    v_ref[...] = v.astype(v_ref.dtype)
    kk_out_ref[...] = (k * kk_ref[...]).astype(kk_out_ref.dtype)
    a_ref[...] = a.astype(a_ref.dtype)
    g_ref[...] = g.astype(g_ref.dtype)


def _rwkv_prep(rw, prev, lw, nseq, t):
    tr = _tile(t, 256)
    nb = t // tr
    wide = lambda s, i: (0, 0)
    row_spec = pl.BlockSpec((tr, RWKV_W), lambda s, i: (s * nb + i, 0))
    out_dtypes = (BF16, F32, BF16, BF16, BF16, BF16, BF16)
    return pl.pallas_call(
        functools.partial(_rwkv_prep_kernel, tr=tr),
        grid=(nseq, nb),
        in_specs=[pl.BlockSpec((1, tr, RWKV_IN_W), lambda s, i: (s, i, 0)),
                  pl.BlockSpec((1, 1, RWKV_IN_W), lambda s, i: (s, 0, 0)),
                  pl.BlockSpec((1, RWKV_IN_W), wide),
                  pl.BlockSpec((1, RWKV_W), wide),
                  pl.BlockSpec((1, RWKV_W), wide),
                  pl.BlockSpec((1, RWKV_W), wide),
                  pl.BlockSpec((1, RWKV_W), wide),
                  pl.BlockSpec((LANE, RWKV_W), wide),
                  pl.BlockSpec((LANE, RWKV_W), wide),
                  pl.BlockSpec((LANE, RWKV_W), wide)],
        out_specs=[row_spec] * 7,
        out_shape=[jax.ShapeDtypeStruct((nseq * t, RWKV_W), dt) for dt in out_dtypes],
        scratch_shapes=[pltpu.VMEM((1, RWKV_IN_W), F32)],
        compiler_params=_params("parallel", "arbitrary"),
        name="rwkv_prep",
    )(rw.reshape(nseq, t, RWKV_IN_W), prev, lw["mu"], lw["w0"], lw["a0"], lw["k_k"], lw["k_a"],
      lw["w_up"], lw["a_up"], lw["g_up"])


def _rwkv_chunk_kernel(*refs, c, has_s0):
    r_ref, lw_ref, k_ref, v_ref, kk_ref, lr_ref, g_ref, gng_ref, gnb_ref, rk_ref = refs[:10]
    s0_ref = refs[10] if has_s0 else None
    o_ref, sout_ref, s_ref = refs[-3:]
    ci = pl.program_id(1)
    pairs = range(RWKV_PAIRS)

    @pl.when(ci == 0)
    def _():
        if has_s0:
            zero = jnp.zeros((RWKV_HD, RWKV_HD), F32)
            for pr in pairs:
                top = jnp.concatenate([s0_ref[0, 0, 2 * pr], zero], axis=1)
                bot = jnp.concatenate([zero, s0_ref[0, 0, 2 * pr + 1]], axis=1)
                s_ref[pr] = jnp.concatenate([top, bot], axis=0)
        else:
            s_ref[...] = jnp.zeros_like(s_ref)

    c2 = 2 * c
    row = lax.broadcasted_iota(jnp.int32, (c2, c2), 0)
    col = lax.broadcasted_iota(jnp.int32, (c2, c2), 1)
    lower = col <= row
    strict = col < row
    eye = jnp.where(row == col, 1.0, 0.0)
    rc = lax.broadcasted_iota(jnp.int32, (c, c), 0)
    cc = lax.broadcasted_iota(jnp.int32, (c, c), 1)
    tri = jnp.where(cc <= rc, 1.0, 0.0).astype(BF16)
    lane = lax.broadcasted_iota(jnp.int32, (c, LANE), 1)
    first = lane < RWKV_HD
    steps = max(c.bit_length() - 2, 0)

    def stack(x):
        return jnp.concatenate([jnp.where(first, x, 0.0), jnp.where(first, 0.0, x)], axis=0)

    own = (lax.broadcasted_iota(jnp.int32, (c2, LANE), 1) // RWKV_HD
           == lax.broadcasted_iota(jnp.int32, (c2, LANE), 0) // c)

    lanes = [slice(pr * LANE, (pr + 1) * LANE) for pr in pairs]
    s_mats = [s_ref[pr] for pr in pairs]
    lws = [lw_ref[:, ls] for ls in lanes]
    cum3 = [_dot(tri, jnp.concatenate(_split3(lw), axis=1)) for lw in lws]
    cums = [x[:, :LANE] + x[:, LANE:2 * LANE] + x[:, 2 * LANE:] for x in cum3]
    totals = [cum[c - 1:c, :] for cum in cums]

    lhs, rhs, bar, v_s, rk_s = [], [], [], [], []
    for ls, lw, cum, total in zip(lanes, lws, cums, totals):
        grow = jnp.exp(-cum)
        rest = jnp.exp(total - cum)
        r = r_ref[:, ls].astype(F32)
        k = k_ref[:, ls].astype(F32)
        kk = kk_ref[:, ls].astype(F32)
        norm = lax.rsqrt(jnp.maximum(jnp.sum(stack(kk * kk), axis=-1, keepdims=True), 1e-24))
        kk = kk * jnp.where(first, norm[:c], norm[c:])
        b = kk * lr_ref[:, ls].astype(F32)
        lhs.append(jnp.concatenate([stack(-kk * jnp.exp(cum - lw)), stack(r * jnp.exp(cum))], axis=0).astype(BF16))
        rhs.append(jnp.concatenate([stack(k * grow), stack(b * grow)], axis=0).astype(BF16))
        bar.append(jnp.concatenate([stack(k * rest), stack(b * rest)], axis=0).astype(BF16))
        v_s.append(stack(v_ref[:, ls].astype(F32)))
        rk_s.append(stack(r * k * rk_ref[:, ls]))

    amats = [_dot(x, y, _NT) for x, y in zip(lhs, rhs)]
    sxs = [_dot(x, s.astype(BF16), _NT) for x, s in zip(lhs, s_mats)]
    a_ab = [jnp.where(strict, m[:c2, c2:], 0.0) for m in amats]
    invs = [eye + n for n in a_ab]
    powers = [_dot(n.astype(BF16), n.astype(BF16)) for n in a_ab]
    for i in range(steps):
        if i < steps - 1:
            xs = [_dot(jnp.concatenate([inv, pw], axis=0).astype(BF16), pw.astype(BF16))
                  for inv, pw in zip(invs, powers)]
            invs = [inv + x[:c2] for inv, x in zip(invs, xs)]
            powers = [x[c2:] for x in xs]
        else:
            invs = [inv + _dot(inv.astype(BF16), pw.astype(BF16)) for inv, pw in zip(invs, powers)]

    akv = [_dot(jnp.where(strict, m[:c2, :c2], 0.0).astype(BF16), v.astype(BF16)) for m, v in zip(amats, v_s)]
    us = [_dot(inv.astype(BF16), (sx[:c2] + x).astype(BF16)) for inv, sx, x in zip(invs, sxs, akv)]
    vus = [jnp.concatenate([v, u], axis=0).astype(BF16) for v, u in zip(v_s, us)]
    for pr in pairs:
        m = amats[pr]
        a_r = jnp.concatenate([jnp.where(lower, m[c2:, :c2], 0.0), jnp.where(lower, m[c2:, c2:], 0.0)], axis=1)
        y_s = sxs[pr][c2:] + _dot(a_r.astype(BF16), vus[pr])
        s_ref[pr] = s_mats[pr] * jnp.exp(totals[pr]) + _dot(vus[pr], bar[pr], _TN)
        ls = lanes[pr]
        mu = jnp.sum(y_s, axis=-1, keepdims=True) * (1.0 / RWKV_HD)
        d = jnp.where(own, y_s - mu, 0.0)
        var = jnp.sum(d * d, axis=-1, keepdims=True) * (1.0 / RWKV_HD)
        yn_s = d * lax.rsqrt(var + RWKV_GN_EPS)
        bonus_s = jnp.sum(rk_s[pr], axis=-1, keepdims=True) * v_s[pr]
        yn = (yn_s[:c] + yn_s[c:]) * gng_ref[:, ls] + gnb_ref[:, ls]
        o_ref[:, ls] = ((yn + (bonus_s[:c] + bonus_s[c:])) * g_ref[:, ls].astype(F32)).astype(o_ref.dtype)

    @pl.when(ci == pl.num_programs(1) - 1)
    def _():
        for pr in pairs:
            s_mat = s_ref[pr]
            sout_ref[0, 2 * pr] = s_mat[:RWKV_HD, :RWKV_HD]
            sout_ref[0, 2 * pr + 1] = s_mat[RWKV_HD:, RWKV_HD:]


def _rwkv_chunks(r, lwd, k, v, kk, lr, g, lw, state, layer, nseq, t, c):
    nc = t // c
    row_spec = pl.BlockSpec((c, RWKV_W), lambda s, i: (s * nc + i, 0))
    par_spec = pl.BlockSpec((1, RWKV_W), lambda s, i: (0, 0))
    in_specs = [row_spec] * 7 + [par_spec] * 3
    args = [r, lwd, k, v, kk, lr, g, lw["gn_g"], lw["gn_b"], lw["r_k"]]
    if state is not None:
        in_specs.append(pl.BlockSpec((1, 1, RWKV_HEADS, RWKV_HD, RWKV_HD), lambda s, i: (layer, s, 0, 0, 0)))
        args.append(state)
    return pl.pallas_call(
        functools.partial(_rwkv_chunk_kernel, c=c, has_s0=state is not None),
        grid=(nseq, nc),
        in_specs=in_specs,
        out_specs=[row_spec, pl.BlockSpec((1, RWKV_HEADS, RWKV_HD, RWKV_HD), lambda s, i: (s, 0, 0, 0))],
        out_shape=[jax.ShapeDtypeStruct((nseq * t, RWKV_W), BF16),
                   jax.ShapeDtypeStruct((nseq, RWKV_HEADS, RWKV_HD, RWKV_HD), F32)],
        scratch_shapes=[pltpu.VMEM((RWKV_PAIRS, LANE, LANE), F32)],
        compiler_params=_params("parallel", "arbitrary"),
        name="rwkv_chunks",
    )(*args)


def _gmlp_kernel(gugv_ref, lng_ref, lnb_ref, ws_ref, bs_ref, o_ref, *vn_out):
    for blk in range(o_ref.shape[0] // GMLP_CHUNK):
        rows = slice(blk * GMLP_CHUNK, (blk + 1) * GMLP_CHUNK)
        u = _gelu_tanh(gugv_ref[rows, :GMLP_W])
        gv = _gelu_tanh(gugv_ref[rows, GMLP_W:])
        mu = jnp.mean(gv, axis=-1, keepdims=True)
        d = gv - mu
        var = jnp.mean(d * d, axis=-1, keepdims=True)
        vn = d * lax.rsqrt(var + LN_EPS) * lng_ref[...] + lnb_ref[...]
        if vn_out:
            vn_out[0][rows, :] = vn
        vb = vn.astype(BF16)
        for g in range(GMLP_GROUPS):
            gs = slice(g * GMLP_GC, (g + 1) * GMLP_GC)
            s = _dot(ws_ref[g], vb[:, gs]) + bs_ref[:, g:g + 1]
            o_ref[rows, gs] = (u[:, gs] * s).astype(o_ref.dtype)


def _gmlp(gugv, lw, w_eff, b_eff, emit_vn):
    m = gugv.shape[0]
    tr = _tile(m, 4 * GMLP_CHUNK)
    assert tr % GMLP_CHUNK == 0
    row_spec = pl.BlockSpec((tr, GMLP_W), lambda i: (i, 0))
    out_shape = [jax.ShapeDtypeStruct((m, GMLP_W), BF16)]
    out_specs = [row_spec]
    if emit_vn:
        out_shape.append(jax.ShapeDtypeStruct((m, GMLP_W), F32))
        out_specs.append(row_spec)
    res = pl.pallas_call(
        _gmlp_kernel,
        grid=(m // tr,),
        in_specs=[pl.BlockSpec((tr, 2 * GMLP_W), lambda i: (i, 0)),
                  pl.BlockSpec((1, GMLP_W), lambda i: (0, 0)),
                  pl.BlockSpec((1, GMLP_W), lambda i: (0, 0)),
                  pl.BlockSpec((GMLP_GROUPS, GMLP_CHUNK, GMLP_CHUNK), lambda i: (0, 0, 0)),
                  pl.BlockSpec((GMLP_CHUNK, GMLP_GROUPS), lambda i: (0, 0))],
        out_specs=out_specs,
        out_shape=out_shape,
        compiler_params=_params("parallel"),
        name="gmlp_sgu",
    )(gugv, lw["ln_g"], lw["ln_b"], w_eff, b_eff)
    return res if emit_vn else (res[0], None)


def _merge_kernel(oa_ref, ob_ref, oc_ref, ga_ref, gb_ref, gc_ref, wa_ref, wb_ref, wc_ref, o_ref):
    m = _sigmoid(ga_ref[...]) * _dot(oa_ref[...], wa_ref[...])
    m = m + _sigmoid(gb_ref[...]) * _dot(ob_ref[...], wb_ref[...])
    m = m + _sigmoid(gc_ref[...]) * _dot(oc_ref[...], wc_ref[...])
    o_ref[...] = m.astype(o_ref.dtype)


def _merge(o_a, o_b, o_c, gates, w_a, w_b, w_c):
    m, kw = o_a.shape
    d = w_a.shape[1]
    tm = _tile(m, 512)
    tn = _col_tile(d, 1024)
    nj = d // tn
    o_spec = pl.BlockSpec((tm, kw), lambda i, j: (i, 0))
    w_spec = pl.BlockSpec((kw, tn), lambda i, j: (0, j))
    return pl.pallas_call(
        _merge_kernel,
        grid=(m // tm, nj),
        in_specs=[o_spec, o_spec, o_spec,
                  pl.BlockSpec((tm, tn), lambda i, j: (i, j)),
                  pl.BlockSpec((tm, tn), lambda i, j: (i, nj + j)),
                  pl.BlockSpec((tm, tn), lambda i, j: (i, 2 * nj + j)),
                  w_spec, w_spec, w_spec],
        out_specs=pl.BlockSpec((tm, tn), lambda i, j: (i, j)),
        out_shape=jax.ShapeDtypeStruct((m, d), BF16),
        compiler_params=_params("parallel", "parallel"),
        name="branch_merge",
    )(o_a, o_b, o_c, gates, gates, gates, w_a, w_b, w_c)


def _out_proj_kernel(m_ref, w_ref, x_ref, g_ref, o_ref):
    o_ref[...] = x_ref[...] + _rms(_dot(m_ref[...], w_ref[...]), g_ref[...])


def _out_proj(mm, w_out, x, g):
    m, d = x.shape
    tm = _tile(m, 512)
    return pl.pallas_call(
        _out_proj_kernel,
        grid=(m // tm,),
        in_specs=[pl.BlockSpec((tm, d), lambda i: (i, 0)),
                  pl.BlockSpec((d, d), lambda i: (0, 0)),
                  pl.BlockSpec((tm, d), lambda i: (i, 0)),
                  pl.BlockSpec((1, d), lambda i: (0, 0))],
        out_specs=pl.BlockSpec((tm, d), lambda i: (i, 0)),
        out_shape=jax.ShapeDtypeStruct((m, d), F32),
        compiler_params=_params("parallel"),
        name="out_proj_residual",
    )(mm, w_out, x, g.reshape(1, d))


def _ffn_kernel(x_ref, gpre_ref, gpost_ref, w1_ref, w3_ref, w2_ref, o_ref, h_ref, acc_ref):
    j = pl.program_id(1)

    @pl.when(j == 0)
    def _():
        h_ref[...] = _rms(x_ref[...], gpre_ref[...]).astype(BF16)
        acc_ref[...] = jnp.zeros_like(acc_ref)

    h = h_ref[...]
    a = _dot(h, w1_ref[...])
    b = _dot(h, w3_ref[...])
    gated = (a * _sigmoid(a)) * b
    acc_ref[...] += _dot(gated.astype(BF16), w2_ref[...])

    @pl.when(j == pl.num_programs(1) - 1)
    def _():
        o_ref[...] = x_ref[...] + _rms(acc_ref[...], gpost_ref[...])


def _ffn(x, g_pre, g_post, w1, w3, w2):
    m, d = x.shape
    dff = w1.shape[1]
    tm = _tile(m, 512)
    tf = _col_tile(dff, 512)
    return pl.pallas_call(
        _ffn_kernel,
        grid=(m // tm, dff // tf),
        in_specs=[pl.BlockSpec((tm, d), lambda i, j: (i, 0)),
                  pl.BlockSpec((1, d), lambda i, j: (0, 0)),
                  pl.BlockSpec((1, d), lambda i, j: (0, 0)),
                  pl.BlockSpec((d, tf), lambda i, j: (0, j)),
                  pl.BlockSpec((d, tf), lambda i, j: (0, j)),
                  pl.BlockSpec((tf, d), lambda i, j: (j, 0))],
        out_specs=pl.BlockSpec((tm, d), lambda i, j: (i, 0)),
        out_shape=jax.ShapeDtypeStruct((m, d), F32),
        scratch_shapes=[pltpu.VMEM((tm, d), BF16), pltpu.VMEM((tm, d), F32)],
        compiler_params=_params("parallel", "arbitrary"),
        name="ffn_swiglu",
    )(x, g_pre.reshape(1, d), g_post.reshape(1, d), w1, w3, w2)


def _pad_rows(w, rows, offset):
    out = jnp.zeros((rows, w.shape[1]), w.dtype)
    return out.at[offset:offset + w.shape[0]].set(w)


def _layer_weights(l, d, w_in, fox_bf, rwkv_mu, rwkv_w0, rwkv_w_up, rwkv_a0, rwkv_a_up, rwkv_g_up, rwkv_k_k,
                   rwkv_k_a, rwkv_r_k, rwkv_gn_g, rwkv_gn_b, gmlp_ln_g, gmlp_ln_b, gmlp_w_s, gmlp_b_s,
                   w_br_fox, w_br_rwkv, w_br_gmlp, w_out, ffn_w1, ffn_w3, ffn_w2, s_new):
    w = w_in[l]
    o_ff = 3 * FOX_W
    o_r = o_ff + FOX_HEADS
    o_g = o_r + RWKV_IN_W
    o_gate = o_g + 2 * GMLP_W
    tril_full = jnp.tril(gmlp_w_s[l])
    blk = jnp.tril(gmlp_w_s[l][:, :s_new, :s_new])
    reps = GMLP_CHUNK // s_new
    eye = jnp.eye(reps, dtype=F32)
    w_sample = jnp.einsum("ab,gij->gaibj", eye, blk).reshape(GMLP_GROUPS, GMLP_CHUNK, GMLP_CHUNK)
    return {
        "wq": w[:, :FOX_W].astype(BF16),
        "wk": w[:, FOX_W:2 * FOX_W].astype(BF16),
        "wv": w[:, 2 * FOX_W:o_ff].astype(BF16),
        "w_ff": jnp.zeros((d, LANE), BF16).at[:, :FOX_HEADS].set(w[:, o_ff:o_r].astype(BF16)),
        "b_ff": jnp.zeros((1, LANE), F32).at[0, :FOX_HEADS].set(fox_bf[l]),
        "w_rwkv": w[:, o_r:o_g].astype(BF16),
        "w_gmlp": w[:, o_g:o_gate].astype(BF16),
        "w_gate": w[:, o_gate:].astype(BF16),
        "rwkv": {
            "mu": rwkv_mu[l].reshape(1, RWKV_IN_W),
            "w0": rwkv_w0[l].reshape(1, RWKV_W),
            "a0": rwkv_a0[l].reshape(1, RWKV_W),
            "k_k": rwkv_k_k[l].reshape(1, RWKV_W),
            "k_a": rwkv_k_a[l].reshape(1, RWKV_W),
            "r_k": rwkv_r_k[l].reshape(1, RWKV_W),
            "gn_g": rwkv_gn_g[l].reshape(1, RWKV_W),
            "gn_b": rwkv_gn_b[l].reshape(1, RWKV_W),
            "w_up": _pad_rows(rwkv_w_up[l], LANE, 0).astype(BF16),
            "a_up": _pad_rows(rwkv_a_up[l], LANE, rwkv_w_up.shape[1]).astype(BF16),
            "g_up": rwkv_g_up[l].astype(BF16),
        },
        "gmlp": {"ln_g": gmlp_ln_g[l].reshape(1, GMLP_W), "ln_b": gmlp_ln_b[l].reshape(1, GMLP_W)},
        "gmlp_w_prompt": tril_full.astype(BF16),
        "gmlp_b_prompt": gmlp_b_s[l].T,
        "gmlp_w_sample": w_sample.astype(BF16),
        "gmlp_b_sample": jnp.tile(gmlp_b_s[l][:, :s_new].T, (reps, 1)),
        "w_br_fox": w_br_fox[l].astype(BF16),
        "w_br_rwkv": w_br_rwkv[l].astype(BF16),
        "w_br_gmlp": w_br_gmlp[l].astype(BF16),
        "w_out": w_out[l].astype(BF16),
        "ffn_w1": ffn_w1[l].astype(BF16),
        "ffn_w3": ffn_w3[l].astype(BF16),
        "ffn_w2": ffn_w2[l].astype(BF16),
    }


def _trunk_layer(x, lw, pre_mix_g, post_mix_g, pre_ffn_g, post_ffn_g, nseq, t, layer, cache):
    m, d = x.shape
    h = _rmsnorm_cast(x, pre_mix_g)
    q = _matmul(h, lw["wq"], (BF16,), scale=FOX_Q_SCALE, tn_pref=1024)
    k, k_bf = _kv_proj(h, lw["wk"])
    v, v_bf = _kv_proj(h, lw["wv"])
    logf = _forget_logits(h, lw["w_ff"], lw["b_ff"])[:, :FOX_HEADS]
    rw = _matmul(h, lw["w_rwkv"], tn_pref=1664)
    gugv = _matmul(h, lw["w_gmlp"], tn_pref=1024)
    gates = _matmul(h, lw["w_gate"], tn_pref=1536)

    logf_rows = logf.reshape(nseq, t, FOX_HEADS).transpose(0, 2, 1)
    if cache is None:
        c_rows = _cumsum_lanes(logf_rows.reshape(nseq * FOX_HEADS, t), _tile(t, 256), LOG2E)
        c_tm = c_rows.reshape(nseq, FOX_HEADS, t).transpose(0, 2, 1).reshape(m, FOX_HEADS)
        xq, xk = _fox_bias_columns(c_tm)
        o_a = _fox_prompt(q, xq, k_bf, xk, v_bf, nseq, t)
        prev = jnp.zeros((nseq, 1, RWKV_IN_W), F32)
        state = None
        chunk = 64
        gm_w, gm_b = lw["gmlp_w_prompt"], lw["gmlp_b_prompt"]
    else:
        past = cache["k"].shape[2]
        total = past + t
        tpad = -(-total // LANE) * LANE
        lf_all = jnp.concatenate([cache["logf"][layer].astype(F32).transpose(0, 2, 1), logf_rows], axis=-1)
        lf_all = jnp.pad(lf_all, ((0, 0), (0, 0), (0, tpad - total)))
        c_rows = _cumsum_lanes(lf_all.reshape(nseq * FOX_HEADS, tpad), LANE, LOG2E).reshape(nseq, FOX_HEADS, tpad)
        c_tm = c_rows[:, :, past:total].transpose(0, 2, 1).reshape(m, FOX_HEADS)
        o_a = _fox_sample(q, k_bf, v_bf, cache["k"], cache["v"], layer, c_tm, c_rows, nseq, t, past)
        prev = cache["shift"][layer]
        state = cache["state"]
        chunk = t
        gm_w, gm_b = lw["gmlp_w_sample"], lw["gmlp_b_sample"]

    r, lwd, k2, v2, kk, lr, g = _rwkv_prep(rw, prev, lw["rwkv"], nseq, t)
    o_b, s_new = _rwkv_chunks(r, lwd, k2, v2, kk, lr, g, lw["rwkv"], state, layer, nseq, t, chunk)
    shift_new = rw.reshape(nseq, t, RWKV_IN_W)[:, t - 1:t, :]

    o_c, vn = _gmlp(gugv, lw["gmlp"], gm_w, gm_b, cache is not None)

    mm = _merge(o_a, o_b, o_c, gates, lw["w_br_fox"], lw["w_br_rwkv"], lw["w_br_gmlp"])
    x = _out_proj(mm, lw["w_out"], x, post_mix_g)
    x = _ffn(x, pre_ffn_g, post_ffn_g, lw["ffn_w1"], lw["ffn_w3"], lw["ffn_w2"])
    return x, (k, v, logf, s_new, shift_new, vn)


def kernel(x_prompt, x_sample, cache_fox_k, cache_fox_v, cache_fox_logf, state_rwkv, state_rwkv_shift, pre_mix_g, w_in, fox_bf, rwkv_mu, rwkv_w0, rwkv_w_up, rwkv_a0, rwkv_a_up, rwkv_g_up, rwkv_k_k, rwkv_k_a, rwkv_r_k, rwkv_gn_g, rwkv_gn_b, gmlp_ln_g, gmlp_ln_b, gmlp_w_s, gmlp_b_s, w_br_fox, w_br_rwkv, w_br_gmlp, w_out, post_mix_g, pre_ffn_g, ffn_w1, ffn_w3, ffn_w2, post_ffn_g):
    batch, seq, d = x_prompt.shape
    dec_batch, dec_seq, _ = x_sample.shape
    depth = w_in.shape[0]
    assert rwkv_w_up.shape[1] + rwkv_a_up.shape[1] == LANE and rwkv_g_up.shape[1] == LANE
    assert seq % GMLP_CHUNK == 0 and GMLP_CHUNK % dec_seq == 0 and (dec_batch * dec_seq) % GMLP_CHUNK == 0

    xp = x_prompt.reshape(batch * seq, d)
    xs = x_sample.reshape(dec_batch * dec_seq, d)
    cache = {"k": cache_fox_k, "v": cache_fox_v, "logf": cache_fox_logf,
             "state": state_rwkv, "shift": state_rwkv_shift}
    outs_p = [[] for _ in range(5)]
    outs_s = [[] for _ in range(6)]
    for l in range(depth):
        lw = _layer_weights(l, d, w_in, fox_bf, rwkv_mu, rwkv_w0, rwkv_w_up, rwkv_a0, rwkv_a_up, rwkv_g_up,
                            rwkv_k_k, rwkv_k_a, rwkv_r_k, rwkv_gn_g, rwkv_gn_b, gmlp_ln_g, gmlp_ln_b, gmlp_w_s,
                            gmlp_b_s, w_br_fox, w_br_rwkv, w_br_gmlp, w_out, ffn_w1, ffn_w3, ffn_w2, dec_seq)
        norms = (pre_mix_g[l], post_mix_g[l], pre_ffn_g[l], post_ffn_g[l])
        xp, (k_p, v_p, lf_p, s_p, sh_p, _) = _trunk_layer(xp, lw, *norms, batch, seq, l, None)
        xs, (k_s, v_s, lf_s, s_s, sh_s, vn_s) = _trunk_layer(xs, lw, *norms, dec_batch, dec_seq, l, cache)
        for acc, val in zip(outs_p, (k_p.reshape(batch, seq, FOX_HEADS, FOX_HD), v_p.reshape(batch, seq, FOX_HEADS, FOX_HD),
                                     lf_p.reshape(batch, seq, FOX_HEADS), s_p, sh_p)):
            acc.append(val)
        for acc, val in zip(outs_s, (k_s.reshape(dec_batch, dec_seq, FOX_HEADS, FOX_HD),
                                     v_s.reshape(dec_batch, dec_seq, FOX_HEADS, FOX_HD),
                                     lf_s.reshape(dec_batch, dec_seq, FOX_HEADS), s_s, sh_s,
                                     vn_s.reshape(dec_batch, dec_seq, GMLP_W))):
            acc.append(val)
    return (xp.reshape(batch, seq, d), xs.reshape(dec_batch, dec_seq, d),
            *[jnp.stack(o) for o in outs_p], *[jnp.stack(o) for o in outs_s])
```

```python
import functools
import math

import jax
import jax.numpy as jnp
from jax import lax
from jax.experimental import pallas as pl
from jax.experimental.pallas import tpu as pltpu

F32 = jnp.float32
BF16 = jnp.bfloat16

RMS_EPS = 1e-6
LN_EPS = 1e-5
RWKV_GN_EPS = 64e-5
NEG_INF = -1e30
LOG2E = math.log2(math.e)

LANE = 128
FOX_HEADS = 8
FOX_HD = 128
FOX_W = FOX_HEADS * FOX_HD
FOX_Q_SCALE = FOX_HD ** -0.5 * LOG2E
RWKV_HEADS = 16
RWKV_HD = 64
RWKV_W = RWKV_HEADS * RWKV_HD
RWKV_PAIRS = RWKV_W // LANE
RWKV_LORA_OFF = 3 * RWKV_W
RWKV_IN_W = 3 * RWKV_W + 2 * LANE
GMLP_W = 1024
GMLP_GROUPS = 8
GMLP_GC = GMLP_W // GMLP_GROUPS
GMLP_CHUNK = 128
N_BRANCH = 3

VMEM_LIMIT = 48 * 1024 * 1024


def _params(*sem):
    return pltpu.CompilerParams(dimension_semantics=sem, vmem_limit_bytes=VMEM_LIMIT)


def _tile(n, pref):
    if n <= pref:
        return n
    t = pref
    while n % t:
        t -= 8
    return t


def _col_tile(n, pref):
    assert n % LANE == 0
    k = n // LANE
    best = 1
    for d in range(1, k + 1):
        if k % d == 0 and d * LANE <= pref:
            best = d
    return best * LANE


def _split3(x):
    hi = x.astype(BF16)
    r = x - hi.astype(F32)
    mid = r.astype(BF16)
    lo = (r - mid.astype(F32)).astype(BF16)
    return hi, mid, lo


_NN = (((1,), (0,)), ((), ()))
_NT = (((1,), (1,)), ((), ()))
_TN = (((0,), (0,)), ((), ()))


def _dot(a, b, dims=_NN):
    return lax.dot_general(a, b, dims, preferred_element_type=F32)


def _dot_exact_rhs(x, e):
    hi, mid, lo = _split3(x)
    return _dot(hi, e) + _dot(mid, e) + _dot(lo, e)


def _sigmoid(x):
    return 1.0 / (1.0 + jnp.exp(-x))


def _log_sigmoid(x):
    return jnp.minimum(x, 0.0) - jnp.log(1.0 + jnp.exp(-jnp.abs(x)))


def _gelu_tanh(x):
    return 0.5 * x * (1.0 + jnp.tanh(0.7978845608028654 * (x + 0.044715 * (x * x * x))))


def _rms(x, g):
    ms = jnp.mean(x * x, axis=-1, keepdims=True)
    return x * lax.rsqrt(ms + RMS_EPS) * g


def _rmsnorm_cast_kernel(x_ref, g_ref, o_ref):
    o_ref[...] = _rms(x_ref[...], g_ref[...]).astype(o_ref.dtype)


def _rmsnorm_cast(x, g):
    m, d = x.shape
    tm = _tile(m, 512)
    return pl.pallas_call(
        _rmsnorm_cast_kernel,
        grid=(m // tm,),
        in_specs=[pl.BlockSpec((tm, d), lambda i: (i, 0)),
                  pl.BlockSpec((1, d), lambda i: (0, 0))],
        out_specs=pl.BlockSpec((tm, d), lambda i: (i, 0)),
        out_shape=jax.ShapeDtypeStruct((m, d), BF16),
        compiler_params=_params("parallel"),
        name="rmsnorm_cast",
    )(x, g.reshape(1, d))


def _matmul_kernel(a_ref, w_ref, *o_refs, scale):
    acc = _dot(a_ref[...], w_ref[...])
    if scale is not None:
        acc = acc * scale
    for o_ref in o_refs:
        o_ref[...] = acc.astype(o_ref.dtype)


def _matmul(a, w, layer, out_dtypes=(F32,), scale=None, tm_pref=1024, tn_pref=512):
    m, k = a.shape
    n = w.shape[2]
    tm = _tile(m, tm_pref)
    tn = _col_tile(n, tn_pref)
    o_spec = pl.BlockSpec((tm, tn), lambda i, j: (i, j))
    res = pl.pallas_call(
        functools.partial(_matmul_kernel, scale=scale),
        grid=(m // tm, n // tn),
        in_specs=[pl.BlockSpec((tm, k), lambda i, j: (i, 0)),
                  pl.BlockSpec((None, k, tn), lambda i, j: (layer, 0, j))],
        out_specs=[o_spec] * len(out_dtypes),
        out_shape=[jax.ShapeDtypeStruct((m, n), dt) for dt in out_dtypes],
        compiler_params=_params("parallel", "parallel"),
        name="proj_matmul",
    )(a, w)
    return res[0] if len(out_dtypes) == 1 else res


def _kv_proj_kernel(a_ref, w_ref, o_ref, obf_ref):
    acc = _dot(a_ref[...], w_ref[...])
    obf_ref[...] = acc.astype(obf_ref.dtype)
    for h in range(FOX_HEADS):
        o_ref[:, h, :] = acc[:, h * FOX_HD:(h + 1) * FOX_HD]


def _kv_proj(a, w, layer):
    m, k = a.shape
    tm = _tile(m, 512)
    return pl.pallas_call(
        _kv_proj_kernel,
        grid=(m // tm,),
        in_specs=[pl.BlockSpec((tm, k), lambda i: (i, 0)),
                  pl.BlockSpec((None, k, FOX_W), lambda i: (layer, 0, 0))],
        out_specs=[pl.BlockSpec((tm, FOX_HEADS, FOX_HD), lambda i: (i, 0, 0)),
                   pl.BlockSpec((tm, FOX_W), lambda i: (i, 0))],
        out_shape=[jax.ShapeDtypeStruct((m, FOX_HEADS, FOX_HD), F32),
                   jax.ShapeDtypeStruct((m, FOX_W), BF16)],
        compiler_params=_params("parallel"),
        name="kv_proj",
    )(a, w)


def _logf_kernel(a_ref, w_ref, b_ref, o_ref):
    o_ref[...] = _log_sigmoid(_dot(a_ref[...], w_ref[...]) + b_ref[...])


def _forget_logits(h, w_ff, b_ff):
    m, k = h.shape
    tm = _tile(m, 1024)
    return pl.pallas_call(
        _logf_kernel,
        grid=(m // tm,),
        in_specs=[pl.BlockSpec((tm, k), lambda i: (i, 0)),
                  pl.BlockSpec((k, LANE), lambda i: (0, 0)),
                  pl.BlockSpec((1, LANE), lambda i: (0, 0))],
        out_specs=pl.BlockSpec((tm, LANE), lambda i: (i, 0)),
        out_shape=jax.ShapeDtypeStruct((m, LANE), F32),
        compiler_params=_params("parallel"),
        name="forget_logits",
    )(h, w_ff, b_ff)


def _cumsum_kernel(x_ref, o_ref, carry_ref, *, tb, scale):
    @pl.when(pl.program_id(0) == 0)
    def _():
        carry_ref[...] = jnp.zeros_like(carry_ref)

    row = lax.broadcasted_iota(jnp.int32, (tb, tb), 0)
    col = lax.broadcasted_iota(jnp.int32, (tb, tb), 1)
    upper = jnp.where(row <= col, 1.0, 0.0).astype(BF16)
    c = _dot_exact_rhs(x_ref[...], upper) + carry_ref[:, :1]
    o_ref[...] = c * scale
    carry_ref[...] = jnp.broadcast_to(c[:, tb - 1:tb], carry_ref.shape)


def _cumsum_lanes(x, tb, scale):
    r, t = x.shape
    return pl.pallas_call(
        functools.partial(_cumsum_kernel, tb=tb, scale=scale),
        grid=(t // tb,),
        in_specs=[pl.BlockSpec((r, tb), lambda i: (0, i))],
        out_specs=pl.BlockSpec((r, tb), lambda i: (0, i)),
        out_shape=jax.ShapeDtypeStruct((r, t), F32),
        scratch_shapes=[pltpu.VMEM((r, LANE), F32)],
        compiler_params=_params("arbitrary"),
        name="logf_cumsum",
    )(x)


def _fox_bias_kernel(c_ref, xq_ref, xk_ref):
    tr = c_ref.shape[0]
    lane = lax.broadcasted_iota(jnp.int32, (tr, LANE), 1)
    for h in range(FOX_HEADS):
        hi, mid, lo = [p.astype(F32) for p in _split3(c_ref[:, h:h + 1])]
        ones = jnp.where(lane < 6, 1.0, 0.0)
        xq = jnp.where(lane == 0, hi, jnp.where(lane == 1, mid, jnp.where(lane == 2, lo, ones)))
        xk = jnp.where(lane == 3, -hi, jnp.where(lane == 4, -mid, jnp.where(lane == 5, -lo, ones)))
        xq_ref[:, h * LANE:(h + 1) * LANE] = xq.astype(BF16)
        xk_ref[:, h * LANE:(h + 1) * LANE] = xk.astype(BF16)


def _fox_bias_columns(c_tm):
    m = c_tm.shape[0]
    tr = _tile(m, 512)
    spec = pl.BlockSpec((tr, FOX_W), lambda i: (i, 0))
    return pl.pallas_call(
        _fox_bias_kernel,
        grid=(m // tr,),
        in_specs=[pl.BlockSpec((tr, FOX_HEADS), lambda i: (i, 0))],
        out_specs=[spec, spec],
        out_shape=[jax.ShapeDtypeStruct((m, FOX_W), BF16)] * 2,
        compiler_params=_params("parallel"),
        name="fox_bias_columns",
    )(c_tm)


def _fox_prompt_kernel(qi_ref, ki_ref, q_ref, xq_ref, k_ref, xk_ref, v_ref, o_ref, m_ref, l_ref, acc_ref, *, tq):
    step = pl.program_id(1)
    qi = qi_ref[step]
    ki = ki_ref[step]

    @pl.when(ki == 0)
    def _():
        m_ref[...] = jnp.full_like(m_ref, NEG_INF)
        l_ref[...] = jnp.zeros_like(l_ref)
        acc_ref[...] = jnp.zeros_like(acc_ref)

    def update(diagonal):
        if diagonal:
            row = lax.broadcasted_iota(jnp.int32, (tq, tq), 0)
            col = lax.broadcasted_iota(jnp.int32, (tq, tq), 1)
            causal = col <= row
        heads = range(FOX_HEADS)
        cols = [slice(h * FOX_HD, (h + 1) * FOX_HD) for h in heads]
        s = [_dot(jnp.concatenate([q_ref[:, hs], xq_ref[:, hs]], axis=1),
                  jnp.concatenate([k_ref[:, hs], xk_ref[:, hs]], axis=1), _NT) for hs in cols]
        if diagonal:
            s = [jnp.where(causal, x, NEG_INF) for x in s]
        m_prev = [m_ref[h] for h in heads]
        m_new = [jnp.maximum(mp, jnp.max(x, axis=-1, keepdims=True)) for mp, x in zip(m_prev, s)]
        p = [jnp.exp2((x - jnp.concatenate([mn] * (tq // LANE), axis=1)).astype(BF16)) for x, mn in zip(s, m_new)]
        ones = jnp.ones((tq, LANE), BF16)
        pv1 = [_dot(x, jnp.concatenate([v_ref[:, hs], ones], axis=1)) for x, hs in zip(p, cols)]
        pv = [x[:, :LANE] for x in pv1]
        p_sum = [x[:, LANE:] for x in pv1]
        for h in heads:
            alpha = jnp.exp2(m_prev[h] - m_new[h])
            l_ref[h] = alpha * l_ref[h] + p_sum[h]
            acc_ref[:, cols[h]] = alpha * acc_ref[:, cols[h]] + pv[h]
            m_ref[h] = m_new[h]

    @pl.when(ki < qi)
    def _():
        update(False)

    @pl.when(ki == qi)
    def _():
        update(True)
        for h in range(FOX_HEADS):
            hs = slice(h * FOX_HD, (h + 1) * FOX_HD)
            o_ref[:, hs] = (acc_ref[:, hs] / l_ref[h]).astype(o_ref.dtype)


def _fox_prompt(q, xq, k, xk, v, batch, seq):
    tq = _tile(seq, 512)
    assert tq % LANE == 0
    nq = seq // tq
    blocks = [(qi, ki) for qi in range(nq) for ki in range(qi + 1)]
    qi_tab = jnp.asarray([b[0] for b in blocks], jnp.int32)
    ki_tab = jnp.asarray([b[1] for b in blocks], jnp.int32)
    q_map = lambda b, s, qt, kt: (b * nq + qt[s], 0)
    kv_map = lambda b, s, qt, kt: (b * nq + kt[s], 0)
    return pl.pallas_call(
        functools.partial(_fox_prompt_kernel, tq=tq),
        grid_spec=pltpu.PrefetchScalarGridSpec(
            num_scalar_prefetch=2,
            grid=(batch, len(blocks)),
            in_specs=[pl.BlockSpec((tq, FOX_W), q_map),
                      pl.BlockSpec((tq, FOX_W), q_map),
                      pl.BlockSpec((tq, FOX_W), kv_map),
                      pl.BlockSpec((tq, FOX_W), kv_map),
                      pl.BlockSpec((tq, FOX_W), kv_map)],
            out_specs=pl.BlockSpec((tq, FOX_W), q_map),
            scratch_shapes=[pltpu.VMEM((FOX_HEADS, tq, LANE), F32),
                            pltpu.VMEM((FOX_HEADS, tq, LANE), F32),
                            pltpu.VMEM((tq, FOX_W), F32)]),
        out_shape=jax.ShapeDtypeStruct((batch * seq, FOX_W), BF16),
        compiler_params=_params("parallel", "arbitrary"),
        name="fox_prompt",
    )(qi_tab, ki_tab, q, xq, k, xk, v)


def _fox_sample_kernel(q_ref, kn_ref, vn_ref, cq_ref, ck_ref, kc_ref, vc_ref, o_ref, *, past, s_new):
    kc = pltpu.einshape("thd->htd", kc_ref[0, 0])
    vc = pltpu.einshape("thd->htd", vc_ref[0, 0])
    row = lax.broadcasted_iota(jnp.int32, (s_new, s_new), 0)
    col = lax.broadcasted_iota(jnp.int32, (s_new, s_new), 1)
    causal = col <= row
    for h in range(FOX_HEADS):
        hs = slice(h * FOX_HD, (h + 1) * FOX_HD)
        q = q_ref[:, hs]
        cq = cq_ref[:, h:h + 1]
        s_c = _dot(q, kc[h].astype(BF16), _NT) + (cq - ck_ref[0, h:h + 1, :past])
        s_n = _dot(q, kn_ref[:, hs], _NT) + (cq - ck_ref[0, h:h + 1, past:past + s_new])
        s_n = jnp.where(causal, s_n, NEG_INF)
        m = jnp.maximum(jnp.max(s_c, axis=-1, keepdims=True), jnp.max(s_n, axis=-1, keepdims=True))
        p_c = jnp.exp2(s_c - m)
        p_n = jnp.exp2(s_n - m)
        l = jnp.sum(p_c, axis=-1, keepdims=True) + jnp.sum(p_n, axis=-1, keepdims=True)
        o = _dot(p_c.astype(BF16), vc[h].astype(BF16)) + _dot(p_n.astype(BF16), vn_ref[:, hs])
        o_ref[:, hs] = (o / l).astype(o_ref.dtype)


def _fox_sample(q, k, v, k_cache, v_cache, layer, c_tm, c_rows, batch, s_new, past):
    tpad = c_rows.shape[-1]
    new_spec = pl.BlockSpec((s_new, FOX_W), lambda b: (b, 0))
    cache_spec = pl.BlockSpec((1, 1, past, FOX_HEADS, FOX_HD), lambda b: (layer, b, 0, 0, 0))
    return pl.pallas_call(
        functools.partial(_fox_sample_kernel, past=past, s_new=s_new),
        grid=(batch,),
        in_specs=[new_spec, new_spec, new_spec,
                  pl.BlockSpec((s_new, FOX_HEADS), lambda b: (b, 0)),
                  pl.BlockSpec((1, FOX_HEADS, tpad), lambda b: (b, 0, 0)), cache_spec, cache_spec],
        out_specs=new_spec,
        out_shape=jax.ShapeDtypeStruct((batch * s_new, FOX_W), BF16),
        compiler_params=_params("parallel"),
        name="fox_sample",
    )(q, k, v, c_tm, c_rows, k_cache, v_cache)


def _rwkv_prep_kernel(p_ref, prev_ref, mu_ref, w0_ref, a0_ref, kk_ref, ka_ref, wup_ref, aup_ref, gup_ref,
                      r_ref, lw_ref, k_ref, v_ref, kk_out_ref, a_ref, g_ref, carry_ref, *, tr):
    @pl.when(pl.program_id(1) == 0)
    def _():
        carry_ref[...] = prev_ref[0]

    p = p_ref[0]
    rows = lax.broadcasted_iota(jnp.int32, (tr, 1), 0)
    p_prev = jnp.where(rows == 0, carry_ref[...], pltpu.roll(p, shift=1, axis=0))
    carry_ref[...] = p[tr - 1:tr, :]
    xs = p + (p_prev - p) * mu_ref[...]

    r = xs[:, :RWKV_W]
    k = xs[:, RWKV_W:2 * RWKV_W]
    v = xs[:, 2 * RWKV_W:3 * RWKV_W]
    d_wa = xs[:, RWKV_LORA_OFF:RWKV_LORA_OFF + LANE]
    d_g = xs[:, RWKV_LORA_OFF + LANE:RWKV_LORA_OFF + 2 * LANE]

    z_w = w0_ref[...] + _dot(jnp.tanh(d_wa).astype(BF16), wup_ref[...])
    log_decay = -jnp.exp(_log_sigmoid(z_w) - 0.5)
    a = _sigmoid(a0_ref[...] + _dot(d_wa.astype(BF16), aup_ref[...]))
    g = _dot(_sigmoid(d_g).astype(BF16), gup_ref[...])

    r_ref[...] = r.astype(r_ref.dtype)
    lw_ref[...] = log_decay
    k_ref[...] = (k * (1.0 + (a - 1.0) * ka_ref[...])).astype(k_ref.dtype)
    v_ref[...] = v.astype(v_ref.dtype)
    kk_out_ref[...] = (k * kk_ref[...]).astype(kk_out_ref.dtype)
    a_ref[...] = a.astype(a_ref.dtype)
    g_ref[...] = g.astype(g_ref.dtype)


def _rwkv_prep(rw, prev, lw, nseq, t):
    tr = _tile(t, 256)
    nb = t // tr
    wide = lambda s, i: (0, 0)
    row_spec = pl.BlockSpec((tr, RWKV_W), lambda s, i: (s * nb + i, 0))
    out_dtypes = (BF16, F32, BF16, BF16, BF16, BF16, BF16)
    return pl.pallas_call(
        functools.partial(_rwkv_prep_kernel, tr=tr),
        grid=(nseq, nb),
        in_specs=[pl.BlockSpec((1, tr, RWKV_IN_W), lambda s, i: (s, i, 0)),
                  pl.BlockSpec((1, 1, RWKV_IN_W), lambda s, i: (s, 0, 0)),
                  pl.BlockSpec((1, RWKV_IN_W), wide),
                  pl.BlockSpec((1, RWKV_W), wide),
                  pl.BlockSpec((1, RWKV_W), wide),
                  pl.BlockSpec((1, RWKV_W), wide),
                  pl.BlockSpec((1, RWKV_W), wide),
                  pl.BlockSpec((LANE, RWKV_W), wide),
                  pl.BlockSpec((LANE, RWKV_W), wide),
                  pl.BlockSpec((LANE, RWKV_W), wide)],
        out_specs=[row_spec] * 7,
        out_shape=[jax.ShapeDtypeStruct((nseq * t, RWKV_W), dt) for dt in out_dtypes],
        scratch_shapes=[pltpu.VMEM((1, RWKV_IN_W), F32)],
        compiler_params=_params("parallel", "arbitrary"),
        name="rwkv_prep",
    )(rw.reshape(nseq, t, RWKV_IN_W), prev, lw["mu"], lw["w0"], lw["a0"], lw["k_k"], lw["k_a"],
      lw["w_up"], lw["a_up"], lw["g_up"])


def _rwkv_chunk_kernel(*refs, c, has_s0):
    r_ref, lw_ref, k_ref, v_ref, kk_ref, lr_ref, g_ref, gng_ref, gnb_ref, rk_ref = refs[:10]
    s0_ref = refs[10] if has_s0 else None
    o_ref, sout_ref, s_ref = refs[-3:]
    ci = pl.program_id(1)
    pairs = range(RWKV_PAIRS)

    @pl.when(ci == 0)
    def _():
        if has_s0:
            zero = jnp.zeros((RWKV_HD, RWKV_HD), F32)
            for pr in pairs:
                top = jnp.concatenate([s0_ref[0, 0, 2 * pr], zero], axis=1)
                bot = jnp.concatenate([zero, s0_ref[0, 0, 2 * pr + 1]], axis=1)
                s_ref[pr] = jnp.concatenate([top, bot], axis=0)
        else:
            s_ref[...] = jnp.zeros_like(s_ref)

    c2 = 2 * c
    row = lax.broadcasted_iota(jnp.int32, (c2, c2), 0)
    col = lax.broadcasted_iota(jnp.int32, (c2, c2), 1)
    lower = col <= row
    strict = col < row
    eye = jnp.where(row == col, 1.0, 0.0)
    rc = lax.broadcasted_iota(jnp.int32, (c, c), 0)
    cc = lax.broadcasted_iota(jnp.int32, (c, c), 1)
    tri = jnp.where(cc <= rc, 1.0, 0.0).astype(BF16)
    lane = lax.broadcasted_iota(jnp.int32, (c, LANE), 1)
    first = lane < RWKV_HD
    steps = max(c.bit_length() - 2, 0)

    def stack(x):
        return jnp.concatenate([jnp.where(first, x, 0.0), jnp.where(first, 0.0, x)], axis=0)

    own = (lax.broadcasted_iota(jnp.int32, (c2, LANE), 1) // RWKV_HD
           == lax.broadcasted_iota(jnp.int32, (c2, LANE), 0) // c)

    lanes = [slice(pr * LANE, (pr + 1) * LANE) for pr in pairs]
    s_mats = [s_ref[pr] for pr in pairs]
    lws = [lw_ref[:, ls] for ls in lanes]
    cum3 = [_dot(tri, jnp.concatenate(_split3(lw), axis=1)) for lw in lws]
    cums = [x[:, :LANE] + x[:, LANE:2 * LANE] + x[:, 2 * LANE:] for x in cum3]
    totals = [cum[c - 1:c, :] for cum in cums]

    lhs, rhs, bar, v_s, rk_s = [], [], [], [], []
    for ls, lw, cum, total in zip(lanes, lws, cums, totals):
        grow = jnp.exp(-cum)
        rest = jnp.exp(total - cum)
        r = r_ref[:, ls].astype(F32)
        k = k_ref[:, ls].astype(F32)
        kk = kk_ref[:, ls].astype(F32)
        norm = lax.rsqrt(jnp.maximum(jnp.sum(stack(kk * kk), axis=-1, keepdims=True), 1e-24))
        kk = kk * jnp.where(first, norm[:c], norm[c:])
        b = kk * lr_ref[:, ls].astype(F32)
        lhs.append(jnp.concatenate([stack(-kk * jnp.exp(cum - lw)), stack(r * jnp.exp(cum))], axis=0).astype(BF16))
        rhs.append(jnp.concatenate([stack(k * grow), stack(b * grow)], axis=0).astype(BF16))
        bar.append(jnp.concatenate([stack(k * rest), stack(b * rest)], axis=0).astype(BF16))
        v_s.append(stack(v_ref[:, ls].astype(F32)))
        rk_s.append(stack(r * k * rk_ref[:, ls]))

    amats = [_dot(x, y, _NT) for x, y in zip(lhs, rhs)]
    sxs = [_dot(x, s.astype(BF16), _NT) for x, s in zip(lhs, s_mats)]
    a_ab = [jnp.where(strict, m[:c2, c2:], 0.0) for m in amats]
    invs = [eye + n for n in a_ab]
    powers = [_dot(n.astype(BF16), n.astype(BF16)) for n in a_ab]
    for i in range(steps):
        if i < steps - 1:
            xs = [_dot(jnp.concatenate([inv, pw], axis=0).astype(BF16), pw.astype(BF16))
                  for inv, pw in zip(invs, powers)]
            invs = [inv + x[:c2] for inv, x in zip(invs, xs)]
            powers = [x[c2:] for x in xs]
        else:
            invs = [inv + _dot(inv.astype(BF16), pw.astype(BF16)) for inv, pw in zip(invs, powers)]

    akv = [_dot(jnp.where(strict, m[:c2, :c2], 0.0).astype(BF16), v.astype(BF16)) for m, v in zip(amats, v_s)]
    us = [_dot(inv.astype(BF16), (sx[:c2] + x).astype(BF16)) for inv, sx, x in zip(invs, sxs, akv)]
    vus = [jnp.concatenate([v, u], axis=0).astype(BF16) for v, u in zip(v_s, us)]
    for pr in pairs:
        m = amats[pr]
        a_r = jnp.concatenate([jnp.where(lower, m[c2:, :c2], 0.0), jnp.where(lower, m[c2:, c2:], 0.0)], axis=1)
        y_s = sxs[pr][c2:] + _dot(a_r.astype(BF16), vus[pr])
        s_ref[pr] = s_mats[pr] * jnp.exp(totals[pr]) + _dot(vus[pr], bar[pr], _TN)
        ls = lanes[pr]
        mu = jnp.sum(y_s, axis=-1, keepdims=True) * (1.0 / RWKV_HD)
        d = jnp.where(own, y_s - mu, 0.0)
        var = jnp.sum(d * d, axis=-1, keepdims=True) * (1.0 / RWKV_HD)
        yn_s = d * lax.rsqrt(var + RWKV_GN_EPS)
        bonus_s = jnp.sum(rk_s[pr], axis=-1, keepdims=True) * v_s[pr]
        yn = (yn_s[:c] + yn_s[c:]) * gng_ref[:, ls] + gnb_ref[:, ls]
        o_ref[:, ls] = ((yn + (bonus_s[:c] + bonus_s[c:])) * g_ref[:, ls].astype(F32)).astype(o_ref.dtype)

    @pl.when(ci == pl.num_programs(1) - 1)
    def _():
        for pr in pairs:
            s_mat = s_ref[pr]
            sout_ref[0, 2 * pr] = s_mat[:RWKV_HD, :RWKV_HD]
            sout_ref[0, 2 * pr + 1] = s_mat[RWKV_HD:, RWKV_HD:]


def _rwkv_chunks(r, lwd, k, v, kk, lr, g, lw, state, layer, nseq, t, c):
    nc = t // c
    row_spec = pl.BlockSpec((c, RWKV_W), lambda s, i: (s * nc + i, 0))
    par_spec = pl.BlockSpec((1, RWKV_W), lambda s, i: (0, 0))
    in_specs = [row_spec] * 7 + [par_spec] * 3
    args = [r, lwd, k, v, kk, lr, g, lw["gn_g"], lw["gn_b"], lw["r_k"]]
    if state is not None:
        in_specs.append(pl.BlockSpec((1, 1, RWKV_HEADS, RWKV_HD, RWKV_HD), lambda s, i: (layer, s, 0, 0, 0)))
        args.append(state)
    return pl.pallas_call(
        functools.partial(_rwkv_chunk_kernel, c=c, has_s0=state is not None),
        grid=(nseq, nc),
        in_specs=in_specs,
        out_specs=[row_spec, pl.BlockSpec((1, RWKV_HEADS, RWKV_HD, RWKV_HD), lambda s, i: (s, 0, 0, 0))],
        out_shape=[jax.ShapeDtypeStruct((nseq * t, RWKV_W), BF16),
                   jax.ShapeDtypeStruct((nseq, RWKV_HEADS, RWKV_HD, RWKV_HD), F32)],
        scratch_shapes=[pltpu.VMEM((RWKV_PAIRS, LANE, LANE), F32)],
        compiler_params=_params("parallel", "arbitrary"),
        name="rwkv_chunks",
    )(*args)


def _gmlp_kernel(gugv_ref, lng_ref, lnb_ref, ws_ref, bs_ref, o_ref, *vn_out):
    for blk in range(o_ref.shape[0] // GMLP_CHUNK):
        rows = slice(blk * GMLP_CHUNK, (blk + 1) * GMLP_CHUNK)
        u = _gelu_tanh(gugv_ref[rows, :GMLP_W])
        gv = _gelu_tanh(gugv_ref[rows, GMLP_W:])
        mu = jnp.mean(gv, axis=-1, keepdims=True)
        d = gv - mu
        var = jnp.mean(d * d, axis=-1, keepdims=True)
        vn = d * lax.rsqrt(var + LN_EPS) * lng_ref[...] + lnb_ref[...]
        if vn_out:
            vn_out[0][rows, :] = vn
        vb = vn.astype(BF16)
        for g in range(GMLP_GROUPS):
            gs = slice(g * GMLP_GC, (g + 1) * GMLP_GC)
            s = _dot(ws_ref[g], vb[:, gs]) + bs_ref[:, g:g + 1]
            o_ref[rows, gs] = (u[:, gs] * s).astype(o_ref.dtype)


def _gmlp(gugv, lw, w_eff, b_eff, emit_vn):
    m = gugv.shape[0]
    tr = _tile(m, 4 * GMLP_CHUNK)
    assert tr % GMLP_CHUNK == 0
    row_spec = pl.BlockSpec((tr, GMLP_W), lambda i: (i, 0))
    out_shape = [jax.ShapeDtypeStruct((m, GMLP_W), BF16)]
    out_specs = [row_spec]
    if emit_vn:
        out_shape.append(jax.ShapeDtypeStruct((m, GMLP_W), F32))
        out_specs.append(row_spec)
    res = pl.pallas_call(
        _gmlp_kernel,
        grid=(m // tr,),
        in_specs=[pl.BlockSpec((tr, 2 * GMLP_W), lambda i: (i, 0)),
                  pl.BlockSpec((1, GMLP_W), lambda i: (0, 0)),
                  pl.BlockSpec((1, GMLP_W), lambda i: (0, 0)),
                  pl.BlockSpec((GMLP_GROUPS, GMLP_CHUNK, GMLP_CHUNK), lambda i: (0, 0, 0)),
                  pl.BlockSpec((GMLP_CHUNK, GMLP_GROUPS), lambda i: (0, 0))],
        out_specs=out_specs,
        out_shape=out_shape,
        compiler_params=_params("parallel"),
        name="gmlp_sgu",
    )(gugv, lw["ln_g"], lw["ln_b"], w_eff, b_eff)
    return res if emit_vn else (res[0], None)


def _merge_kernel(oa_ref, ob_ref, oc_ref, ga_ref, gb_ref, gc_ref, wa_ref, wb_ref, wc_ref, o_ref):
    m = _sigmoid(ga_ref[...].astype(F32)) * _dot(oa_ref[...], wa_ref[...])
    m = m + _sigmoid(gb_ref[...].astype(F32)) * _dot(ob_ref[...], wb_ref[...])
    m = m + _sigmoid(gc_ref[...].astype(F32)) * _dot(oc_ref[...], wc_ref[...])
    o_ref[...] = m.astype(o_ref.dtype)


def _merge(o_a, o_b, o_c, gates, w_a, w_b, w_c, layer):
    m, kw = o_a.shape
    d = w_a.shape[2]
    tm = _tile(m, 512)
    tn = _col_tile(d, 1024)
    nj = d // tn
    o_spec = pl.BlockSpec((tm, kw), lambda j, i: (i, 0))
    w_spec = pl.BlockSpec((None, kw, tn), lambda j, i: (layer, 0, j))
    return pl.pallas_call(
        _merge_kernel,
        grid=(nj, m // tm),
        in_specs=[o_spec, o_spec, o_spec,
                  pl.BlockSpec((tm, tn), lambda j, i: (i, j)),
                  pl.BlockSpec((tm, tn), lambda j, i: (i, nj + j)),
                  pl.BlockSpec((tm, tn), lambda j, i: (i, 2 * nj + j)),
                  w_spec, w_spec, w_spec],
        out_specs=pl.BlockSpec((tm, tn), lambda j, i: (i, j)),
        out_shape=jax.ShapeDtypeStruct((m, d), BF16),
        compiler_params=_params("parallel", "parallel"),
        name="branch_merge",
    )(o_a, o_b, o_c, gates, gates, gates, w_a, w_b, w_c)


def _out_proj_kernel(m_ref, w_ref, x_ref, g_ref, o_ref):
    o_ref[...] = x_ref[...] + _rms(_dot(m_ref[...], w_ref[...]), g_ref[...])


def _out_proj(mm, w_out, layer, x, g):
    m, d = x.shape
    tm = _tile(m, 512)
    return pl.pallas_call(
        _out_proj_kernel,
        grid=(m // tm,),
        in_specs=[pl.BlockSpec((tm, d), lambda i: (i, 0)),
                  pl.BlockSpec((None, d, d), lambda i: (layer, 0, 0)),
                  pl.BlockSpec((tm, d), lambda i: (i, 0)),
                  pl.BlockSpec((1, d), lambda i: (0, 0))],
        out_specs=pl.BlockSpec((tm, d), lambda i: (i, 0)),
        out_shape=jax.ShapeDtypeStruct((m, d), F32),
        compiler_params=_params("parallel"),
        name="out_proj_residual",
    )(mm, w_out, x, g.reshape(1, d))


def _ffn_kernel(x_ref, gpre_ref, gpost_ref, gnext_ref, w1_ref, w3_ref, w2_ref, o_ref, hn_ref, h_ref, acc_ref):
    j = pl.program_id(1)

    @pl.when(j == 0)
    def _():
        h_ref[...] = _rms(x_ref[...], gpre_ref[...]).astype(BF16)
        acc_ref[...] = jnp.zeros_like(acc_ref)

    h = h_ref[...]
    a = _dot(h, w1_ref[...])
    b = _dot(h, w3_ref[...])
    gated = (a * _sigmoid(a)) * b
    acc_ref[...] += _dot(gated.astype(BF16), w2_ref[...])

    @pl.when(j == pl.num_programs(1) - 1)
    def _():
        x_new = x_ref[...] + _rms(acc_ref[...], gpost_ref[...])
        o_ref[...] = x_new
        hn_ref[...] = _rms(x_new, gnext_ref[...]).astype(hn_ref.dtype)


def _ffn(x, g_pre, g_post, g_next, w1, w3, w2, layer):
    m, d = x.shape
    dff = w1.shape[2]
    tm = _tile(m, 512)
    tf = _col_tile(dff, 512)
    vec = pl.BlockSpec((1, d), lambda i, j: (0, 0))
    row = pl.BlockSpec((tm, d), lambda i, j: (i, 0))
    return pl.pallas_call(
        _ffn_kernel,
        grid=(m // tm, dff // tf),
        in_specs=[row, vec, vec, vec,
                  pl.BlockSpec((None, d, tf), lambda i, j: (layer, 0, j)),
                  pl.BlockSpec((None, d, tf), lambda i, j: (layer, 0, j)),
                  pl.BlockSpec((None, tf, d), lambda i, j: (layer, j, 0))],
        out_specs=[row, row],
        out_shape=[jax.ShapeDtypeStruct((m, d), F32), jax.ShapeDtypeStruct((m, d), BF16)],
        scratch_shapes=[pltpu.VMEM((tm, d), BF16), pltpu.VMEM((tm, d), F32)],
        compiler_params=_params("parallel", "arbitrary"),
        name="ffn_swiglu",
    )(x, g_pre.reshape(1, d), g_post.reshape(1, d), g_next.reshape(1, d), w1, w3, w2)


def _pad_rows(w, rows, offset):
    out = jnp.zeros((rows, w.shape[1]), w.dtype)
    return out.at[offset:offset + w.shape[0]].set(w)


_O_FF = 3 * FOX_W
_O_RWKV = _O_FF + FOX_HEADS
_O_GMLP = _O_RWKV + RWKV_IN_W
_O_GATE = _O_GMLP + 2 * GMLP_W


def _stacked_weights(w_in, w_br_fox, w_br_rwkv, w_br_gmlp, w_out, ffn_w1, ffn_w3, ffn_w2):
    return {
        "wq": w_in[:, :, :FOX_W].astype(BF16),
        "wk": w_in[:, :, FOX_W:2 * FOX_W].astype(BF16),
        "wv": w_in[:, :, 2 * FOX_W:_O_FF].astype(BF16),
        "w_rwkv": w_in[:, :, _O_RWKV:_O_GMLP].astype(BF16),
        "w_gmlp": w_in[:, :, _O_GMLP:_O_GATE].astype(BF16),
        "w_gate": w_in[:, :, _O_GATE:].astype(BF16),
        "w_br_fox": w_br_fox.astype(BF16),
        "w_br_rwkv": w_br_rwkv.astype(BF16),
        "w_br_gmlp": w_br_gmlp.astype(BF16),
        "w_out": w_out.astype(BF16),
        "ffn_w1": ffn_w1.astype(BF16),
        "ffn_w3": ffn_w3.astype(BF16),
        "ffn_w2": ffn_w2.astype(BF16),
    }


def _layer_weights(l, d, w_in, fox_bf, rwkv_mu, rwkv_w0, rwkv_w_up, rwkv_a0, rwkv_a_up, rwkv_g_up, rwkv_k_k,
                   rwkv_k_a, rwkv_r_k, rwkv_gn_g, rwkv_gn_b, gmlp_ln_g, gmlp_ln_b, gmlp_w_s, gmlp_b_s, s_new):
    tril_full = jnp.tril(gmlp_w_s[l])
    blk = jnp.tril(gmlp_w_s[l][:, :s_new, :s_new])
    reps = GMLP_CHUNK // s_new
    eye = jnp.eye(reps, dtype=F32)
    w_sample = jnp.einsum("ab,gij->gaibj", eye, blk).reshape(GMLP_GROUPS, GMLP_CHUNK, GMLP_CHUNK)
    return {
        "w_ff": jnp.zeros((d, LANE), BF16).at[:, :FOX_HEADS].set(w_in[l, :, _O_FF:_O_RWKV].astype(BF16)),
        "b_ff": jnp.zeros((1, LANE), F32).at[0, :FOX_HEADS].set(fox_bf[l]),
        "rwkv": {
            "mu": rwkv_mu[l].reshape(1, RWKV_IN_W),
            "w0": rwkv_w0[l].reshape(1, RWKV_W),
            "a0": rwkv_a0[l].reshape(1, RWKV_W),
            "k_k": rwkv_k_k[l].reshape(1, RWKV_W),
            "k_a": rwkv_k_a[l].reshape(1, RWKV_W),
            "r_k": rwkv_r_k[l].reshape(1, RWKV_W),
            "gn_g": rwkv_gn_g[l].reshape(1, RWKV_W),
            "gn_b": rwkv_gn_b[l].reshape(1, RWKV_W),
            "w_up": _pad_rows(rwkv_w_up[l], LANE, 0).astype(BF16),
            "a_up": _pad_rows(rwkv_a_up[l], LANE, rwkv_w_up.shape[1]).astype(BF16),
            "g_up": rwkv_g_up[l].astype(BF16),
        },
        "gmlp": {"ln_g": gmlp_ln_g[l].reshape(1, GMLP_W), "ln_b": gmlp_ln_b[l].reshape(1, GMLP_W)},
        "gmlp_w_prompt": tril_full.astype(BF16),
        "gmlp_b_prompt": gmlp_b_s[l].T,
        "gmlp_w_sample": w_sample.astype(BF16),
        "gmlp_b_sample": jnp.tile(gmlp_b_s[l][:, :s_new].T, (reps, 1)),
    }


def _trunk_layer(x, h, big, lw, post_mix_g, pre_ffn_g, post_ffn_g, next_mix_g, nseq, t, layer, cache):
    m, d = x.shape
    q = _matmul(h, big["wq"], layer, (BF16,), scale=FOX_Q_SCALE, tn_pref=1024)
    k, k_bf = _kv_proj(h, big["wk"], layer)
    v, v_bf = _kv_proj(h, big["wv"], layer)
    logf = _forget_logits(h, lw["w_ff"], lw["b_ff"])[:, :FOX_HEADS]
    rw = _matmul(h, big["w_rwkv"], layer, tn_pref=1664)
    gugv = _matmul(h, big["w_gmlp"], layer, tn_pref=1024)
    gates = _matmul(h, big["w_gate"], layer, (BF16,), tn_pref=1536)

    logf_rows = logf.reshape(nseq, t, FOX_HEADS).transpose(0, 2, 1)
    if cache is None:
        c_rows = _cumsum_lanes(logf_rows.reshape(nseq * FOX_HEADS, t), _tile(t, 256), LOG2E)
        c_tm = c_rows.reshape(nseq, FOX_HEADS, t).transpose(0, 2, 1).reshape(m, FOX_HEADS)
        xq, xk = _fox_bias_columns(c_tm)
        o_a = _fox_prompt(q, xq, k_bf, xk, v_bf, nseq, t)
        prev = jnp.zeros((nseq, 1, RWKV_IN_W), F32)
        state = None
        chunk = 64
        gm_w, gm_b = lw["gmlp_w_prompt"], lw["gmlp_b_prompt"]
    else:
        past = cache["k"].shape[2]
        total = past + t
        tpad = -(-total // LANE) * LANE
        lf_all = jnp.concatenate([cache["logf"][layer].astype(F32).transpose(0, 2, 1), logf_rows], axis=-1)
        lf_all = jnp.pad(lf_all, ((0, 0), (0, 0), (0, tpad - total)))
        c_rows = _cumsum_lanes(lf_all.reshape(nseq * FOX_HEADS, tpad), LANE, LOG2E).reshape(nseq, FOX_HEADS, tpad)
        c_tm = c_rows[:, :, past:total].transpose(0, 2, 1).reshape(m, FOX_HEADS)
        o_a = _fox_sample(q, k_bf, v_bf, cache["k"], cache["v"], layer, c_tm, c_rows, nseq, t, past)
        prev = cache["shift"][layer]
        state = cache["state"]
        chunk = t
        gm_w, gm_b = lw["gmlp_w_sample"], lw["gmlp_b_sample"]

    r, lwd, k2, v2, kk, lr, g = _rwkv_prep(rw, prev, lw["rwkv"], nseq, t)
    o_b, s_new = _rwkv_chunks(r, lwd, k2, v2, kk, lr, g, lw["rwkv"], state, layer, nseq, t, chunk)
    shift_new = rw.reshape(nseq, t, RWKV_IN_W)[:, t - 1:t, :]

    o_c, vn = _gmlp(gugv, lw["gmlp"], gm_w, gm_b, cache is not None)

    mm = _merge(o_a, o_b, o_c, gates, big["w_br_fox"], big["w_br_rwkv"], big["w_br_gmlp"], layer)
    x = _out_proj(mm, big["w_out"], layer, x, post_mix_g)
    x, h_next = _ffn(x, pre_ffn_g, post_ffn_g, next_mix_g, big["ffn_w1"], big["ffn_w3"], big["ffn_w2"], layer)
    return x, h_next, (k, v, logf, s_new, shift_new, vn)


def kernel(x_prompt, x_sample, cache_fox_k, cache_fox_v, cache_fox_logf, state_rwkv, state_rwkv_shift, pre_mix_g, w_in, fox_bf, rwkv_mu, rwkv_w0, rwkv_w_up, rwkv_a0, rwkv_a_up, rwkv_g_up, rwkv_k_k, rwkv_k_a, rwkv_r_k, rwkv_gn_g, rwkv_gn_b, gmlp_ln_g, gmlp_ln_b, gmlp_w_s, gmlp_b_s, w_br_fox, w_br_rwkv, w_br_gmlp, w_out, post_mix_g, pre_ffn_g, ffn_w1, ffn_w3, ffn_w2, post_ffn_g):
    batch, seq, d = x_prompt.shape
    dec_batch, dec_seq, _ = x_sample.shape
    depth = w_in.shape[0]
    assert rwkv_w_up.shape[1] + rwkv_a_up.shape[1] == LANE and rwkv_g_up.shape[1] == LANE
    assert seq % GMLP_CHUNK == 0 and GMLP_CHUNK % dec_seq == 0 and (dec_batch * dec_seq) % GMLP_CHUNK == 0

    xp = x_prompt.reshape(batch * seq, d)
    xs = x_sample.reshape(dec_batch * dec_seq, d)
    cache = {"k": cache_fox_k, "v": cache_fox_v, "logf": cache_fox_logf,
             "state": state_rwkv, "shift": state_rwkv_shift}
    big = _stacked_weights(w_in, w_br_fox, w_br_rwkv, w_br_gmlp, w_out, ffn_w1, ffn_w3, ffn_w2)
    outs_p = [[] for _ in range(5)]
    outs_s = [[] for _ in range(6)]
    hp = _rmsnorm_cast(xp, pre_mix_g[0])
    hs = _rmsnorm_cast(xs, pre_mix_g[0])
    for l in range(depth):
        lw = _layer_weights(l, d, w_in, fox_bf, rwkv_mu, rwkv_w0, rwkv_w_up, rwkv_a0, rwkv_a_up, rwkv_g_up,
                            rwkv_k_k, rwkv_k_a, rwkv_r_k, rwkv_gn_g, rwkv_gn_b, gmlp_ln_g, gmlp_ln_b, gmlp_w_s,
                            gmlp_b_s, dec_seq)
        norms = (post_mix_g[l], pre_ffn_g[l], post_ffn_g[l], pre_mix_g[(l + 1) % depth])
        xp, hp, (k_p, v_p, lf_p, s_p, sh_p, _) = _trunk_layer(xp, hp, big, lw, *norms, batch, seq, l, None)
        xs, hs, (k_s, v_s, lf_s, s_s, sh_s, vn_s) = _trunk_layer(xs, hs, big, lw, *norms, dec_batch, dec_seq, l, cache)
        for acc, val in zip(outs_p, (k_p.reshape(batch, seq, FOX_HEADS, FOX_HD), v_p.reshape(batch, seq, FOX_HEADS, FOX_HD),
                                     lf_p.reshape(batch, seq, FOX_HEADS), s_p, sh_p)):
            acc.append(val)
        for acc, val in zip(outs_s, (k_s.reshape(dec_batch, dec_seq, FOX_HEADS, FOX_HD),
                                     v_s.reshape(dec_batch, dec_seq, FOX_HEADS, FOX_HD),
                                     lf_s.reshape(dec_batch, dec_seq, FOX_HEADS), s_s, sh_s,
                                     vn_s.reshape(dec_batch, dec_seq, GMLP_W))):
            acc.append(val)
    return (xp.reshape(batch, seq, d), xs.reshape(dec_batch, dec_seq, d),
            *[jnp.stack(o) for o in outs_p], *[jnp.stack(o) for o in outs_s])
```

```python
import functools
import math

import jax
import jax.numpy as jnp
from jax import lax
from jax.experimental import pallas as pl
from jax.experimental.pallas import tpu as pltpu

F32 = jnp.float32
BF16 = jnp.bfloat16

RMS_EPS = 1e-6
LN_EPS = 1e-5
RWKV_GN_EPS = 64e-5
NEG_INF = -1e30
LOG2E = math.log2(math.e)

LANE = 128
FOX_HEADS = 8
FOX_HD = 128
FOX_W = FOX_HEADS * FOX_HD
FOX_Q_SCALE = FOX_HD ** -0.5 * LOG2E
RWKV_HEADS = 16
RWKV_HD = 64
RWKV_W = RWKV_HEADS * RWKV_HD
RWKV_PAIRS = RWKV_W // LANE
RWKV_LORA_OFF = 3 * RWKV_W
RWKV_IN_W = 3 * RWKV_W + 2 * LANE
GMLP_W = 1024
GMLP_GROUPS = 8
GMLP_GC = GMLP_W // GMLP_GROUPS
GMLP_CHUNK = 128
N_BRANCH = 3

VMEM_LIMIT = 48 * 1024 * 1024


def _params(*sem):
    return pltpu.CompilerParams(dimension_semantics=sem, vmem_limit_bytes=VMEM_LIMIT)


def _tile(n, pref):
    if n <= pref:
        return n
    t = pref
    while n % t:
        t -= 8
    return t


def _col_tile(n, pref):
    assert n % LANE == 0
    k = n // LANE
    best = 1
    for d in range(1, k + 1):
        if k % d == 0 and d * LANE <= pref:
            best = d
    return best * LANE


def _split3(x):
    hi = x.astype(BF16)
    r = x - hi.astype(F32)
    mid = r.astype(BF16)
    lo = (r - mid.astype(F32)).astype(BF16)
    return hi, mid, lo


_NN = (((1,), (0,)), ((), ()))
_NT = (((1,), (1,)), ((), ()))
_TN = (((0,), (0,)), ((), ()))


def _dot(a, b, dims=_NN):
    return lax.dot_general(a, b, dims, preferred_element_type=F32)


def _dot_exact_rhs(x, e):
    hi, mid, lo = _split3(x)
    return _dot(hi, e) + _dot(mid, e) + _dot(lo, e)


def _sigmoid(x):
    return 1.0 / (1.0 + jnp.exp(-x))


def _log_sigmoid(x):
    return jnp.minimum(x, 0.0) - jnp.log(1.0 + jnp.exp(-jnp.abs(x)))


def _gelu_tanh(x):
    return 0.5 * x * (1.0 + jnp.tanh(0.7978845608028654 * (x + 0.044715 * (x * x * x))))


def _rms(x, g):
    ms = jnp.mean(x * x, axis=-1, keepdims=True)
    return x * lax.rsqrt(ms + RMS_EPS) * g


def _rmsnorm_cast_kernel(x_ref, g_ref, o_ref):
    o_ref[...] = _rms(x_ref[...], g_ref[...]).astype(o_ref.dtype)


def _rmsnorm_cast(x, g):
    m, d = x.shape
    tm = _tile(m, 512)
    return pl.pallas_call(
        _rmsnorm_cast_kernel,
        grid=(m // tm,),
        in_specs=[pl.BlockSpec((tm, d), lambda i: (i, 0)),
                  pl.BlockSpec((1, d), lambda i: (0, 0))],
        out_specs=pl.BlockSpec((tm, d), lambda i: (i, 0)),
        out_shape=jax.ShapeDtypeStruct((m, d), BF16),
        compiler_params=_params("parallel"),
        name="rmsnorm_cast",
    )(x, g.reshape(1, d))


def _matmul_kernel(a_ref, w_ref, *o_refs, scale):
    acc = _dot(a_ref[...], w_ref[...])
    if scale is not None:
        acc = acc * scale
    for o_ref in o_refs:
        o_ref[...] = acc.astype(o_ref.dtype)


def _matmul(a, w, layer, out_dtypes=(F32,), scale=None, tm_pref=1024, tn_pref=512):
    m, k = a.shape
    n = w.shape[2]
    tm = _tile(m, tm_pref)
    tn = _col_tile(n, tn_pref)
    o_spec = pl.BlockSpec((tm, tn), lambda i, j: (i, j))
    res = pl.pallas_call(
        functools.partial(_matmul_kernel, scale=scale),
        grid=(m // tm, n // tn),
        in_specs=[pl.BlockSpec((tm, k), lambda i, j: (i, 0)),
                  pl.BlockSpec((None, k, tn), lambda i, j: (layer, 0, j))],
        out_specs=[o_spec] * len(out_dtypes),
        out_shape=[jax.ShapeDtypeStruct((m, n), dt) for dt in out_dtypes],
        compiler_params=_params("parallel", "parallel"),
        name="proj_matmul",
    )(a, w)
    return res[0] if len(out_dtypes) == 1 else res


def _kv_proj_kernel(a_ref, w_ref, o_ref, obf_ref):
    acc = _dot(a_ref[...], w_ref[...])
    obf_ref[...] = acc.astype(obf_ref.dtype)
    for h in range(FOX_HEADS):
        o_ref[:, h, :] = acc[:, h * FOX_HD:(h + 1) * FOX_HD]


def _kv_proj(a, w, layer):
    m, k = a.shape
    tm = _tile(m, 512)
    return pl.pallas_call(
        _kv_proj_kernel,
        grid=(m // tm,),
        in_specs=[pl.BlockSpec((tm, k), lambda i: (i, 0)),
                  pl.BlockSpec((None, k, FOX_W), lambda i: (layer, 0, 0))],
        out_specs=[pl.BlockSpec((tm, FOX_HEADS, FOX_HD), lambda i: (i, 0, 0)),
                   pl.BlockSpec((tm, FOX_W), lambda i: (i, 0))],
        out_shape=[jax.ShapeDtypeStruct((m, FOX_HEADS, FOX_HD), F32),
                   jax.ShapeDtypeStruct((m, FOX_W), BF16)],
        compiler_params=_params("parallel"),
        name="kv_proj",
    )(a, w)


def _logf_kernel(a_ref, w_ref, b_ref, o_ref):
    o_ref[...] = _log_sigmoid(_dot(a_ref[...], w_ref[...]) + b_ref[...])


def _forget_logits(h, w_ff, b_ff):
    m, k = h.shape
    tm = _tile(m, 1024)
    return pl.pallas_call(
        _logf_kernel,
        grid=(m // tm,),
        in_specs=[pl.BlockSpec((tm, k), lambda i: (i, 0)),
                  pl.BlockSpec((k, LANE), lambda i: (0, 0)),
                  pl.BlockSpec((1, LANE), lambda i: (0, 0))],
        out_specs=pl.BlockSpec((tm, LANE), lambda i: (i, 0)),
        out_shape=jax.ShapeDtypeStruct((m, LANE), F32),
        compiler_params=_params("parallel"),
        name="forget_logits",
    )(h, w_ff, b_ff)


def _cumsum_kernel(x_ref, o_ref, carry_ref, *, tb, scale):
    @pl.when(pl.program_id(0) == 0)
    def _():
        carry_ref[...] = jnp.zeros_like(carry_ref)

    row = lax.broadcasted_iota(jnp.int32, (tb, tb), 0)
    col = lax.broadcasted_iota(jnp.int32, (tb, tb), 1)
    upper = jnp.where(row <= col, 1.0, 0.0).astype(BF16)
    c = _dot_exact_rhs(x_ref[...], upper) + carry_ref[:, :1]
    o_ref[...] = c * scale
    carry_ref[...] = jnp.broadcast_to(c[:, tb - 1:tb], carry_ref.shape)


def _cumsum_lanes(x, tb, scale):
    r, t = x.shape
    return pl.pallas_call(
        functools.partial(_cumsum_kernel, tb=tb, scale=scale),
        grid=(t // tb,),
        in_specs=[pl.BlockSpec((r, tb), lambda i: (0, i))],
        out_specs=pl.BlockSpec((r, tb), lambda i: (0, i)),
        out_shape=jax.ShapeDtypeStruct((r, t), F32),
        scratch_shapes=[pltpu.VMEM((r, LANE), F32)],
        compiler_params=_params("arbitrary"),
        name="logf_cumsum",
    )(x)


def _fox_bias_kernel(c_ref, xq_ref, xk_ref):
    tr = c_ref.shape[0]
    lane = lax.broadcasted_iota(jnp.int32, (tr, LANE), 1)
    for h in range(FOX_HEADS):
        hi, mid, lo = [p.astype(F32) for p in _split3(c_ref[:, h:h + 1])]
        ones = jnp.where(lane < 6, 1.0, 0.0)
        xq = jnp.where(lane == 0, hi, jnp.where(lane == 1, mid, jnp.where(lane == 2, lo, ones)))
        xk = jnp.where(lane == 3, -hi, jnp.where(lane == 4, -mid, jnp.where(lane == 5, -lo, ones)))
        xq_ref[:, h * LANE:(h + 1) * LANE] = xq.astype(BF16)
        xk_ref[:, h * LANE:(h + 1) * LANE] = xk.astype(BF16)


def _fox_bias_columns(c_tm):
    m = c_tm.shape[0]
    tr = _tile(m, 512)
    spec = pl.BlockSpec((tr, FOX_W), lambda i: (i, 0))
    return pl.pallas_call(
        _fox_bias_kernel,
        grid=(m // tr,),
        in_specs=[pl.BlockSpec((tr, FOX_HEADS), lambda i: (i, 0))],
        out_specs=[spec, spec],
        out_shape=[jax.ShapeDtypeStruct((m, FOX_W), BF16)] * 2,
        compiler_params=_params("parallel"),
        name="fox_bias_columns",
    )(c_tm)


FOX_KEYS_FULL, FOX_KEYS_DIAG_HALF, FOX_KEYS_FULL_DIAG = 0, 1, 2
FOX_HEAD_GROUP = 4


def _fox_prompt_kernel(qi_ref, kj_ref, kind_ref, q_ref, xq_ref, k_ref, xk_ref, v_ref, o_ref, m_ref, l_ref, acc_ref,
                       *, tq):
    step = pl.program_id(1)
    kj = kj_ref[step]
    kind = kind_ref[step]

    @pl.when(kj == 0)
    def _():
        m_ref[...] = jnp.full_like(m_ref, NEG_INF)
        l_ref[...] = jnp.zeros_like(l_ref)
        acc_ref[...] = jnp.zeros_like(acc_ref)

    def update(keys_kind):
        nk = tq if keys_kind == FOX_KEYS_DIAG_HALF else 2 * tq
        keys = slice(0, nk)
        if keys_kind != FOX_KEYS_FULL:
            row = lax.broadcasted_iota(jnp.int32, (tq, nk), 0)
            col = lax.broadcasted_iota(jnp.int32, (tq, nk), 1)
            causal = col <= row + (nk - tq)
        ones = jnp.ones((nk, LANE), BF16)
        for g0 in range(0, FOX_HEADS, FOX_HEAD_GROUP):
            heads = range(g0, g0 + FOX_HEAD_GROUP)
            cols = [slice(h * FOX_HD, (h + 1) * FOX_HD) for h in heads]
            s = [_dot(jnp.concatenate([q_ref[:, hs], xq_ref[:, hs]], axis=1),
                      jnp.concatenate([k_ref[keys, hs], xk_ref[keys, hs]], axis=1), _NT) for hs in cols]
            if keys_kind != FOX_KEYS_FULL:
                s = [jnp.where(causal, x, NEG_INF) for x in s]
            m_prev = [m_ref[h] for h in heads]
            m_new = [jnp.maximum(mp, jnp.max(x, axis=-1, keepdims=True)) for mp, x in zip(m_prev, s)]
            p = [jnp.exp2((x - jnp.concatenate([mn] * (nk // LANE), axis=1)).astype(BF16)) for x, mn in zip(s, m_new)]
            pv1 = [_dot(x, jnp.concatenate([v_ref[keys, hs], ones], axis=1)) for x, hs in zip(p, cols)]
            for i, h in enumerate(heads):
                alpha = jnp.exp2(m_prev[i] - m_new[i])
                l_ref[h] = alpha * l_ref[h] + pv1[i][:, LANE:]
                acc_ref[:, cols[i]] = alpha * acc_ref[:, cols[i]] + pv1[i][:, :LANE]
                m_ref[h] = m_new[i]

    def finish():
        for h in range(FOX_HEADS):
            hs = slice(h * FOX_HD, (h + 1) * FOX_HD)
            o_ref[:, hs] = (acc_ref[:, hs] / l_ref[h]).astype(o_ref.dtype)

    @pl.when(kind == FOX_KEYS_FULL)
    def _():
        update(FOX_KEYS_FULL)

    @pl.when(kind == FOX_KEYS_DIAG_HALF)
    def _():
        update(FOX_KEYS_DIAG_HALF)
        finish()

    @pl.when(kind == FOX_KEYS_FULL_DIAG)
    def _():
        update(FOX_KEYS_FULL_DIAG)
        finish()


def _fox_prompt(q, xq, k, xk, v, batch, seq):
    tq = _tile(seq // 2, 512)
    assert tq % LANE == 0 and seq % (2 * tq) == 0
    nq = seq // tq
    blocks = []
    for qi in range(nq):
        last = qi // 2
        blocks += [(qi, kj, FOX_KEYS_FULL) for kj in range(last)]
        blocks.append((qi, last, FOX_KEYS_DIAG_HALF if qi % 2 == 0 else FOX_KEYS_FULL_DIAG))
    tabs = [jnp.asarray([b[i] for b in blocks], jnp.int32) for i in range(3)]
    q_map = lambda b, s, qt, kt, kinds: (b * nq + qt[s], 0)
    kv_map = lambda b, s, qt, kt, kinds: (b * (nq // 2) + kt[s], 0)
    return pl.pallas_call(
        functools.partial(_fox_prompt_kernel, tq=tq),
        grid_spec=pltpu.PrefetchScalarGridSpec(
            num_scalar_prefetch=3,
            grid=(batch, len(blocks)),
            in_specs=[pl.BlockSpec((tq, FOX_W), q_map),
                      pl.BlockSpec((tq, FOX_W), q_map),
                      pl.BlockSpec((2 * tq, FOX_W), kv_map),
                      pl.BlockSpec((2 * tq, FOX_W), kv_map),
                      pl.BlockSpec((2 * tq, FOX_W), kv_map)],
            out_specs=pl.BlockSpec((tq, FOX_W), q_map),
            scratch_shapes=[pltpu.VMEM((FOX_HEADS, tq, LANE), F32),
                            pltpu.VMEM((FOX_HEADS, tq, LANE), F32),
                            pltpu.VMEM((tq, FOX_W), F32)]),
        out_shape=jax.ShapeDtypeStruct((batch * seq, FOX_W), BF16),
        compiler_params=_params("parallel", "arbitrary"),
        name="fox_prompt",
    )(*tabs, q, xq, k, xk, v)


def _fox_sample_kernel(q_ref, kn_ref, vn_ref, cq_ref, ck_ref, kc_ref, vc_ref, o_ref, *, past, s_new):
    kc = pltpu.einshape("thd->htd", kc_ref[0, 0])
    vc = pltpu.einshape("thd->htd", vc_ref[0, 0])
    row = lax.broadcasted_iota(jnp.int32, (s_new, s_new), 0)
    col = lax.broadcasted_iota(jnp.int32, (s_new, s_new), 1)
    causal = col <= row
    for h in range(FOX_HEADS):
        hs = slice(h * FOX_HD, (h + 1) * FOX_HD)
        q = q_ref[:, hs]
        cq = cq_ref[:, h:h + 1]
        s_c = _dot(q, kc[h].astype(BF16), _NT) + (cq - ck_ref[0, h:h + 1, :past])
        s_n = _dot(q, kn_ref[:, hs], _NT) + (cq - ck_ref[0, h:h + 1, past:past + s_new])
        s_n = jnp.where(causal, s_n, NEG_INF)
        m = jnp.maximum(jnp.max(s_c, axis=-1, keepdims=True), jnp.max(s_n, axis=-1, keepdims=True))
        p_c = jnp.exp2(s_c - m)
        p_n = jnp.exp2(s_n - m)
        l = jnp.sum(p_c, axis=-1, keepdims=True) + jnp.sum(p_n, axis=-1, keepdims=True)
        o = _dot(p_c.astype(BF16), vc[h].astype(BF16)) + _dot(p_n.astype(BF16), vn_ref[:, hs])
        o_ref[:, hs] = (o / l).astype(o_ref.dtype)


def _fox_sample(q, k, v, k_cache, v_cache, layer, c_tm, c_rows, batch, s_new, past):
    tpad = c_rows.shape[-1]
    new_spec = pl.BlockSpec((s_new, FOX_W), lambda b: (b, 0))
    cache_spec = pl.BlockSpec((1, 1, past, FOX_HEADS, FOX_HD), lambda b: (layer, b, 0, 0, 0))
    return pl.pallas_call(
        functools.partial(_fox_sample_kernel, past=past, s_new=s_new),
        grid=(batch,),
        in_specs=[new_spec, new_spec, new_spec,
                  pl.BlockSpec((s_new, FOX_HEADS), lambda b: (b, 0)),
                  pl.BlockSpec((1, FOX_HEADS, tpad), lambda b: (b, 0, 0)), cache_spec, cache_spec],
        out_specs=new_spec,
        out_shape=jax.ShapeDtypeStruct((batch * s_new, FOX_W), BF16),
        compiler_params=_params("parallel"),
        name="fox_sample",
    )(q, k, v, c_tm, c_rows, k_cache, v_cache)


def _rwkv_prep_kernel(p_ref, prev_ref, mu_ref, w0_ref, a0_ref, kk_ref, ka_ref, wup_ref, aup_ref, gup_ref,
                      r_ref, lw_ref, k_ref, v_ref, kk_out_ref, a_ref, g_ref, carry_ref, *, tr):
    @pl.when(pl.program_id(1) == 0)
    def _():
        carry_ref[...] = prev_ref[0]

    p = p_ref[0]
    rows = lax.broadcasted_iota(jnp.int32, (tr, 1), 0)
    p_prev = jnp.where(rows == 0, carry_ref[...], pltpu.roll(p, shift=1, axis=0))
    carry_ref[...] = p[tr - 1:tr, :]
    xs = p + (p_prev - p) * mu_ref[...]

    r = xs[:, :RWKV_W]
    k = xs[:, RWKV_W:2 * RWKV_W]
    v = xs[:, 2 * RWKV_W:3 * RWKV_W]
    d_wa = xs[:, RWKV_LORA_OFF:RWKV_LORA_OFF + LANE]
    d_g = xs[:, RWKV_LORA_OFF + LANE:RWKV_LORA_OFF + 2 * LANE]

    z_w = w0_ref[...] + _dot(jnp.tanh(d_wa).astype(BF16), wup_ref[...])
    log_decay = -jnp.exp(_log_sigmoid(z_w) - 0.5)
    a = _sigmoid(a0_ref[...] + _dot(d_wa.astype(BF16), aup_ref[...]))
    g = _dot(_sigmoid(d_g).astype(BF16), gup_ref[...])

    r_ref[...] = r.astype(r_ref.dtype)
    lw_ref[...] = log_decay
    k_ref[...] = (k * (1.0 + (a - 1.0) * ka_ref[...])).astype(k_ref.dtype)
    v_ref[...] = v.astype(v_ref.dtype)
    kk_out_ref[...] = (k * kk_ref[...]).astype(kk_out_ref.dtype)
    a_ref[...] = a.astype(a_ref.dtype)
    g_ref[...] = g.astype(g_ref.dtype)


def _rwkv_prep(rw, prev, lw, nseq, t):
    tr = _tile(t, 256)
    nb = t // tr
    wide = lambda s, i: (0, 0)
    row_spec = pl.BlockSpec((tr, RWKV_W), lambda s, i: (s * nb + i, 0))
    out_dtypes = (BF16, F32, BF16, BF16, BF16, BF16, BF16)
    return pl.pallas_call(
        functools.partial(_rwkv_prep_kernel, tr=tr),
        grid=(nseq, nb),
        in_specs=[pl.BlockSpec((1, tr, RWKV_IN_W), lambda s, i: (s, i, 0)),
                  pl.BlockSpec((1, 1, RWKV_IN_W), lambda s, i: (s, 0, 0)),
                  pl.BlockSpec((1, RWKV_IN_W), wide),
                  pl.BlockSpec((1, RWKV_W), wide),
                  pl.BlockSpec((1, RWKV_W), wide),
                  pl.BlockSpec((1, RWKV_W), wide),
                  pl.BlockSpec((1, RWKV_W), wide),
                  pl.BlockSpec((LANE, RWKV_W), wide),
                  pl.BlockSpec((LANE, RWKV_W), wide),
                  pl.BlockSpec((LANE, RWKV_W), wide)],
        out_specs=[row_spec] * 7,
        out_shape=[jax.ShapeDtypeStruct((nseq * t, RWKV_W), dt) for dt in out_dtypes],
        scratch_shapes=[pltpu.VMEM((1, RWKV_IN_W), F32)],
        compiler_params=_params("parallel", "arbitrary"),
        name="rwkv_prep",
    )(rw.reshape(nseq, t, RWKV_IN_W), prev, lw["mu"], lw["w0"], lw["a0"], lw["k_k"], lw["k_a"],
      lw["w_up"], lw["a_up"], lw["g_up"])


RWKV_CHUNKS_PER_STEP = 2


def _rwkv_chunk_kernel(*refs, c, has_s0):
    r_ref, lw_ref, k_ref, v_ref, kk_ref, lr_ref, g_ref, gng_ref, gnb_ref, rk_ref = refs[:10]
    s0_ref = refs[10] if has_s0 else None
    o_ref, sout_ref, s_ref = refs[-3:]
    ci = pl.program_id(1)
    pairs = range(RWKV_PAIRS)

    @pl.when(ci == 0)
    def _():
        if has_s0:
            zero = jnp.zeros((RWKV_HD, RWKV_HD), F32)
            for pr in pairs:
                top = jnp.concatenate([s0_ref[0, 0, 2 * pr], zero], axis=1)
                bot = jnp.concatenate([zero, s0_ref[0, 0, 2 * pr + 1]], axis=1)
                s_ref[pr] = jnp.concatenate([top, bot], axis=0)
        else:
            s_ref[...] = jnp.zeros_like(s_ref)

    c2 = 2 * c
    row = lax.broadcasted_iota(jnp.int32, (c2, c2), 0)
    col = lax.broadcasted_iota(jnp.int32, (c2, c2), 1)
    lower = col <= row
    strict = col < row
    eye = jnp.where(row == col, 1.0, 0.0)
    rc = lax.broadcasted_iota(jnp.int32, (c, c), 0)
    cc = lax.broadcasted_iota(jnp.int32, (c, c), 1)
    tri = jnp.where(cc <= rc, 1.0, 0.0).astype(BF16)
    lane = lax.broadcasted_iota(jnp.int32, (c, LANE), 1)
    first = lane < RWKV_HD
    steps = max(c.bit_length() - 2, 0)

    def stack(x):
        return jnp.concatenate([jnp.where(first, x, 0.0), jnp.where(first, 0.0, x)], axis=0)

    own = (lax.broadcasted_iota(jnp.int32, (c2, LANE), 1) // RWKV_HD
           == lax.broadcasted_iota(jnp.int32, (c2, LANE), 0) // c)

    n_sub = r_ref.shape[0] // c
    lanes = [slice(pr * LANE, (pr + 1) * LANE) for pr in pairs]
    units = [(slice(sb * c, (sb + 1) * c), ls) for sb in range(n_sub) for ls in lanes]
    lws = [lw_ref[rows, ls] for rows, ls in units]
    cum3 = [_dot(tri, jnp.concatenate(_split3(lw), axis=1)) for lw in lws]
    cums = [x[:, :LANE] + x[:, LANE:2 * LANE] + x[:, 2 * LANE:] for x in cum3]
    totals = [cum[c - 1:c, :] for cum in cums]

    lhs, rhs, bar, v_s, rk_s = [], [], [], [], []
    for (rows, ls), lw, cum, total in zip(units, lws, cums, totals):
        grow = jnp.exp(-cum)
        rest = jnp.exp(total - cum)
        r = r_ref[rows, ls].astype(F32)
        k = k_ref[rows, ls].astype(F32)
        kk = kk_ref[rows, ls].astype(F32)
        norm = lax.rsqrt(jnp.maximum(jnp.sum(stack(kk * kk), axis=-1, keepdims=True), 1e-24))
        kk = kk * jnp.where(first, norm[:c], norm[c:])
        b = kk * lr_ref[rows, ls].astype(F32)
        lhs.append(jnp.concatenate([stack(-kk * jnp.exp(cum - lw)), stack(r * jnp.exp(cum))], axis=0).astype(BF16))
        rhs.append(jnp.concatenate([stack(k * grow), stack(b * grow)], axis=0).astype(BF16))
        bar.append(jnp.concatenate([stack(k * rest), stack(b * rest)], axis=0).astype(BF16))
        v_s.append(stack(v_ref[rows, ls].astype(F32)))
        rk_s.append(stack(r * k * rk_ref[:, ls]))

    amats = [_dot(x, y, _NT) for x, y in zip(lhs, rhs)]
    a_ab = [jnp.where(strict, m[:c2, c2:], 0.0) for m in amats]
    invs = [eye + n for n in a_ab]
    powers = [_dot(n.astype(BF16), n.astype(BF16)) for n in a_ab]
    for i in range(steps):
        if i < steps - 1:
            xs = [_dot(jnp.concatenate([inv, pw], axis=0).astype(BF16), pw.astype(BF16))
                  for inv, pw in zip(invs, powers)]
            invs = [inv + x[:c2] for inv, x in zip(invs, xs)]
            powers = [x[c2:] for x in xs]
        else:
            invs = [inv + _dot(inv.astype(BF16), pw.astype(BF16)) for inv, pw in zip(invs, powers)]

    akv = [_dot(jnp.where(strict, m[:c2, :c2], 0.0).astype(BF16), v.astype(BF16)) for m, v in zip(amats, v_s)]
    a_rs = [jnp.concatenate([jnp.where(lower, m[c2:, :c2], 0.0), jnp.where(lower, m[c2:, c2:], 0.0)],
                            axis=1).astype(BF16) for m in amats]

    s_mats = [s_ref[pr] for pr in pairs]
    for sb in range(n_sub):
        base = sb * RWKV_PAIRS
        sxs = [_dot(lhs[base + pr], s_mats[pr].astype(BF16), _NT) for pr in pairs]
        us = [_dot(invs[base + pr].astype(BF16), (sxs[pr][:c2] + akv[base + pr]).astype(BF16)) for pr in pairs]
        vus = [jnp.concatenate([v_s[base + pr], us[pr]], axis=0).astype(BF16) for pr in pairs]
        for pr in pairs:
            i = base + pr
            rows, ls = units[i]
            y_s = sxs[pr][c2:] + _dot(a_rs[i], vus[pr])
            s_mats[pr] = s_mats[pr] * jnp.exp(totals[i]) + _dot(vus[pr], bar[i], _TN)
            mu = jnp.sum(y_s, axis=-1, keepdims=True) * (1.0 / RWKV_HD)
            d = jnp.where(own, y_s - mu, 0.0)
            var = jnp.sum(d * d, axis=-1, keepdims=True) * (1.0 / RWKV_HD)
            yn_s = d * lax.rsqrt(var + RWKV_GN_EPS)
            bonus_s = jnp.sum(rk_s[i], axis=-1, keepdims=True) * v_s[i]
            yn = (yn_s[:c] + yn_s[c:]) * gng_ref[:, ls] + gnb_ref[:, ls]
            o_ref[rows, ls] = ((yn + (bonus_s[:c] + bonus_s[c:])) * g_ref[rows, ls].astype(F32)).astype(o_ref.dtype)
    for pr in pairs:
        s_ref[pr] = s_mats[pr]

    @pl.when(ci == pl.num_programs(1) - 1)
    def _():
        for pr in pairs:
            s_mat = s_ref[pr]
            sout_ref[0, 2 * pr] = s_mat[:RWKV_HD, :RWKV_HD]
            sout_ref[0, 2 * pr + 1] = s_mat[RWKV_HD:, RWKV_HD:]


def _rwkv_chunks(r, lwd, k, v, kk, lr, g, lw, state, layer, nseq, t, c):
    rows = c * RWKV_CHUNKS_PER_STEP if t % (c * RWKV_CHUNKS_PER_STEP) == 0 else c
    nc = t // rows
    row_spec = pl.BlockSpec((rows, RWKV_W), lambda s, i: (s * nc + i, 0))
    par_spec = pl.BlockSpec((1, RWKV_W), lambda s, i: (0, 0))
    in_specs = [row_spec] * 7 + [par_spec] * 3
    args = [r, lwd, k, v, kk, lr, g, lw["gn_g"], lw["gn_b"], lw["r_k"]]
    if state is not None:
        in_specs.append(pl.BlockSpec((1, 1, RWKV_HEADS, RWKV_HD, RWKV_HD), lambda s, i: (layer, s, 0, 0, 0)))
        args.append(state)
    return pl.pallas_call(
        functools.partial(_rwkv_chunk_kernel, c=c, has_s0=state is not None),
        grid=(nseq, nc),
        in_specs=in_specs,
        out_specs=[row_spec, pl.BlockSpec((1, RWKV_HEADS, RWKV_HD, RWKV_HD), lambda s, i: (s, 0, 0, 0))],
        out_shape=[jax.ShapeDtypeStruct((nseq * t, RWKV_W), BF16),
                   jax.ShapeDtypeStruct((nseq, RWKV_HEADS, RWKV_HD, RWKV_HD), F32)],
        scratch_shapes=[pltpu.VMEM((RWKV_PAIRS, LANE, LANE), F32)],
        compiler_params=_params("parallel", "arbitrary"),
        name="rwkv_chunks",
    )(*args)


def _gmlp_kernel(gugv_ref, lng_ref, lnb_ref, ws_ref, bs_ref, o_ref, *vn_out):
    for blk in range(o_ref.shape[0] // GMLP_CHUNK):
        rows = slice(blk * GMLP_CHUNK, (blk + 1) * GMLP_CHUNK)
        u = _gelu_tanh(gugv_ref[rows, :GMLP_W])
        gv = _gelu_tanh(gugv_ref[rows, GMLP_W:])
        mu = jnp.mean(gv, axis=-1, keepdims=True)
        d = gv - mu
        var = jnp.mean(d * d, axis=-1, keepdims=True)
        vn = d * lax.rsqrt(var + LN_EPS) * lng_ref[...] + lnb_ref[...]
        if vn_out:
            vn_out[0][rows, :] = vn
        vb = vn.astype(BF16)
        for g in range(GMLP_GROUPS):
            gs = slice(g * GMLP_GC, (g + 1) * GMLP_GC)
            s = _dot(ws_ref[g], vb[:, gs]) + bs_ref[:, g:g + 1]
            o_ref[rows, gs] = (u[:, gs] * s).astype(o_ref.dtype)


def _gmlp(gugv, lw, w_eff, b_eff, emit_vn):
    m = gugv.shape[0]
    tr = _tile(m, 4 * GMLP_CHUNK)
    assert tr % GMLP_CHUNK == 0
    row_spec = pl.BlockSpec((tr, GMLP_W), lambda i: (i, 0))
    out_shape = [jax.ShapeDtypeStruct((m, GMLP_W), BF16)]
    out_specs = [row_spec]
    if emit_vn:
        out_shape.append(jax.ShapeDtypeStruct((m, GMLP_W), F32))
        out_specs.append(row_spec)
    res = pl.pallas_call(
        _gmlp_kernel,
        grid=(m // tr,),
        in_specs=[pl.BlockSpec((tr, 2 * GMLP_W), lambda i: (i, 0)),
                  pl.BlockSpec((1, GMLP_W), lambda i: (0, 0)),
                  pl.BlockSpec((1, GMLP_W), lambda i: (0, 0)),
                  pl.BlockSpec((GMLP_GROUPS, GMLP_CHUNK, GMLP_CHUNK), lambda i: (0, 0, 0)),
                  pl.BlockSpec((GMLP_CHUNK, GMLP_GROUPS), lambda i: (0, 0))],
        out_specs=out_specs,
        out_shape=out_shape,
        compiler_params=_params("parallel"),
        name="gmlp_sgu",
    )(gugv, lw["ln_g"], lw["ln_b"], w_eff, b_eff)
    return res if emit_vn else (res[0], None)


def _merge_kernel(oa_ref, ob_ref, oc_ref, ga_ref, gb_ref, gc_ref, wa_ref, wb_ref, wc_ref, o_ref):
    m = _sigmoid(ga_ref[...].astype(F32)) * _dot(oa_ref[...], wa_ref[...])
    m = m + _sigmoid(gb_ref[...].astype(F32)) * _dot(ob_ref[...], wb_ref[...])
    m = m + _sigmoid(gc_ref[...].astype(F32)) * _dot(oc_ref[...], wc_ref[...])
    o_ref[...] = m.astype(o_ref.dtype)


def _merge(o_a, o_b, o_c, gates, w_a, w_b, w_c, layer):
    m, kw = o_a.shape
    d = w_a.shape[2]
    tm = _tile(m, 512)
    tn = _col_tile(d, 1024)
    nj = d // tn
    o_spec = pl.BlockSpec((tm, kw), lambda j, i: (i, 0))
    w_spec = pl.BlockSpec((None, kw, tn), lambda j, i: (layer, 0, j))
    return pl.pallas_call(
        _merge_kernel,
        grid=(nj, m // tm),
        in_specs=[o_spec, o_spec, o_spec,
                  pl.BlockSpec((tm, tn), lambda j, i: (i, j)),
                  pl.BlockSpec((tm, tn), lambda j, i: (i, nj + j)),
                  pl.BlockSpec((tm, tn), lambda j, i: (i, 2 * nj + j)),
                  w_spec, w_spec, w_spec],
        out_specs=pl.BlockSpec((tm, tn), lambda j, i: (i, j)),
        out_shape=jax.ShapeDtypeStruct((m, d), BF16),
        compiler_params=_params("parallel", "parallel"),
        name="branch_merge",
    )(o_a, o_b, o_c, gates, gates, gates, w_a, w_b, w_c)


def _out_proj_kernel(m_ref, w_ref, x_ref, g_ref, o_ref):
    o_ref[...] = x_ref[...] + _rms(_dot(m_ref[...], w_ref[...]), g_ref[...])


def _out_proj(mm, w_out, layer, x, g):
    m, d = x.shape
    tm = _tile(m, 512)
    return pl.pallas_call(
        _out_proj_kernel,
        grid=(m // tm,),
        in_specs=[pl.BlockSpec((tm, d), lambda i: (i, 0)),
                  pl.BlockSpec((None, d, d), lambda i: (layer, 0, 0)),
                  pl.BlockSpec((tm, d), lambda i: (i, 0)),
                  pl.BlockSpec((1, d), lambda i: (0, 0))],
        out_specs=pl.BlockSpec((tm, d), lambda i: (i, 0)),
        out_shape=jax.ShapeDtypeStruct((m, d), F32),
        compiler_params=_params("parallel"),
        name="out_proj_residual",
    )(mm, w_out, x, g.reshape(1, d))


def _ffn_kernel(x_ref, gpre_ref, gpost_ref, gnext_ref, w1_ref, w3_ref, w2_ref, o_ref, hn_ref, h_ref, acc_ref):
    j = pl.program_id(1)

    @pl.when(j == 0)
    def _():
        h_ref[...] = _rms(x_ref[...], gpre_ref[...]).astype(BF16)
        acc_ref[...] = jnp.zeros_like(acc_ref)

    h = h_ref[...]
    a = _dot(h, w1_ref[...])
    b = _dot(h, w3_ref[...])
    gated = (a * _sigmoid(a)) * b
    acc_ref[...] += _dot(gated.astype(BF16), w2_ref[...])

    @pl.when(j == pl.num_programs(1) - 1)
    def _():
        x_new = x_ref[...] + _rms(acc_ref[...], gpost_ref[...])
        o_ref[...] = x_new
        hn_ref[...] = _rms(x_new, gnext_ref[...]).astype(hn_ref.dtype)


def _ffn(x, g_pre, g_post, g_next, w1, w3, w2, layer):
    m, d = x.shape
    dff = w1.shape[2]
    tm = _tile(m, 512)
    tf = _col_tile(dff, 512)
    vec = pl.BlockSpec((1, d), lambda i, j: (0, 0))
    row = pl.BlockSpec((tm, d), lambda i, j: (i, 0))
    return pl.pallas_call(
        _ffn_kernel,
        grid=(m // tm, dff // tf),
        in_specs=[row, vec, vec, vec,
                  pl.BlockSpec((None, d, tf), lambda i, j: (layer, 0, j)),
                  pl.BlockSpec((None, d, tf), lambda i, j: (layer, 0, j)),
                  pl.BlockSpec((None, tf, d), lambda i, j: (layer, j, 0))],
        out_specs=[row, row],
        out_shape=[jax.ShapeDtypeStruct((m, d), F32), jax.ShapeDtypeStruct((m, d), BF16)],
        scratch_shapes=[pltpu.VMEM((tm, d), BF16), pltpu.VMEM((tm, d), F32)],
        compiler_params=_params("parallel", "arbitrary"),
        name="ffn_swiglu",
    )(x, g_pre.reshape(1, d), g_post.reshape(1, d), g_next.reshape(1, d), w1, w3, w2)


def _pad_rows(w, rows, offset):
    out = jnp.zeros((rows, w.shape[1]), w.dtype)
    return out.at[offset:offset + w.shape[0]].set(w)


_O_FF = 3 * FOX_W
_O_RWKV = _O_FF + FOX_HEADS
_O_GMLP = _O_RWKV + RWKV_IN_W
_O_GATE = _O_GMLP + 2 * GMLP_W


def _stacked_weights(w_in, w_br_fox, w_br_rwkv, w_br_gmlp, w_out, ffn_w1, ffn_w3, ffn_w2):
    return {
        "wq": w_in[:, :, :FOX_W].astype(BF16),
        "wk": w_in[:, :, FOX_W:2 * FOX_W].astype(BF16),
        "wv": w_in[:, :, 2 * FOX_W:_O_FF].astype(BF16),
        "w_rwkv": w_in[:, :, _O_RWKV:_O_GMLP].astype(BF16),
        "w_gmlp": w_in[:, :, _O_GMLP:_O_GATE].astype(BF16),
        "w_gate": w_in[:, :, _O_GATE:].astype(BF16),
        "w_br_fox": w_br_fox.astype(BF16),
        "w_br_rwkv": w_br_rwkv.astype(BF16),
        "w_br_gmlp": w_br_gmlp.astype(BF16),
        "w_out": w_out.astype(BF16),
        "ffn_w1": ffn_w1.astype(BF16),
        "ffn_w3": ffn_w3.astype(BF16),
        "ffn_w2": ffn_w2.astype(BF16),
    }


def _layer_weights(l, d, w_in, fox_bf, rwkv_mu, rwkv_w0, rwkv_w_up, rwkv_a0, rwkv_a_up, rwkv_g_up, rwkv_k_k,
                   rwkv_k_a, rwkv_r_k, rwkv_gn_g, rwkv_gn_b, gmlp_ln_g, gmlp_ln_b, gmlp_w_s, gmlp_b_s, s_new):
    tril_full = jnp.tril(gmlp_w_s[l])
    blk = jnp.tril(gmlp_w_s[l][:, :s_new, :s_new])
    reps = GMLP_CHUNK // s_new
    eye = jnp.eye(reps, dtype=F32)
    w_sample = jnp.einsum("ab,gij->gaibj", eye, blk).reshape(GMLP_GROUPS, GMLP_CHUNK, GMLP_CHUNK)
    return {
        "w_ff": jnp.zeros((d, LANE), BF16).at[:, :FOX_HEADS].set(w_in[l, :, _O_FF:_O_RWKV].astype(BF16)),
        "b_ff": jnp.zeros((1, LANE), F32).at[0, :FOX_HEADS].set(fox_bf[l]),
        "rwkv": {
            "mu": rwkv_mu[l].reshape(1, RWKV_IN_W),
            "w0": rwkv_w0[l].reshape(1, RWKV_W),
            "a0": rwkv_a0[l].reshape(1, RWKV_W),
            "k_k": rwkv_k_k[l].reshape(1, RWKV_W),
            "k_a": rwkv_k_a[l].reshape(1, RWKV_W),
            "r_k": rwkv_r_k[l].reshape(1, RWKV_W),
            "gn_g": rwkv_gn_g[l].reshape(1, RWKV_W),
            "gn_b": rwkv_gn_b[l].reshape(1, RWKV_W),
            "w_up": _pad_rows(rwkv_w_up[l], LANE, 0).astype(BF16),
            "a_up": _pad_rows(rwkv_a_up[l], LANE, rwkv_w_up.shape[1]).astype(BF16),
            "g_up": rwkv_g_up[l].astype(BF16),
        },
        "gmlp": {"ln_g": gmlp_ln_g[l].reshape(1, GMLP_W), "ln_b": gmlp_ln_b[l].reshape(1, GMLP_W)},
        "gmlp_w_prompt": tril_full.astype(BF16),
        "gmlp_b_prompt": gmlp_b_s[l].T,
        "gmlp_w_sample": w_sample.astype(BF16),
        "gmlp_b_sample": jnp.tile(gmlp_b_s[l][:, :s_new].T, (reps, 1)),
    }


def _trunk_layer(x, h, big, lw, post_mix_g, pre_ffn_g, post_ffn_g, next_mix_g, nseq, t, layer, cache):
    m, d = x.shape
    q = _matmul(h, big["wq"], layer, (BF16,), scale=FOX_Q_SCALE, tn_pref=1024)
    k, k_bf = _kv_proj(h, big["wk"], layer)
    v, v_bf = _kv_proj(h, big["wv"], layer)
    logf = _forget_logits(h, lw["w_ff"], lw["b_ff"])[:, :FOX_HEADS]
    rw = _matmul(h, big["w_rwkv"], layer, tn_pref=1664)
    gugv = _matmul(h, big["w_gmlp"], layer, tn_pref=1024)
    gates = _matmul(h, big["w_gate"], layer, (BF16,), tn_pref=1536)

    logf_rows = logf.reshape(nseq, t, FOX_HEADS).transpose(0, 2, 1)
    if cache is None:
        c_rows = _cumsum_lanes(logf_rows.reshape(nseq * FOX_HEADS, t), _tile(t, 256), LOG2E)
        c_tm = c_rows.reshape(nseq, FOX_HEADS, t).transpose(0, 2, 1).reshape(m, FOX_HEADS)
        xq, xk = _fox_bias_columns(c_tm)
        o_a = _fox_prompt(q, xq, k_bf, xk, v_bf, nseq, t)
        prev = jnp.zeros((nseq, 1, RWKV_IN_W), F32)
        state = None
        chunk = 64
        gm_w, gm_b = lw["gmlp_w_prompt"], lw["gmlp_b_prompt"]
    else:
        past = cache["k"].shape[2]
        total = past + t
        tpad = -(-total // LANE) * LANE
        lf_all = jnp.concatenate([cache["logf"][layer].astype(F32).transpose(0, 2, 1), logf_rows], axis=-1)
        lf_all = jnp.pad(lf_all, ((0, 0), (0, 0), (0, tpad - total)))
        c_rows = _cumsum_lanes(lf_all.reshape(nseq * FOX_HEADS, tpad), LANE, LOG2E).reshape(nseq, FOX_HEADS, tpad)
        c_tm = c_rows[:, :, past:total].transpose(0, 2, 1).reshape(m, FOX_HEADS)
        o_a = _fox_sample(q, k_bf, v_bf, cache["k"], cache["v"], layer, c_tm, c_rows, nseq, t, past)
        prev = cache["shift"][layer]
        state = cache["state"]
        chunk = t
        gm_w, gm_b = lw["gmlp_w_sample"], lw["gmlp_b_sample"]

    r, lwd, k2, v2, kk, lr, g = _rwkv_prep(rw, prev, lw["rwkv"], nseq, t)
    o_b, s_new = _rwkv_chunks(r, lwd, k2, v2, kk, lr, g, lw["rwkv"], state, layer, nseq, t, chunk)
    shift_new = rw.reshape(nseq, t, RWKV_IN_W)[:, t - 1:t, :]

    o_c, vn = _gmlp(gugv, lw["gmlp"], gm_w, gm_b, cache is not None)

    mm = _merge(o_a, o_b, o_c, gates, big["w_br_fox"], big["w_br_rwkv"], big["w_br_gmlp"], layer)
    x = _out_proj(mm, big["w_out"], layer, x, post_mix_g)
    x, h_next = _ffn(x, pre_ffn_g, post_ffn_g, next_mix_g, big["ffn_w1"], big["ffn_w3"], big["ffn_w2"], layer)
    return x, h_next, (k, v, logf, s_new, shift_new, vn)


def kernel(x_prompt, x_sample, cache_fox_k, cache_fox_v, cache_fox_logf, state_rwkv, state_rwkv_shift, pre_mix_g, w_in, fox_bf, rwkv_mu, rwkv_w0, rwkv_w_up, rwkv_a0, rwkv_a_up, rwkv_g_up, rwkv_k_k, rwkv_k_a, rwkv_r_k, rwkv_gn_g, rwkv_gn_b, gmlp_ln_g, gmlp_ln_b, gmlp_w_s, gmlp_b_s, w_br_fox, w_br_rwkv, w_br_gmlp, w_out, post_mix_g, pre_ffn_g, ffn_w1, ffn_w3, ffn_w2, post_ffn_g):
    batch, seq, d = x_prompt.shape
    dec_batch, dec_seq, _ = x_sample.shape
    depth = w_in.shape[0]
    assert rwkv_w_up.shape[1] + rwkv_a_up.shape[1] == LANE and rwkv_g_up.shape[1] == LANE
    assert seq % GMLP_CHUNK == 0 and GMLP_CHUNK % dec_seq == 0 and (dec_batch * dec_seq) % GMLP_CHUNK == 0

    xp = x_prompt.reshape(batch * seq, d)
    xs = x_sample.reshape(dec_batch * dec_seq, d)
    cache = {"k": cache_fox_k, "v": cache_fox_v, "logf": cache_fox_logf,
             "state": state_rwkv, "shift": state_rwkv_shift}
    big = _stacked_weights(w_in, w_br_fox, w_br_rwkv, w_br_gmlp, w_out, ffn_w1, ffn_w3, ffn_w2)
    outs_p = [[] for _ in range(5)]
    outs_s = [[] for _ in range(6)]
    hp = _rmsnorm_cast(xp, pre_mix_g[0])
    hs = _rmsnorm_cast(xs, pre_mix_g[0])
    for l in range(depth):
        lw = _layer_weights(l, d, w_in, fox_bf, rwkv_mu, rwkv_w0, rwkv_w_up, rwkv_a0, rwkv_a_up, rwkv_g_up,
                            rwkv_k_k, rwkv_k_a, rwkv_r_k, rwkv_gn_g, rwkv_gn_b, gmlp_ln_g, gmlp_ln_b, gmlp_w_s,
                            gmlp_b_s, dec_seq)
        norms = (post_mix_g[l], pre_ffn_g[l], post_ffn_g[l], pre_mix_g[(l + 1) % depth])
        xp, hp, (k_p, v_p, lf_p, s_p, sh_p, _) = _trunk_layer(xp, hp, big, lw, *norms, batch, seq, l, None)
        xs, hs, (k_s, v_s, lf_s, s_s, sh_s, vn_s) = _trunk_layer(xs, hs, big, lw, *norms, dec_batch, dec_seq, l, cache)
        for acc, val in zip(outs_p, (k_p.reshape(batch, seq, FOX_HEADS, FOX_HD), v_p.reshape(batch, seq, FOX_HEADS, FOX_HD),
                                     lf_p.reshape(batch, seq, FOX_HEADS), s_p, sh_p)):
            acc.append(val)
        for acc, val in zip(outs_s, (k_s.reshape(dec_batch, dec_seq, FOX_HEADS, FOX_HD),
                                     v_s.reshape(dec_batch, dec_seq, FOX_HEADS, FOX_HD),
                                     lf_s.reshape(dec_batch, dec_seq, FOX_HEADS), s_s, sh_s,
                                     vn_s.reshape(dec_batch, dec_seq, GMLP_W))):
            acc.append(val)
    return (xp.reshape(batch, seq, d), xs.reshape(dec_batch, dec_seq, d),
            *[jnp.stack(o) for o in outs_p], *[jnp.stack(o) for o in outs_s])
```

```python
import functools
import math

import jax
import jax.numpy as jnp
from jax import lax
from jax.experimental import pallas as pl
from jax.experimental.pallas import tpu as pltpu

F32 = jnp.float32
BF16 = jnp.bfloat16

RMS_EPS = 1e-6
LN_EPS = 1e-5
RWKV_GN_EPS = 64e-5
NEG_INF = -1e30
LOG2E = math.log2(math.e)

LANE = 128
FOX_HEADS = 8
FOX_HD = 128
FOX_W = FOX_HEADS * FOX_HD
FOX_Q_SCALE = FOX_HD ** -0.5 * LOG2E
RWKV_HEADS = 16
RWKV_HD = 64
RWKV_W = RWKV_HEADS * RWKV_HD
RWKV_PAIRS = RWKV_W // LANE
RWKV_LORA_OFF = 3 * RWKV_W
RWKV_IN_W = 3 * RWKV_W + 2 * LANE
GMLP_W = 1024
GMLP_GROUPS = 8
GMLP_GC = GMLP_W // GMLP_GROUPS
GMLP_CHUNK = 128
N_BRANCH = 3

VMEM_LIMIT = 48 * 1024 * 1024


def _params(*sem):
    return pltpu.CompilerParams(dimension_semantics=sem, vmem_limit_bytes=VMEM_LIMIT)


def _tile(n, pref):
    if n <= pref:
        return n
    t = pref
    while n % t:
        t -= 8
    return t


def _col_tile(n, pref):
    assert n % LANE == 0
    k = n // LANE
    best = 1
    for d in range(1, k + 1):
        if k % d == 0 and d * LANE <= pref:
            best = d
    return best * LANE


def _split3(x):
    hi = x.astype(BF16)
    r = x - hi.astype(F32)
    mid = r.astype(BF16)
    lo = (r - mid.astype(F32)).astype(BF16)
    return hi, mid, lo


_NN = (((1,), (0,)), ((), ()))
_NT = (((1,), (1,)), ((), ()))
_TN = (((0,), (0,)), ((), ()))


def _dot(a, b, dims=_NN):
    return lax.dot_general(a, b, dims, preferred_element_type=F32)


def _dot_exact_rhs(x, e):
    hi, mid, lo = _split3(x)
    return _dot(hi, e) + _dot(mid, e) + _dot(lo, e)


def _sigmoid(x):
    return 1.0 / (1.0 + jnp.exp(-x))


def _log_sigmoid(x):
    return jnp.minimum(x, 0.0) - jnp.log(1.0 + jnp.exp(-jnp.abs(x)))


def _gelu_tanh(x):
    return 0.5 * x * (1.0 + jnp.tanh(0.7978845608028654 * (x + 0.044715 * (x * x * x))))


def _rms(x, g):
    ms = jnp.mean(x * x, axis=-1, keepdims=True)
    return x * lax.rsqrt(ms + RMS_EPS) * g


def _rmsnorm_cast_kernel(x_ref, g_ref, o_ref):
    o_ref[...] = _rms(x_ref[...], g_ref[...]).astype(o_ref.dtype)


def _rmsnorm_cast(x, g):
    m, d = x.shape
    tm = _tile(m, 512)
    return pl.pallas_call(
        _rmsnorm_cast_kernel,
        grid=(m // tm,),
        in_specs=[pl.BlockSpec((tm, d), lambda i: (i, 0)),
                  pl.BlockSpec((1, d), lambda i: (0, 0))],
        out_specs=pl.BlockSpec((tm, d), lambda i: (i, 0)),
        out_shape=jax.ShapeDtypeStruct((m, d), BF16),
        compiler_params=_params("parallel"),
        name="rmsnorm_cast",
    )(x, g.reshape(1, d))


def _matmul_kernel(a_ref, w_ref, *o_refs, scale):
    acc = _dot(a_ref[...], w_ref[...])
    if scale is not None:
        acc = acc * scale
    for o_ref in o_refs:
        o_ref[...] = acc.astype(o_ref.dtype)


def _matmul(a, w, layer, out_dtypes=(F32,), scale=None, tm_pref=1024, tn_pref=512):
    m, k = a.shape
    n = w.shape[2]
    tm = _tile(m, tm_pref)
    tn = _col_tile(n, tn_pref)
    o_spec = pl.BlockSpec((tm, tn), lambda i, j: (i, j))
    res = pl.pallas_call(
        functools.partial(_matmul_kernel, scale=scale),
        grid=(m // tm, n // tn),
        in_specs=[pl.BlockSpec((tm, k), lambda i, j: (i, 0)),
                  pl.BlockSpec((None, k, tn), lambda i, j: (layer, 0, j))],
        out_specs=[o_spec] * len(out_dtypes),
        out_shape=[jax.ShapeDtypeStruct((m, n), dt) for dt in out_dtypes],
        compiler_params=_params("parallel", "parallel"),
        name="proj_matmul",
    )(a, w)
    return res[0] if len(out_dtypes) == 1 else res


def _kv_proj_kernel(a_ref, w_ref, o_ref, obf_ref):
    acc = _dot(a_ref[...], w_ref[...])
    obf_ref[...] = acc.astype(obf_ref.dtype)
    o_ref[...] = pltpu.einshape("t(hd)->thd", acc, h=FOX_HEADS)


def _kv_proj(a, w, layer):
    m, k = a.shape
    tm = _tile(m, 512)
    return pl.pallas_call(
        _kv_proj_kernel,
        grid=(m // tm,),
        in_specs=[pl.BlockSpec((tm, k), lambda i: (i, 0)),
                  pl.BlockSpec((None, k, FOX_W), lambda i: (layer, 0, 0))],
        out_specs=[pl.BlockSpec((tm, FOX_HEADS, FOX_HD), lambda i: (i, 0, 0)),
                   pl.BlockSpec((tm, FOX_W), lambda i: (i, 0))],
        out_shape=[jax.ShapeDtypeStruct((m, FOX_HEADS, FOX_HD), F32),
                   jax.ShapeDtypeStruct((m, FOX_W), BF16)],
        compiler_params=_params("parallel"),
        name="kv_proj",
    )(a, w)


def _logf_kernel(a_ref, w_ref, b_ref, o_ref):
    o_ref[...] = _log_sigmoid(_dot(a_ref[...], w_ref[...]) + b_ref[...])


def _forget_logits(h, w_ff, b_ff):
    m, k = h.shape
    tm = _tile(m, 1024)
    return pl.pallas_call(
        _logf_kernel,
        grid=(m // tm,),
        in_specs=[pl.BlockSpec((tm, k), lambda i: (i, 0)),
                  pl.BlockSpec((k, LANE), lambda i: (0, 0)),
                  pl.BlockSpec((1, LANE), lambda i: (0, 0))],
        out_specs=pl.BlockSpec((tm, LANE), lambda i: (i, 0)),
        out_shape=jax.ShapeDtypeStruct((m, LANE), F32),
        compiler_params=_params("parallel"),
        name="forget_logits",
    )(h, w_ff, b_ff)


def _cumsum_kernel(x_ref, o_ref, carry_ref, *, tb, scale):
    @pl.when(pl.program_id(0) == 0)
    def _():
        carry_ref[...] = jnp.zeros_like(carry_ref)

    row = lax.broadcasted_iota(jnp.int32, (tb, tb), 0)
    col = lax.broadcasted_iota(jnp.int32, (tb, tb), 1)
    upper = jnp.where(row <= col, 1.0, 0.0).astype(BF16)
    c = _dot_exact_rhs(x_ref[...], upper) + carry_ref[:, :1]
    o_ref[...] = c * scale
    carry_ref[...] = jnp.broadcast_to(c[:, tb - 1:tb], carry_ref.shape)


def _cumsum_lanes(x, tb, scale):
    r, t = x.shape
    return pl.pallas_call(
        functools.partial(_cumsum_kernel, tb=tb, scale=scale),
        grid=(t // tb,),
        in_specs=[pl.BlockSpec((r, tb), lambda i: (0, i))],
        out_specs=pl.BlockSpec((r, tb), lambda i: (0, i)),
        out_shape=jax.ShapeDtypeStruct((r, t), F32),
        scratch_shapes=[pltpu.VMEM((r, LANE), F32)],
        compiler_params=_params("arbitrary"),
        name="logf_cumsum",
    )(x)


def _fox_bias_placement():
    part = jnp.arange(3 * LANE) // LANE
    head = jnp.arange(3 * LANE) % LANE
    out_head = jnp.arange(FOX_W) // FOX_HD
    out_lane = jnp.arange(FOX_W) % FOX_HD
    same_head = (head[:, None] == out_head[None, :]) & (head[:, None] < FOX_HEADS)
    pq = jnp.where(same_head & (out_lane[None, :] == part[:, None]), 1.0, 0.0).astype(BF16)
    pk = jnp.where(same_head & (out_lane[None, :] == part[:, None] + 3), -1.0, 0.0).astype(BF16)
    one_q = jnp.where((out_lane >= 3) & (out_lane < 6), 1.0, 0.0).reshape(1, FOX_W)
    one_k = jnp.where(out_lane < 3, 1.0, 0.0).reshape(1, FOX_W)
    return pq, pk, one_q, one_k


def _fox_forget_kernel(h_ref, w_ref, b_ref, pq_ref, pk_ref, oneq_ref, onek_ref, logf_ref, xq_ref, xk_ref, carry_ref,
                       *, tb):
    @pl.when(pl.program_id(1) == 0)
    def _():
        carry_ref[...] = jnp.zeros_like(carry_ref)

    logf = _log_sigmoid(_dot(h_ref[...], w_ref[...]) + b_ref[...])
    logf_ref[...] = logf
    row = lax.broadcasted_iota(jnp.int32, (tb, tb), 0)
    col = lax.broadcasted_iota(jnp.int32, (tb, tb), 1)
    lower = jnp.where(col <= row, 1.0, 0.0).astype(BF16)
    hi, mid, lo = _split3(logf)
    c = _dot(lower, hi) + _dot(lower, mid) + _dot(lower, lo) + carry_ref[:1, :]
    carry_ref[...] = jnp.broadcast_to(c[tb - 1:tb, :], carry_ref.shape)
    parts = jnp.concatenate(_split3(c * LOG2E), axis=1)
    xq_ref[...] = (_dot(parts, pq_ref[...]) + oneq_ref[...]).astype(BF16)
    xk_ref[...] = (_dot(parts, pk_ref[...]) + onek_ref[...]).astype(BF16)


def _fox_forget(h, w_ff, b_ff, nseq, t):
    m, k = h.shape
    tb = _tile(t, 256)
    nb = t // tb
    rows = lambda s, i: (s * nb + i, 0)
    const = lambda s, i: (0, 0)
    return pl.pallas_call(
        functools.partial(_fox_forget_kernel, tb=tb),
        grid=(nseq, nb),
        in_specs=[pl.BlockSpec((tb, k), rows),
                  pl.BlockSpec((k, LANE), const),
                  pl.BlockSpec((1, LANE), const),
                  pl.BlockSpec((3 * LANE, FOX_W), const),
                  pl.BlockSpec((3 * LANE, FOX_W), const),
                  pl.BlockSpec((1, FOX_W), const),
                  pl.BlockSpec((1, FOX_W), const)],
        out_specs=[pl.BlockSpec((tb, LANE), rows), pl.BlockSpec((tb, FOX_W), rows), pl.BlockSpec((tb, FOX_W), rows)],
        out_shape=[jax.ShapeDtypeStruct((m, LANE), F32),
                   jax.ShapeDtypeStruct((m, FOX_W), BF16),
                   jax.ShapeDtypeStruct((m, FOX_W), BF16)],
        scratch_shapes=[pltpu.VMEM((8, LANE), F32)],
        compiler_params=_params("parallel", "arbitrary"),
        name="fox_forget",
    )(h, w_ff, b_ff, *_fox_bias_placement())


FOX_KEYS_FULL, FOX_KEYS_DIAG_HALF, FOX_KEYS_FULL_DIAG = 0, 1, 2
FOX_HEAD_GROUP = 4


def _fox_prompt_kernel(qi_ref, kj_ref, kind_ref, q_ref, xq_ref, k_ref, xk_ref, v_ref, o_ref, m_ref, l_ref, acc_ref,
                       *, tq):
    step = pl.program_id(1)
    kj = kj_ref[step]
    kind = kind_ref[step]

    @pl.when(kj == 0)
    def _():
        m_ref[...] = jnp.full_like(m_ref, NEG_INF)
        l_ref[...] = jnp.zeros_like(l_ref)
        acc_ref[...] = jnp.zeros_like(acc_ref)

    def update(keys_kind):
        nk = tq if keys_kind == FOX_KEYS_DIAG_HALF else 2 * tq
        keys = slice(0, nk)
        if keys_kind != FOX_KEYS_FULL:
            row = lax.broadcasted_iota(jnp.int32, (tq, nk), 0)
            col = lax.broadcasted_iota(jnp.int32, (tq, nk), 1)
            causal = col <= row + (nk - tq)
        ones = jnp.ones((nk, LANE), BF16)
        for g0 in range(0, FOX_HEADS, FOX_HEAD_GROUP):
            heads = range(g0, g0 + FOX_HEAD_GROUP)
            cols = [slice(h * FOX_HD, (h + 1) * FOX_HD) for h in heads]
            s = [_dot(jnp.concatenate([q_ref[:, hs], xq_ref[:, hs]], axis=1),
                      jnp.concatenate([k_ref[keys, hs], xk_ref[keys, hs]], axis=1), _NT) for hs in cols]
            if keys_kind != FOX_KEYS_FULL:
                s = [jnp.where(causal, x, NEG_INF) for x in s]
            m_prev = [m_ref[h] for h in heads]
            m_new = [jnp.maximum(mp, jnp.max(x, axis=-1, keepdims=True)) for mp, x in zip(m_prev, s)]
            p = [jnp.exp2((x - jnp.concatenate([mn] * (nk // LANE), axis=1)).astype(BF16)) for x, mn in zip(s, m_new)]
            pv1 = [_dot(x, jnp.concatenate([v_ref[keys, hs], ones], axis=1)) for x, hs in zip(p, cols)]
            for i, h in enumerate(heads):
                alpha = jnp.exp2(m_prev[i] - m_new[i])
                l_ref[h] = alpha * l_ref[h] + pv1[i][:, LANE:]
                acc_ref[:, cols[i]] = alpha * acc_ref[:, cols[i]] + pv1[i][:, :LANE]
                m_ref[h] = m_new[i]

    def finish():
        for h in range(FOX_HEADS):
            hs = slice(h * FOX_HD, (h + 1) * FOX_HD)
            o_ref[:, hs] = (acc_ref[:, hs] / l_ref[h]).astype(o_ref.dtype)

    @pl.when(kind == FOX_KEYS_FULL)
    def _():
        update(FOX_KEYS_FULL)

    @pl.when(kind == FOX_KEYS_DIAG_HALF)
    def _():
        update(FOX_KEYS_DIAG_HALF)
        finish()

    @pl.when(kind == FOX_KEYS_FULL_DIAG)
    def _():
        update(FOX_KEYS_FULL_DIAG)
        finish()


def _fox_prompt(q, xq, k, xk, v, batch, seq):
    tq = _tile(seq // 2, 512)
    assert tq % LANE == 0 and seq % (2 * tq) == 0
    nq = seq // tq
    blocks = []
    for qi in range(nq):
        last = qi // 2
        blocks += [(qi, kj, FOX_KEYS_FULL) for kj in range(last)]
        blocks.append((qi, last, FOX_KEYS_DIAG_HALF if qi % 2 == 0 else FOX_KEYS_FULL_DIAG))
    tabs = [jnp.asarray([b[i] for b in blocks], jnp.int32) for i in range(3)]
    q_map = lambda b, s, qt, kt, kinds: (b * nq + qt[s], 0)
    kv_map = lambda b, s, qt, kt, kinds: (b * (nq // 2) + kt[s], 0)
    return pl.pallas_call(
        functools.partial(_fox_prompt_kernel, tq=tq),
        grid_spec=pltpu.PrefetchScalarGridSpec(
            num_scalar_prefetch=3,
            grid=(batch, len(blocks)),
            in_specs=[pl.BlockSpec((tq, FOX_W), q_map),
                      pl.BlockSpec((tq, FOX_W), q_map),
                      pl.BlockSpec((2 * tq, FOX_W), kv_map),
                      pl.BlockSpec((2 * tq, FOX_W), kv_map),
                      pl.BlockSpec((2 * tq, FOX_W), kv_map)],
            out_specs=pl.BlockSpec((tq, FOX_W), q_map),
            scratch_shapes=[pltpu.VMEM((FOX_HEADS, tq, LANE), F32),
                            pltpu.VMEM((FOX_HEADS, tq, LANE), F32),
                            pltpu.VMEM((tq, FOX_W), F32)]),
        out_shape=jax.ShapeDtypeStruct((batch * seq, FOX_W), BF16),
        compiler_params=_params("parallel", "arbitrary"),
        name="fox_prompt",
    )(*tabs, q, xq, k, xk, v)


def _fox_sample_kernel(q_ref, kn_ref, vn_ref, cq_ref, ck_ref, kc_ref, vc_ref, o_ref, *, past, s_new):
    kc = pltpu.einshape("thd->htd", kc_ref[0, 0])
    vc = pltpu.einshape("thd->htd", vc_ref[0, 0])
    row = lax.broadcasted_iota(jnp.int32, (s_new, s_new), 0)
    col = lax.broadcasted_iota(jnp.int32, (s_new, s_new), 1)
    causal = col <= row
    for h in range(FOX_HEADS):
        hs = slice(h * FOX_HD, (h + 1) * FOX_HD)
        q = q_ref[:, hs]
        cq = cq_ref[:, h:h + 1]
        s_c = _dot(q, kc[h].astype(BF16), _NT) + (cq - ck_ref[0, h:h + 1, :past])
        s_n = _dot(q, kn_ref[:, hs], _NT) + (cq - ck_ref[0, h:h + 1, past:past + s_new])
        s_n = jnp.where(causal, s_n, NEG_INF)
        m = jnp.maximum(jnp.max(s_c, axis=-1, keepdims=True), jnp.max(s_n, axis=-1, keepdims=True))
        p_c = jnp.exp2(s_c - m)
        p_n = jnp.exp2(s_n - m)
        l = jnp.sum(p_c, axis=-1, keepdims=True) + jnp.sum(p_n, axis=-1, keepdims=True)
        o = _dot(p_c.astype(BF16), vc[h].astype(BF16)) + _dot(p_n.astype(BF16), vn_ref[:, hs])
        o_ref[:, hs] = (o / l).astype(o_ref.dtype)


def _fox_sample(q, k, v, k_cache, v_cache, layer, c_tm, c_rows, batch, s_new, past):
    tpad = c_rows.shape[-1]
    new_spec = pl.BlockSpec((s_new, FOX_W), lambda b: (b, 0))
    cache_spec = pl.BlockSpec((1, 1, past, FOX_HEADS, FOX_HD), lambda b: (layer, b, 0, 0, 0))
    return pl.pallas_call(
        functools.partial(_fox_sample_kernel, past=past, s_new=s_new),
        grid=(batch,),
        in_specs=[new_spec, new_spec, new_spec,
                  pl.BlockSpec((s_new, FOX_HEADS), lambda b: (b, 0)),
                  pl.BlockSpec((1, FOX_HEADS, tpad), lambda b: (b, 0, 0)), cache_spec, cache_spec],
        out_specs=new_spec,
        out_shape=jax.ShapeDtypeStruct((batch * s_new, FOX_W), BF16),
        compiler_params=_params("parallel"),
        name="fox_sample",
    )(q, k, v, c_tm, c_rows, k_cache, v_cache)


def _rwkv_prep_kernel(p_ref, prev_ref, mu_ref, w0_ref, a0_ref, kk_ref, ka_ref, wup_ref, aup_ref, gup_ref,
                      r_ref, lw_ref, k_ref, v_ref, kk_out_ref, a_ref, g_ref, carry_ref, *, tr):
    @pl.when(pl.program_id(1) == 0)
    def _():
        carry_ref[...] = prev_ref[0]

    p = p_ref[0]
    rows = lax.broadcasted_iota(jnp.int32, (tr, 1), 0)
    p_prev = jnp.where(rows == 0, carry_ref[...], pltpu.roll(p, shift=1, axis=0))
    carry_ref[...] = p[tr - 1:tr, :]
    xs = p + (p_prev - p) * mu_ref[...]

    r = xs[:, :RWKV_W]
    k = xs[:, RWKV_W:2 * RWKV_W]
    v = xs[:, 2 * RWKV_W:3 * RWKV_W]
    d_wa = xs[:, RWKV_LORA_OFF:RWKV_LORA_OFF + LANE]
    d_g = xs[:, RWKV_LORA_OFF + LANE:RWKV_LORA_OFF + 2 * LANE]

    z_w = w0_ref[...] + _dot(jnp.tanh(d_wa).astype(BF16), wup_ref[...])
    log_decay = -jnp.exp(_log_sigmoid(z_w) - 0.5)
    a = _sigmoid(a0_ref[...] + _dot(d_wa.astype(BF16), aup_ref[...]))
    g = _dot(_sigmoid(d_g).astype(BF16), gup_ref[...])

    r_ref[...] = r.astype(r_ref.dtype)
    lw_ref[...] = log_decay
    k_ref[...] = (k * (1.0 + (a - 1.0) * ka_ref[...])).astype(k_ref.dtype)
    v_ref[...] = v.astype(v_ref.dtype)
    kk_out_ref[...] = (k * kk_ref[...]).astype(kk_out_ref.dtype)
    a_ref[...] = a.astype(a_ref.dtype)
    g_ref[...] = g.astype(g_ref.dtype)


def _rwkv_prep(rw, prev, lw, nseq, t):
    tr = _tile(t, 256)
    nb = t // tr
    wide = lambda s, i: (0, 0)
    row_spec = pl.BlockSpec((tr, RWKV_W), lambda s, i: (s * nb + i, 0))
    out_dtypes = (BF16, F32, BF16, BF16, BF16, BF16, BF16)
    return pl.pallas_call(
        functools.partial(_rwkv_prep_kernel, tr=tr),
        grid=(nseq, nb),
        in_specs=[pl.BlockSpec((1, tr, RWKV_IN_W), lambda s, i: (s, i, 0)),
                  pl.BlockSpec((1, 1, RWKV_IN_W), lambda s, i: (s, 0, 0)),
                  pl.BlockSpec((1, RWKV_IN_W), wide),
                  pl.BlockSpec((1, RWKV_W), wide),
                  pl.BlockSpec((1, RWKV_W), wide),
                  pl.BlockSpec((1, RWKV_W), wide),
                  pl.BlockSpec((1, RWKV_W), wide),
                  pl.BlockSpec((LANE, RWKV_W), wide),
                  pl.BlockSpec((LANE, RWKV_W), wide),
                  pl.BlockSpec((LANE, RWKV_W), wide)],
        out_specs=[row_spec] * 7,
        out_shape=[jax.ShapeDtypeStruct((nseq * t, RWKV_W), dt) for dt in out_dtypes],
        scratch_shapes=[pltpu.VMEM((1, RWKV_IN_W), F32)],
        compiler_params=_params("parallel", "arbitrary"),
        name="rwkv_prep",
    )(rw.reshape(nseq, t, RWKV_IN_W), prev, lw["mu"], lw["w0"], lw["a0"], lw["k_k"], lw["k_a"],
      lw["w_up"], lw["a_up"], lw["g_up"])


RWKV_CHUNKS_PER_STEP = 2


def _rwkv_chunk_kernel(*refs, c, has_s0):
    r_ref, lw_ref, k_ref, v_ref, kk_ref, lr_ref, g_ref, gng_ref, gnb_ref, rk_ref = refs[:10]
    s0_ref = refs[10] if has_s0 else None
    o_ref, sout_ref, s_ref = refs[-3:]
    ci = pl.program_id(1)
    pairs = range(RWKV_PAIRS)

    @pl.when(ci == 0)
    def _():
        if has_s0:
            zero = jnp.zeros((RWKV_HD, RWKV_HD), F32)
            for pr in pairs:
                top = jnp.concatenate([s0_ref[0, 0, 2 * pr], zero], axis=1)
                bot = jnp.concatenate([zero, s0_ref[0, 0, 2 * pr + 1]], axis=1)
                s_ref[pr] = jnp.concatenate([top, bot], axis=0)
        else:
            s_ref[...] = jnp.zeros_like(s_ref)

    c2 = 2 * c
    row = lax.broadcasted_iota(jnp.int32, (c2, c2), 0)
    col = lax.broadcasted_iota(jnp.int32, (c2, c2), 1)
    lower = col <= row
    strict = col < row
    eye = jnp.where(row == col, 1.0, 0.0)
    rc = lax.broadcasted_iota(jnp.int32, (c, c), 0)
    cc = lax.broadcasted_iota(jnp.int32, (c, c), 1)
    tri = jnp.where(cc <= rc, 1.0, 0.0).astype(BF16)
    lane = lax.broadcasted_iota(jnp.int32, (c, LANE), 1)
    first = lane < RWKV_HD
    steps = max(c.bit_length() - 2, 0)

    def stack(x):
        return jnp.concatenate([jnp.where(first, x, 0.0), jnp.where(first, 0.0, x)], axis=0)

    own = (lax.broadcasted_iota(jnp.int32, (c2, LANE), 1) // RWKV_HD
           == lax.broadcasted_iota(jnp.int32, (c2, LANE), 0) // c)

    n_sub = r_ref.shape[0] // c
    lanes = [slice(pr * LANE, (pr + 1) * LANE) for pr in pairs]
    units = [(slice(sb * c, (sb + 1) * c), ls) for sb in range(n_sub) for ls in lanes]
    lws = [lw_ref[rows, ls] for rows, ls in units]
    cum3 = [_dot(tri, jnp.concatenate(_split3(lw), axis=1)) for lw in lws]
    cums = [x[:, :LANE] + x[:, LANE:2 * LANE] + x[:, 2 * LANE:] for x in cum3]
    totals = [cum[c - 1:c, :] for cum in cums]

    lhs, rhs, bar, v_s, rk_s = [], [], [], [], []
    for (rows, ls), lw, cum, total in zip(units, lws, cums, totals):
        grow = jnp.exp(-cum)
        rest = jnp.exp(total - cum)
        r = r_ref[rows, ls].astype(F32)
        k = k_ref[rows, ls].astype(F32)
        kk = kk_ref[rows, ls].astype(F32)
        norm = lax.rsqrt(jnp.maximum(jnp.sum(stack(kk * kk), axis=-1, keepdims=True), 1e-24))
        kk = kk * jnp.where(first, norm[:c], norm[c:])
        b = kk * lr_ref[rows, ls].astype(F32)
        lhs.append(jnp.concatenate([stack(-kk * jnp.exp(cum - lw)), stack(r * jnp.exp(cum))], axis=0).astype(BF16))
        rhs.append(jnp.concatenate([stack(k * grow), stack(b * grow)], axis=0).astype(BF16))
        bar.append(jnp.concatenate([stack(k * rest), stack(b * rest)], axis=0).astype(BF16))
        v_s.append(stack(v_ref[rows, ls].astype(F32)))
        rk_s.append(stack(r * k * rk_ref[:, ls]))

    amats = [_dot(x, y, _NT) for x, y in zip(lhs, rhs)]
    a_ab = [jnp.where(strict, m[:c2, c2:], 0.0) for m in amats]
    invs = [eye + n for n in a_ab]
    powers = [_dot(n.astype(BF16), n.astype(BF16)) for n in a_ab]
    for i in range(steps):
        if i < steps - 1:
            xs = [_dot(jnp.concatenate([inv, pw], axis=0).astype(BF16), pw.astype(BF16))
                  for inv, pw in zip(invs, powers)]
            invs = [inv + x[:c2] for inv, x in zip(invs, xs)]
            powers = [x[c2:] for x in xs]
        else:
            invs = [inv + _dot(inv.astype(BF16), pw.astype(BF16)) for inv, pw in zip(invs, powers)]

    akv = [_dot(jnp.where(strict, m[:c2, :c2], 0.0).astype(BF16), v.astype(BF16)) for m, v in zip(amats, v_s)]
    a_rs = [jnp.concatenate([jnp.where(lower, m[c2:, :c2], 0.0), jnp.where(lower, m[c2:, c2:], 0.0)],
                            axis=1).astype(BF16) for m in amats]

    s_mats = [s_ref[pr] for pr in pairs]
    for sb in range(n_sub):
        base = sb * RWKV_PAIRS
        sxs = [_dot(lhs[base + pr], s_mats[pr].astype(BF16), _NT) for pr in pairs]
        us = [_dot(invs[base + pr].astype(BF16), (sxs[pr][:c2] + akv[base + pr]).astype(BF16)) for pr in pairs]
        vus = [jnp.concatenate([v_s[base + pr], us[pr]], axis=0).astype(BF16) for pr in pairs]
        for pr in pairs:
            i = base + pr
            rows, ls = units[i]
            y_s = sxs[pr][c2:] + _dot(a_rs[i], vus[pr])
            s_mats[pr] = s_mats[pr] * jnp.exp(totals[i]) + _dot(vus[pr], bar[i], _TN)
            mu = jnp.sum(y_s, axis=-1, keepdims=True) * (1.0 / RWKV_HD)
            d = jnp.where(own, y_s - mu, 0.0)
            var = jnp.sum(d * d, axis=-1, keepdims=True) * (1.0 / RWKV_HD)
            yn_s = d * lax.rsqrt(var + RWKV_GN_EPS)
            bonus_s = jnp.sum(rk_s[i], axis=-1, keepdims=True) * v_s[i]
            yn = (yn_s[:c] + yn_s[c:]) * gng_ref[:, ls] + gnb_ref[:, ls]
            o_ref[rows, ls] = ((yn + (bonus_s[:c] + bonus_s[c:])) * g_ref[rows, ls].astype(F32)).astype(o_ref.dtype)
    for pr in pairs:
        s_ref[pr] = s_mats[pr]

    @pl.when(ci == pl.num_programs(1) - 1)
    def _():
        for pr in pairs:
            s_mat = s_ref[pr]
            sout_ref[0, 2 * pr] = s_mat[:RWKV_HD, :RWKV_HD]
            sout_ref[0, 2 * pr + 1] = s_mat[RWKV_HD:, RWKV_HD:]


def _rwkv_chunks(r, lwd, k, v, kk, lr, g, lw, state, layer, nseq, t, c):
    rows = c * RWKV_CHUNKS_PER_STEP if t % (c * RWKV_CHUNKS_PER_STEP) == 0 else c
    nc = t // rows
    row_spec = pl.BlockSpec((rows, RWKV_W), lambda s, i: (s * nc + i, 0))
    par_spec = pl.BlockSpec((1, RWKV_W), lambda s, i: (0, 0))
    in_specs = [row_spec] * 7 + [par_spec] * 3
    args = [r, lwd, k, v, kk, lr, g, lw["gn_g"], lw["gn_b"], lw["r_k"]]
    if state is not None:
        in_specs.append(pl.BlockSpec((1, 1, RWKV_HEADS, RWKV_HD, RWKV_HD), lambda s, i: (layer, s, 0, 0, 0)))
        args.append(state)
    return pl.pallas_call(
        functools.partial(_rwkv_chunk_kernel, c=c, has_s0=state is not None),
        grid=(nseq, nc),
        in_specs=in_specs,
        out_specs=[row_spec, pl.BlockSpec((1, RWKV_HEADS, RWKV_HD, RWKV_HD), lambda s, i: (s, 0, 0, 0))],
        out_shape=[jax.ShapeDtypeStruct((nseq * t, RWKV_W), BF16),
                   jax.ShapeDtypeStruct((nseq, RWKV_HEADS, RWKV_HD, RWKV_HD), F32)],
        scratch_shapes=[pltpu.VMEM((RWKV_PAIRS, LANE, LANE), F32)],
        compiler_params=_params("parallel", "arbitrary"),
        name="rwkv_chunks",
    )(*args)


def _gmlp_kernel(gugv_ref, lng_ref, lnb_ref, ws_ref, bs_ref, o_ref, *vn_out):
    for blk in range(o_ref.shape[0] // GMLP_CHUNK):
        rows = slice(blk * GMLP_CHUNK, (blk + 1) * GMLP_CHUNK)
        u = _gelu_tanh(gugv_ref[rows, :GMLP_W].astype(F32))
        gv = _gelu_tanh(gugv_ref[rows, GMLP_W:].astype(F32))
        mu = jnp.mean(gv, axis=-1, keepdims=True)
        d = gv - mu
        var = jnp.mean(d * d, axis=-1, keepdims=True)
        vn = d * lax.rsqrt(var + LN_EPS) * lng_ref[...] + lnb_ref[...]
        if vn_out:
            vn_out[0][rows, :] = vn
        vb = vn.astype(BF16)
        for g in range(GMLP_GROUPS):
            gs = slice(g * GMLP_GC, (g + 1) * GMLP_GC)
            s = _dot(ws_ref[g], vb[:, gs]) + bs_ref[:, g:g + 1]
            o_ref[rows, gs] = (u[:, gs] * s).astype(o_ref.dtype)


def _gmlp(gugv, lw, w_eff, b_eff, emit_vn):
    m = gugv.shape[0]
    tr = _tile(m, 4 * GMLP_CHUNK)
    assert tr % GMLP_CHUNK == 0
    row_spec = pl.BlockSpec((tr, GMLP_W), lambda i: (i, 0))
    out_shape = [jax.ShapeDtypeStruct((m, GMLP_W), BF16)]
    out_specs = [row_spec]
    if emit_vn:
        out_shape.append(jax.ShapeDtypeStruct((m, GMLP_W), F32))
        out_specs.append(row_spec)
    res = pl.pallas_call(
        _gmlp_kernel,
        grid=(m // tr,),
        in_specs=[pl.BlockSpec((tr, 2 * GMLP_W), lambda i: (i, 0)),
                  pl.BlockSpec((1, GMLP_W), lambda i: (0, 0)),
                  pl.BlockSpec((1, GMLP_W), lambda i: (0, 0)),
                  pl.BlockSpec((GMLP_GROUPS, GMLP_CHUNK, GMLP_CHUNK), lambda i: (0, 0, 0)),
                  pl.BlockSpec((GMLP_CHUNK, GMLP_GROUPS), lambda i: (0, 0))],
        out_specs=out_specs,
        out_shape=out_shape,
        compiler_params=_params("parallel"),
        name="gmlp_sgu",
    )(gugv, lw["ln_g"], lw["ln_b"], w_eff, b_eff)
    return res if emit_vn else (res[0], None)


def _merge_kernel(oa_ref, ob_ref, oc_ref, ga_ref, gb_ref, gc_ref, wa_ref, wb_ref, wc_ref, o_ref):
    m = _sigmoid(ga_ref[...].astype(F32)) * _dot(oa_ref[...], wa_ref[...])
    m = m + _sigmoid(gb_ref[...].astype(F32)) * _dot(ob_ref[...], wb_ref[...])
    m = m + _sigmoid(gc_ref[...].astype(F32)) * _dot(oc_ref[...], wc_ref[...])
    o_ref[...] = m.astype(o_ref.dtype)


def _merge(o_a, o_b, o_c, gates, w_a, w_b, w_c, layer):
    m, kw = o_a.shape
    d = w_a.shape[2]
    tm = _tile(m, 512)
    tn = _col_tile(d, 1024)
    nj = d // tn
    o_spec = pl.BlockSpec((tm, kw), lambda j, i: (i, 0))
    w_spec = pl.BlockSpec((None, kw, tn), lambda j, i: (layer, 0, j))
    return pl.pallas_call(
        _merge_kernel,
        grid=(nj, m // tm),
        in_specs=[o_spec, o_spec, o_spec,
                  pl.BlockSpec((tm, tn), lambda j, i: (i, j)),
                  pl.BlockSpec((tm, tn), lambda j, i: (i, nj + j)),
                  pl.BlockSpec((tm, tn), lambda j, i: (i, 2 * nj + j)),
                  w_spec, w_spec, w_spec],
        out_specs=pl.BlockSpec((tm, tn), lambda j, i: (i, j)),
        out_shape=jax.ShapeDtypeStruct((m, d), BF16),
        compiler_params=_params("parallel", "parallel"),
        name="branch_merge",
    )(o_a, o_b, o_c, gates, gates, gates, w_a, w_b, w_c)


def _out_proj_kernel(m_ref, w_ref, x_ref, g_ref, o_ref):
    o_ref[...] = x_ref[...] + _rms(_dot(m_ref[...], w_ref[...]), g_ref[...])


def _out_proj(mm, w_out, layer, x, g):
    m, d = x.shape
    tm = _tile(m, 512)
    return pl.pallas_call(
        _out_proj_kernel,
        grid=(m // tm,),
        in_specs=[pl.BlockSpec((tm, d), lambda i: (i, 0)),
                  pl.BlockSpec((None, d, d), lambda i: (layer, 0, 0)),
                  pl.BlockSpec((tm, d), lambda i: (i, 0)),
                  pl.BlockSpec((1, d), lambda i: (0, 0))],
        out_specs=pl.BlockSpec((tm, d), lambda i: (i, 0)),
        out_shape=jax.ShapeDtypeStruct((m, d), F32),
        compiler_params=_params("parallel"),
        name="out_proj_residual",
    )(mm, w_out, x, g.reshape(1, d))


def _ffn_kernel(x_ref, gpre_ref, gpost_ref, gnext_ref, w1_ref, w3_ref, w2_ref, o_ref, hn_ref, h_ref, acc_ref):
    j = pl.program_id(1)

    @pl.when(j == 0)
    def _():
        h_ref[...] = _rms(x_ref[...], gpre_ref[...]).astype(BF16)
        acc_ref[...] = jnp.zeros_like(acc_ref)

    h = h_ref[...]
    a = _dot(h, w1_ref[...])
    b = _dot(h, w3_ref[...])
    gated = (a * _sigmoid(a)) * b
    acc_ref[...] += _dot(gated.astype(BF16), w2_ref[...])

    @pl.when(j == pl.num_programs(1) - 1)
    def _():
        x_new = x_ref[...] + _rms(acc_ref[...], gpost_ref[...])
        o_ref[...] = x_new
        hn_ref[...] = _rms(x_new, gnext_ref[...]).astype(hn_ref.dtype)


def _ffn(x, g_pre, g_post, g_next, w1, w3, w2, layer):
    m, d = x.shape
    dff = w1.shape[2]
    tm = _tile(m, 512)
    tf = _col_tile(dff, 512)
    vec = pl.BlockSpec((1, d), lambda i, j: (0, 0))
    row = pl.BlockSpec((tm, d), lambda i, j: (i, 0))
    return pl.pallas_call(
        _ffn_kernel,
        grid=(m // tm, dff // tf),
        in_specs=[row, vec, vec, vec,
                  pl.BlockSpec((None, d, tf), lambda i, j: (layer, 0, j)),
                  pl.BlockSpec((None, d, tf), lambda i, j: (layer, 0, j)),
                  pl.BlockSpec((None, tf, d), lambda i, j: (layer, j, 0))],
        out_specs=[row, row],
        out_shape=[jax.ShapeDtypeStruct((m, d), F32), jax.ShapeDtypeStruct((m, d), BF16)],
        scratch_shapes=[pltpu.VMEM((tm, d), BF16), pltpu.VMEM((tm, d), F32)],
        compiler_params=_params("parallel", "arbitrary"),
        name="ffn_swiglu",
    )(x, g_pre.reshape(1, d), g_post.reshape(1, d), g_next.reshape(1, d), w1, w3, w2)


def _pad_rows(w, rows, offset):
    out = jnp.zeros((rows, w.shape[1]), w.dtype)
    return out.at[offset:offset + w.shape[0]].set(w)


_O_FF = 3 * FOX_W
_O_RWKV = _O_FF + FOX_HEADS
_O_GMLP = _O_RWKV + RWKV_IN_W
_O_GATE = _O_GMLP + 2 * GMLP_W


def _stacked_weights(w_in, w_br_fox, w_br_rwkv, w_br_gmlp, w_out, ffn_w1, ffn_w3, ffn_w2):
    return {
        "wq": w_in[:, :, :FOX_W].astype(BF16),
        "wk": w_in[:, :, FOX_W:2 * FOX_W].astype(BF16),
        "wv": w_in[:, :, 2 * FOX_W:_O_FF].astype(BF16),
        "w_rwkv": w_in[:, :, _O_RWKV:_O_GMLP].astype(BF16),
        "w_gmlp": w_in[:, :, _O_GMLP:_O_GATE].astype(BF16),
        "w_gate": w_in[:, :, _O_GATE:].astype(BF16),
        "w_br_fox": w_br_fox.astype(BF16),
        "w_br_rwkv": w_br_rwkv.astype(BF16),
        "w_br_gmlp": w_br_gmlp.astype(BF16),
        "w_out": w_out.astype(BF16),
        "ffn_w1": ffn_w1.astype(BF16),
        "ffn_w3": ffn_w3.astype(BF16),
        "ffn_w2": ffn_w2.astype(BF16),
    }


def _layer_weights(l, d, w_in, fox_bf, rwkv_mu, rwkv_w0, rwkv_w_up, rwkv_a0, rwkv_a_up, rwkv_g_up, rwkv_k_k,
                   rwkv_k_a, rwkv_r_k, rwkv_gn_g, rwkv_gn_b, gmlp_ln_g, gmlp_ln_b, gmlp_w_s, gmlp_b_s, s_new):
    tril_full = jnp.tril(gmlp_w_s[l])
    blk = jnp.tril(gmlp_w_s[l][:, :s_new, :s_new])
    reps = GMLP_CHUNK // s_new
    eye = jnp.eye(reps, dtype=F32)
    w_sample = jnp.einsum("ab,gij->gaibj", eye, blk).reshape(GMLP_GROUPS, GMLP_CHUNK, GMLP_CHUNK)
    return {
        "w_ff": jnp.zeros((d, LANE), BF16).at[:, :FOX_HEADS].set(w_in[l, :, _O_FF:_O_RWKV].astype(BF16)),
        "b_ff": jnp.zeros((1, LANE), F32).at[0, :FOX_HEADS].set(fox_bf[l]),
        "rwkv": {
            "mu": rwkv_mu[l].reshape(1, RWKV_IN_W),
            "w0": rwkv_w0[l].reshape(1, RWKV_W),
            "a0": rwkv_a0[l].reshape(1, RWKV_W),
            "k_k": rwkv_k_k[l].reshape(1, RWKV_W),
            "k_a": rwkv_k_a[l].reshape(1, RWKV_W),
            "r_k": rwkv_r_k[l].reshape(1, RWKV_W),
            "gn_g": rwkv_gn_g[l].reshape(1, RWKV_W),
            "gn_b": rwkv_gn_b[l].reshape(1, RWKV_W),
            "w_up": _pad_rows(rwkv_w_up[l], LANE, 0).astype(BF16),
            "a_up": _pad_rows(rwkv_a_up[l], LANE, rwkv_w_up.shape[1]).astype(BF16),
            "g_up": rwkv_g_up[l].astype(BF16),
        },
        "gmlp": {"ln_g": gmlp_ln_g[l].reshape(1, GMLP_W), "ln_b": gmlp_ln_b[l].reshape(1, GMLP_W)},
        "gmlp_w_prompt": tril_full.astype(BF16),
        "gmlp_b_prompt": gmlp_b_s[l].T,
        "gmlp_w_sample": w_sample.astype(BF16),
        "gmlp_b_sample": jnp.tile(gmlp_b_s[l][:, :s_new].T, (reps, 1)),
    }


def _trunk_layer(x, h, big, lw, post_mix_g, pre_ffn_g, post_ffn_g, next_mix_g, nseq, t, layer, cache):
    m, d = x.shape
    q = _matmul(h, big["wq"], layer, (BF16,), scale=FOX_Q_SCALE, tn_pref=1024)
    k, k_bf = _kv_proj(h, big["wk"], layer)
    v, v_bf = _kv_proj(h, big["wv"], layer)
    rw = _matmul(h, big["w_rwkv"], layer, tn_pref=1664)
    gugv = _matmul(h, big["w_gmlp"], layer, (BF16,), tn_pref=1024)
    gates = _matmul(h, big["w_gate"], layer, (BF16,), tn_pref=1536)

    if cache is None:
        logf, xq, xk = _fox_forget(h, lw["w_ff"], lw["b_ff"], nseq, t)
        logf = logf[:, :FOX_HEADS]
        o_a = _fox_prompt(q, xq, k_bf, xk, v_bf, nseq, t)
        prev = jnp.zeros((nseq, 1, RWKV_IN_W), F32)
        state = None
        chunk = 64
        gm_w, gm_b = lw["gmlp_w_prompt"], lw["gmlp_b_prompt"]
    else:
        past = cache["k"].shape[2]
        total = past + t
        tpad = -(-total // LANE) * LANE
        logf = _forget_logits(h, lw["w_ff"], lw["b_ff"])[:, :FOX_HEADS]
        logf_rows = logf.reshape(nseq, t, FOX_HEADS).transpose(0, 2, 1)
        lf_all = jnp.concatenate([cache["logf"][layer].astype(F32).transpose(0, 2, 1), logf_rows], axis=-1)
        lf_all = jnp.pad(lf_all, ((0, 0), (0, 0), (0, tpad - total)))
        c_rows = _cumsum_lanes(lf_all.reshape(nseq * FOX_HEADS, tpad), LANE, LOG2E).reshape(nseq, FOX_HEADS, tpad)
        c_tm = c_rows[:, :, past:total].transpose(0, 2, 1).reshape(m, FOX_HEADS)
        o_a = _fox_sample(q, k_bf, v_bf, cache["k"], cache["v"], layer, c_tm, c_rows, nseq, t, past)
        prev = cache["shift"][layer]
        state = cache["state"]
        chunk = t
        gm_w, gm_b = lw["gmlp_w_sample"], lw["gmlp_b_sample"]

    r, lwd, k2, v2, kk, lr, g = _rwkv_prep(rw, prev, lw["rwkv"], nseq, t)
    o_b, s_new = _rwkv_chunks(r, lwd, k2, v2, kk, lr, g, lw["rwkv"], state, layer, nseq, t, chunk)
    shift_new = rw.reshape(nseq, t, RWKV_IN_W)[:, t - 1:t, :]

    o_c, vn = _gmlp(gugv, lw["gmlp"], gm_w, gm_b, cache is not None)

    mm = _merge(o_a, o_b, o_c, gates, big["w_br_fox"], big["w_br_rwkv"], big["w_br_gmlp"], layer)
    x = _out_proj(mm, big["w_out"], layer, x, post_mix_g)
    x, h_next = _ffn(x, pre_ffn_g, post_ffn_g, next_mix_g, big["ffn_w1"], big["ffn_w3"], big["ffn_w2"], layer)
    return x, h_next, (k, v, logf, s_new, shift_new, vn)


def kernel(x_prompt, x_sample, cache_fox_k, cache_fox_v, cache_fox_logf, state_rwkv, state_rwkv_shift, pre_mix_g, w_in, fox_bf, rwkv_mu, rwkv_w0, rwkv_w_up, rwkv_a0, rwkv_a_up, rwkv_g_up, rwkv_k_k, rwkv_k_a, rwkv_r_k, rwkv_gn_g, rwkv_gn_b, gmlp_ln_g, gmlp_ln_b, gmlp_w_s, gmlp_b_s, w_br_fox, w_br_rwkv, w_br_gmlp, w_out, post_mix_g, pre_ffn_g, ffn_w1, ffn_w3, ffn_w2, post_ffn_g):
    batch, seq, d = x_prompt.shape
    dec_batch, dec_seq, _ = x_sample.shape
    depth = w_in.shape[0]
    assert rwkv_w_up.shape[1] + rwkv_a_up.shape[1] == LANE and rwkv_g_up.shape[1] == LANE
    assert seq % GMLP_CHUNK == 0 and GMLP_CHUNK % dec_seq == 0 and (dec_batch * dec_seq) % GMLP_CHUNK == 0

    xp = x_prompt.reshape(batch * seq, d)
    xs = x_sample.reshape(dec_batch * dec_seq, d)
    cache = {"k": cache_fox_k, "v": cache_fox_v, "logf": cache_fox_logf,
             "state": state_rwkv, "shift": state_rwkv_shift}
    big = _stacked_weights(w_in, w_br_fox, w_br_rwkv, w_br_gmlp, w_out, ffn_w1, ffn_w3, ffn_w2)
    outs_p = [[] for _ in range(5)]
    outs_s = [[] for _ in range(6)]
    hp = _rmsnorm_cast(xp, pre_mix_g[0])
    hs = _rmsnorm_cast(xs, pre_mix_g[0])
    for l in range(depth):
        lw = _layer_weights(l, d, w_in, fox_bf, rwkv_mu, rwkv_w0, rwkv_w_up, rwkv_a0, rwkv_a_up, rwkv_g_up,
                            rwkv_k_k, rwkv_k_a, rwkv_r_k, rwkv_gn_g, rwkv_gn_b, gmlp_ln_g, gmlp_ln_b, gmlp_w_s,
                            gmlp_b_s, dec_seq)
        norms = (post_mix_g[l], pre_ffn_g[l], post_ffn_g[l], pre_mix_g[(l + 1) % depth])
        xp, hp, (k_p, v_p, lf_p, s_p, sh_p, _) = _trunk_layer(xp, hp, big, lw, *norms, batch, seq, l, None)
        xs, hs, (k_s, v_s, lf_s, s_s, sh_s, vn_s) = _trunk_layer(xs, hs, big, lw, *norms, dec_batch, dec_seq, l, cache)
        for acc, val in zip(outs_p, (k_p.reshape(batch, seq, FOX_HEADS, FOX_HD), v_p.reshape(batch, seq, FOX_HEADS, FOX_HD),
                                     lf_p.reshape(batch, seq, FOX_HEADS), s_p, sh_p)):
            acc.append(val)
        for acc, val in zip(outs_s, (k_s.reshape(dec_batch, dec_seq, FOX_HEADS, FOX_HD),
                                     v_s.reshape(dec_batch, dec_seq, FOX_HEADS, FOX_HD),
                                     lf_s.reshape(dec_batch, dec_seq, FOX_HEADS), s_s, sh_s,
                                     vn_s.reshape(dec_batch, dec_seq, GMLP_W))):
            acc.append(val)
    return (xp.reshape(batch, seq, d), xs.reshape(dec_batch, dec_seq, d),
            *[jnp.stack(o) for o in outs_p], *[jnp.stack(o) for o in outs_s])
```

```python
import functools
import math

import jax
import jax.numpy as jnp
from jax import lax
from jax.experimental import pallas as pl
from jax.experimental.pallas import tpu as pltpu

F32 = jnp.float32
BF16 = jnp.bfloat16

RMS_EPS = 1e-6
LN_EPS = 1e-5
RWKV_GN_EPS = 64e-5
NEG_INF = -1e30
LOG2E = math.log2(math.e)

LANE = 128
FOX_HEADS = 8
FOX_HD = 128
FOX_W = FOX_HEADS * FOX_HD
FOX_Q_SCALE = FOX_HD ** -0.5 * LOG2E
RWKV_HEADS = 16
RWKV_HD = 64
RWKV_W = RWKV_HEADS * RWKV_HD
RWKV_PAIRS = RWKV_W // LANE
RWKV_LORA_OFF = 3 * RWKV_W
RWKV_IN_W = 3 * RWKV_W + 2 * LANE
GMLP_W = 1024
GMLP_GROUPS = 8
GMLP_GC = GMLP_W // GMLP_GROUPS
GMLP_CHUNK = 128
N_BRANCH = 3

VMEM_LIMIT = 48 * 1024 * 1024


def _params(*sem):
    return pltpu.CompilerParams(dimension_semantics=sem, vmem_limit_bytes=VMEM_LIMIT)


def _tile(n, pref):
    if n <= pref:
        return n
    t = pref
    while n % t:
        t -= 8
    return t


def _col_tile(n, pref):
    assert n % LANE == 0
    k = n // LANE
    best = 1
    for d in range(1, k + 1):
        if k % d == 0 and d * LANE <= pref:
            best = d
    return best * LANE


def _split3(x):
    hi = x.astype(BF16)
    r = x - hi.astype(F32)
    mid = r.astype(BF16)
    lo = (r - mid.astype(F32)).astype(BF16)
    return hi, mid, lo


_NN = (((1,), (0,)), ((), ()))
_NT = (((1,), (1,)), ((), ()))
_TN = (((0,), (0,)), ((), ()))


def _dot(a, b, dims=_NN):
    return lax.dot_general(a, b, dims, preferred_element_type=F32)


def _dot_exact_rhs(x, e):
    hi, mid, lo = _split3(x)
    return _dot(hi, e) + _dot(mid, e) + _dot(lo, e)


def _sigmoid(x):
    return 1.0 / (1.0 + jnp.exp(-x))


def _log_sigmoid(x):
    return jnp.minimum(x, 0.0) - jnp.log(1.0 + jnp.exp(-jnp.abs(x)))


def _gelu_tanh(x):
    return 0.5 * x * (1.0 + jnp.tanh(0.7978845608028654 * (x + 0.044715 * (x * x * x))))


def _rms(x, g):
    ms = jnp.mean(x * x, axis=-1, keepdims=True)
    return x * lax.rsqrt(ms + RMS_EPS) * g


def _rmsnorm_cast_kernel(x_ref, g_ref, o_ref):
    o_ref[...] = _rms(x_ref[...], g_ref[...]).astype(o_ref.dtype)


def _rmsnorm_cast(x, g):
    m, d = x.shape
    tm = _tile(m, 512)
    return pl.pallas_call(
        _rmsnorm_cast_kernel,
        grid=(m // tm,),
        in_specs=[pl.BlockSpec((tm, d), lambda i: (i, 0)),
                  pl.BlockSpec((1, d), lambda i: (0, 0))],
        out_specs=pl.BlockSpec((tm, d), lambda i: (i, 0)),
        out_shape=jax.ShapeDtypeStruct((m, d), BF16),
        compiler_params=_params("parallel"),
        name="rmsnorm_cast",
    )(x, g.reshape(1, d))


def _matmul_kernel(a_ref, w_ref, *o_refs, scale):
    acc = _dot(a_ref[...], w_ref[...])
    if scale is not None:
        acc = acc * scale
    for o_ref in o_refs:
        o_ref[...] = acc.astype(o_ref.dtype)


def _matmul(a, w, layer, out_dtypes=(F32,), scale=None, tm_pref=1024, tn_pref=512):
    m, k = a.shape
    n = w.shape[2]
    tm = _tile(m, tm_pref)
    tn = _col_tile(n, tn_pref)
    o_spec = pl.BlockSpec((tm, tn), lambda i, j: (i, j))
    res = pl.pallas_call(
        functools.partial(_matmul_kernel, scale=scale),
        grid=(m // tm, n // tn),
        in_specs=[pl.BlockSpec((tm, k), lambda i, j: (i, 0)),
                  pl.BlockSpec((None, k, tn), lambda i, j: (layer, 0, j))],
        out_specs=[o_spec] * len(out_dtypes),
        out_shape=[jax.ShapeDtypeStruct((m, n), dt) for dt in out_dtypes],
        compiler_params=_params("parallel", "parallel"),
        name="proj_matmul",
    )(a, w)
    return res[0] if len(out_dtypes) == 1 else res


def _kv_proj_kernel(a_ref, w_ref, *refs):
    o_ref, obf_ref = refs[-2:]
    acc = _dot(a_ref[...], w_ref[...])
    obf_ref[...] = acc.astype(obf_ref.dtype)
    o_ref[...] = pltpu.einshape("t(hd)->thd", acc, h=FOX_HEADS)


def _kv_proj(a, w, layer, stacked=None):
    m, k = a.shape
    tm = _tile(m, 512)
    in_specs = [pl.BlockSpec((tm, k), lambda i: (i, 0)),
                pl.BlockSpec((None, k, FOX_W), lambda i: (layer, 0, 0))]
    args = [a, w]
    if stacked is None:
        o_spec = pl.BlockSpec((tm, FOX_HEADS, FOX_HD), lambda i: (i, 0, 0))
        o_shape = jax.ShapeDtypeStruct((m, FOX_HEADS, FOX_HD), F32)
        aliases = {}
    else:
        in_specs.append(pl.BlockSpec(memory_space=pl.ANY))
        args.append(stacked)
        o_spec = pl.BlockSpec((None, tm, FOX_HEADS, FOX_HD), lambda i: (layer, i, 0, 0))
        o_shape = jax.ShapeDtypeStruct(stacked.shape, F32)
        aliases = {2: 0}
    return pl.pallas_call(
        _kv_proj_kernel,
        grid=(m // tm,),
        in_specs=in_specs,
        out_specs=[o_spec, pl.BlockSpec((tm, FOX_W), lambda i: (i, 0))],
        out_shape=[o_shape, jax.ShapeDtypeStruct((m, FOX_W), BF16)],
        input_output_aliases=aliases,
        compiler_params=_params("parallel"),
        name="kv_proj",
    )(*args)


def _logf_kernel(a_ref, w_ref, b_ref, o_ref):
    o_ref[...] = _log_sigmoid(_dot(a_ref[...], w_ref[...]) + b_ref[...])


def _forget_logits(h, w_ff, b_ff):
    m, k = h.shape
    tm = _tile(m, 1024)
    return pl.pallas_call(
        _logf_kernel,
        grid=(m // tm,),
        in_specs=[pl.BlockSpec((tm, k), lambda i: (i, 0)),
                  pl.BlockSpec((k, LANE), lambda i: (0, 0)),
                  pl.BlockSpec((1, LANE), lambda i: (0, 0))],
        out_specs=pl.BlockSpec((tm, LANE), lambda i: (i, 0)),
        out_shape=jax.ShapeDtypeStruct((m, LANE), F32),
        compiler_params=_params("parallel"),
        name="forget_logits",
    )(h, w_ff, b_ff)


def _cumsum_kernel(x_ref, o_ref, carry_ref, *, tb, scale):
    @pl.when(pl.program_id(0) == 0)
    def _():
        carry_ref[...] = jnp.zeros_like(carry_ref)

    row = lax.broadcasted_iota(jnp.int32, (tb, tb), 0)
    col = lax.broadcasted_iota(jnp.int32, (tb, tb), 1)
    upper = jnp.where(row <= col, 1.0, 0.0).astype(BF16)
    c = _dot_exact_rhs(x_ref[...], upper) + carry_ref[:, :1]
    o_ref[...] = c * scale
    carry_ref[...] = jnp.broadcast_to(c[:, tb - 1:tb], carry_ref.shape)


def _cumsum_lanes(x, tb, scale):
    r, t = x.shape
    return pl.pallas_call(
        functools.partial(_cumsum_kernel, tb=tb, scale=scale),
        grid=(t // tb,),
        in_specs=[pl.BlockSpec((r, tb), lambda i: (0, i))],
        out_specs=pl.BlockSpec((r, tb), lambda i: (0, i)),
        out_shape=jax.ShapeDtypeStruct((r, t), F32),
        scratch_shapes=[pltpu.VMEM((r, LANE), F32)],
        compiler_params=_params("arbitrary"),
        name="logf_cumsum",
    )(x)


def _fox_bias_placement():
    part = jnp.arange(3 * LANE) // LANE
    head = jnp.arange(3 * LANE) % LANE
    out_head = jnp.arange(FOX_W) // FOX_HD
    out_lane = jnp.arange(FOX_W) % FOX_HD
    same_head = (head[:, None] == out_head[None, :]) & (head[:, None] < FOX_HEADS)
    pq = jnp.where(same_head & (out_lane[None, :] == part[:, None]), 1.0, 0.0).astype(BF16)
    pk = jnp.where(same_head & (out_lane[None, :] == part[:, None] + 3), -1.0, 0.0).astype(BF16)
    one_q = jnp.where((out_lane >= 3) & (out_lane < 6), 1.0, 0.0).reshape(1, FOX_W)
    one_k = jnp.where(out_lane < 3, 1.0, 0.0).reshape(1, FOX_W)
    return pq, pk, one_q, one_k


def _fox_forget_kernel(h_ref, w_ref, b_ref, pq_ref, pk_ref, oneq_ref, onek_ref, logf_ref, xq_ref, xk_ref, carry_ref,
                       *, tb):
    @pl.when(pl.program_id(1) == 0)
    def _():
        carry_ref[...] = jnp.zeros_like(carry_ref)

    logf = _log_sigmoid(_dot(h_ref[...], w_ref[...]) + b_ref[...])
    logf_ref[...] = logf
    row = lax.broadcasted_iota(jnp.int32, (tb, tb), 0)
    col = lax.broadcasted_iota(jnp.int32, (tb, tb), 1)
    lower = jnp.where(col <= row, 1.0, 0.0).astype(BF16)
    hi, mid, lo = _split3(logf)
    c = _dot(lower, hi) + _dot(lower, mid) + _dot(lower, lo) + carry_ref[:1, :]
    carry_ref[...] = jnp.broadcast_to(c[tb - 1:tb, :], carry_ref.shape)
    parts = jnp.concatenate(_split3(c * LOG2E), axis=1)
    xq_ref[...] = (_dot(parts, pq_ref[...]) + oneq_ref[...]).astype(BF16)
    xk_ref[...] = (_dot(parts, pk_ref[...]) + onek_ref[...]).astype(BF16)


def _fox_forget(h, w_ff, b_ff, nseq, t):
    m, k = h.shape
    tb = _tile(t, 256)
    nb = t // tb
    rows = lambda s, i: (s * nb + i, 0)
    const = lambda s, i: (0, 0)
    return pl.pallas_call(
        functools.partial(_fox_forget_kernel, tb=tb),
        grid=(nseq, nb),
        in_specs=[pl.BlockSpec((tb, k), rows),
                  pl.BlockSpec((k, LANE), const),
                  pl.BlockSpec((1, LANE), const),
                  pl.BlockSpec((3 * LANE, FOX_W), const),
                  pl.BlockSpec((3 * LANE, FOX_W), const),
                  pl.BlockSpec((1, FOX_W), const),
                  pl.BlockSpec((1, FOX_W), const)],
        out_specs=[pl.BlockSpec((tb, LANE), rows), pl.BlockSpec((tb, FOX_W), rows), pl.BlockSpec((tb, FOX_W), rows)],
        out_shape=[jax.ShapeDtypeStruct((m, LANE), F32),
                   jax.ShapeDtypeStruct((m, FOX_W), BF16),
                   jax.ShapeDtypeStruct((m, FOX_W), BF16)],
        scratch_shapes=[pltpu.VMEM((8, LANE), F32)],
        compiler_params=_params("parallel", "arbitrary"),
        name="fox_forget",
    )(h, w_ff, b_ff, *_fox_bias_placement())


FOX_KEYS_FULL, FOX_KEYS_DIAG_HALF, FOX_KEYS_FULL_DIAG = 0, 1, 2
FOX_HEAD_GROUP = 4


def _fox_prompt_kernel(qi_ref, kj_ref, kind_ref, q_ref, xq_ref, k_ref, xk_ref, v_ref, o_ref, m_ref, l_ref, acc_ref,
                       *, tq):
    step = pl.program_id(1)
    kj = kj_ref[step]
    kind = kind_ref[step]

    @pl.when(kj == 0)
    def _():
        m_ref[...] = jnp.full_like(m_ref, NEG_INF)
        l_ref[...] = jnp.zeros_like(l_ref)
        acc_ref[...] = jnp.zeros_like(acc_ref)

    def update(keys_kind):
        nk = tq if keys_kind == FOX_KEYS_DIAG_HALF else 2 * tq
        keys = slice(0, nk)
        if keys_kind != FOX_KEYS_FULL:
            row = lax.broadcasted_iota(jnp.int32, (tq, nk), 0)
            col = lax.broadcasted_iota(jnp.int32, (tq, nk), 1)
            causal = col <= row + (nk - tq)
        ones = jnp.ones((nk, LANE), BF16)
        for g0 in range(0, FOX_HEADS, FOX_HEAD_GROUP):
            heads = range(g0, g0 + FOX_HEAD_GROUP)
            cols = [slice(h * FOX_HD, (h + 1) * FOX_HD) for h in heads]
            s = [_dot(jnp.concatenate([q_ref[:, hs], xq_ref[:, hs]], axis=1),
                      jnp.concatenate([k_ref[keys, hs], xk_ref[keys, hs]], axis=1), _NT) for hs in cols]
            if keys_kind != FOX_KEYS_FULL:
                s = [jnp.where(causal, x, NEG_INF) for x in s]
            m_prev = [m_ref[h] for h in heads]
            m_new = [jnp.maximum(mp, jnp.max(x, axis=-1, keepdims=True)) for mp, x in zip(m_prev, s)]
            p = [jnp.exp2((x - jnp.concatenate([mn] * (nk // LANE), axis=1)).astype(BF16)) for x, mn in zip(s, m_new)]
            pv1 = [_dot(x, jnp.concatenate([v_ref[keys, hs], ones], axis=1)) for x, hs in zip(p, cols)]
            for i, h in enumerate(heads):
                alpha = jnp.exp2(m_prev[i] - m_new[i])
                l_ref[h] = alpha * l_ref[h] + pv1[i][:, LANE:]
                acc_ref[:, cols[i]] = alpha * acc_ref[:, cols[i]] + pv1[i][:, :LANE]
                m_ref[h] = m_new[i]

    def finish():
        for h in range(FOX_HEADS):
            hs = slice(h * FOX_HD, (h + 1) * FOX_HD)
            o_ref[:, hs] = (acc_ref[:, hs] / l_ref[h]).astype(o_ref.dtype)

    @pl.when(kind == FOX_KEYS_FULL)
    def _():
        update(FOX_KEYS_FULL)

    @pl.when(kind == FOX_KEYS_DIAG_HALF)
    def _():
        update(FOX_KEYS_DIAG_HALF)
        finish()

    @pl.when(kind == FOX_KEYS_FULL_DIAG)
    def _():
        update(FOX_KEYS_FULL_DIAG)
        finish()


def _fox_prompt(q, xq, k, xk, v, batch, seq):
    tq = _tile(seq // 2, 512)
    assert tq % LANE == 0 and seq % (2 * tq) == 0
    nq = seq // tq
    blocks = []
    for qi in range(nq):
        last = qi // 2
        blocks += [(qi, kj, FOX_KEYS_FULL) for kj in range(last)]
        blocks.append((qi, last, FOX_KEYS_DIAG_HALF if qi % 2 == 0 else FOX_KEYS_FULL_DIAG))
    tabs = [jnp.asarray([b[i] for b in blocks], jnp.int32) for i in range(3)]
    q_map = lambda b, s, qt, kt, kinds: (b * nq + qt[s], 0)
    kv_map = lambda b, s, qt, kt, kinds: (b * (nq // 2) + kt[s], 0)
    return pl.pallas_call(
        functools.partial(_fox_prompt_kernel, tq=tq),
        grid_spec=pltpu.PrefetchScalarGridSpec(
            num_scalar_prefetch=3,
            grid=(batch, len(blocks)),
            in_specs=[pl.BlockSpec((tq, FOX_W), q_map),
                      pl.BlockSpec((tq, FOX_W), q_map),
                      pl.BlockSpec((2 * tq, FOX_W), kv_map),
                      pl.BlockSpec((2 * tq, FOX_W), kv_map),
                      pl.BlockSpec((2 * tq, FOX_W), kv_map)],
            out_specs=pl.BlockSpec((tq, FOX_W), q_map),
            scratch_shapes=[pltpu.VMEM((FOX_HEADS, tq, LANE), F32),
                            pltpu.VMEM((FOX_HEADS, tq, LANE), F32),
                            pltpu.VMEM((tq, FOX_W), F32)]),
        out_shape=jax.ShapeDtypeStruct((batch * seq, FOX_W), BF16),
        compiler_params=_params("parallel", "arbitrary"),
        name="fox_prompt",
    )(*tabs, q, xq, k, xk, v)


def _fox_sample_kernel(q_ref, kn_ref, vn_ref, cq_ref, ck_ref, kc_ref, vc_ref, o_ref, *, past, s_new):
    kc = pltpu.einshape("thd->htd", kc_ref[0, 0])
    vc = pltpu.einshape("thd->htd", vc_ref[0, 0])
    row = lax.broadcasted_iota(jnp.int32, (s_new, s_new), 0)
    col = lax.broadcasted_iota(jnp.int32, (s_new, s_new), 1)
    causal = col <= row
    heads = range(FOX_HEADS)
    cols = [slice(h * FOX_HD, (h + 1) * FOX_HD) for h in heads]
    cq = [cq_ref[:, h:h + 1] for h in heads]
    s_c = [_dot(q_ref[:, hs], kc[h].astype(BF16), _NT) + (cq[h] - ck_ref[0, h:h + 1, :past])
           for h, hs in zip(heads, cols)]
    s_n = [jnp.where(causal, _dot(q_ref[:, hs], kn_ref[:, hs], _NT)
                     + (cq[h] - ck_ref[0, h:h + 1, past:past + s_new]), NEG_INF) for h, hs in zip(heads, cols)]
    m = [jnp.maximum(jnp.max(x, axis=-1, keepdims=True), jnp.max(y, axis=-1, keepdims=True)) for x, y in zip(s_c, s_n)]
    p_c = [jnp.exp2(x - mx) for x, mx in zip(s_c, m)]
    p_n = [jnp.exp2(x - mx) for x, mx in zip(s_n, m)]
    l = [jnp.sum(x, axis=-1, keepdims=True) + jnp.sum(y, axis=-1, keepdims=True) for x, y in zip(p_c, p_n)]
    o = [_dot(x.astype(BF16), vc[h].astype(BF16)) + _dot(y.astype(BF16), vn_ref[:, hs])
         for h, hs, x, y in zip(heads, cols, p_c, p_n)]
    for h in heads:
        o_ref[:, cols[h]] = (o[h] / l[h]).astype(o_ref.dtype)


def _fox_sample(q, k, v, k_cache, v_cache, layer, c_tm, c_rows, batch, s_new, past):
    tpad = c_rows.shape[-1]
    new_spec = pl.BlockSpec((s_new, FOX_W), lambda b: (b, 0))
    cache_spec = pl.BlockSpec((1, 1, past, FOX_HEADS, FOX_HD), lambda b: (layer, b, 0, 0, 0))
    return pl.pallas_call(
        functools.partial(_fox_sample_kernel, past=past, s_new=s_new),
        grid=(batch,),
        in_specs=[new_spec, new_spec, new_spec,
                  pl.BlockSpec((s_new, FOX_HEADS), lambda b: (b, 0)),
                  pl.BlockSpec((1, FOX_HEADS, tpad), lambda b: (b, 0, 0)), cache_spec, cache_spec],
        out_specs=new_spec,
        out_shape=jax.ShapeDtypeStruct((batch * s_new, FOX_W), BF16),
        compiler_params=_params("parallel"),
        name="fox_sample",
    )(q, k, v, c_tm, c_rows, k_cache, v_cache)


def _rwkv_prep_kernel(p_ref, prev_ref, mu_ref, w0_ref, a0_ref, kk_ref, ka_ref, wup_ref, aup_ref, gup_ref,
                      r_ref, lw_ref, k_ref, v_ref, kk_out_ref, a_ref, g_ref, carry_ref, *, tr):
    @pl.when(pl.program_id(1) == 0)
    def _():
        carry_ref[...] = prev_ref[0]

    p = p_ref[0]
    rows = lax.broadcasted_iota(jnp.int32, (tr, 1), 0)
    p_prev = jnp.where(rows == 0, carry_ref[...], pltpu.roll(p, shift=1, axis=0))
    carry_ref[...] = p[tr - 1:tr, :]
    xs = p + (p_prev - p) * mu_ref[...]

    r = xs[:, :RWKV_W]
    k = xs[:, RWKV_W:2 * RWKV_W]
    v = xs[:, 2 * RWKV_W:3 * RWKV_W]
    d_wa = xs[:, RWKV_LORA_OFF:RWKV_LORA_OFF + LANE]
    d_g = xs[:, RWKV_LORA_OFF + LANE:RWKV_LORA_OFF + 2 * LANE]

    z_w = w0_ref[...] + _dot(jnp.tanh(d_wa).astype(BF16), wup_ref[...])
    log_decay = -jnp.exp(_log_sigmoid(z_w) - 0.5)
    a = _sigmoid(a0_ref[...] + _dot(d_wa.astype(BF16), aup_ref[...]))
    g = _dot(_sigmoid(d_g).astype(BF16), gup_ref[...])

    r_ref[...] = r.astype(r_ref.dtype)
    lw_ref[...] = log_decay
    k_ref[...] = (k * (1.0 + (a - 1.0) * ka_ref[...])).astype(k_ref.dtype)
    v_ref[...] = v.astype(v_ref.dtype)
    kk_out_ref[...] = (k * kk_ref[...]).astype(kk_out_ref.dtype)
    a_ref[...] = a.astype(a_ref.dtype)
    g_ref[...] = g.astype(g_ref.dtype)


def _rwkv_prep(rw, prev, lw, nseq, t):
    tr = _tile(t, 256)
    nb = t // tr
    wide = lambda s, i: (0, 0)
    row_spec = pl.BlockSpec((tr, RWKV_W), lambda s, i: (s * nb + i, 0))
    out_dtypes = (BF16, F32, BF16, BF16, BF16, BF16, BF16)
    return pl.pallas_call(
        functools.partial(_rwkv_prep_kernel, tr=tr),
        grid=(nseq, nb),
        in_specs=[pl.BlockSpec((1, tr, RWKV_IN_W), lambda s, i: (s, i, 0)),
                  pl.BlockSpec((1, 1, RWKV_IN_W), lambda s, i: (s, 0, 0)),
                  pl.BlockSpec((1, RWKV_IN_W), wide),
                  pl.BlockSpec((1, RWKV_W), wide),
                  pl.BlockSpec((1, RWKV_W), wide),
                  pl.BlockSpec((1, RWKV_W), wide),
                  pl.BlockSpec((1, RWKV_W), wide),
                  pl.BlockSpec((LANE, RWKV_W), wide),
                  pl.BlockSpec((LANE, RWKV_W), wide),
                  pl.BlockSpec((LANE, RWKV_W), wide)],
        out_specs=[row_spec] * 7,
        out_shape=[jax.ShapeDtypeStruct((nseq * t, RWKV_W), dt) for dt in out_dtypes],
        scratch_shapes=[pltpu.VMEM((1, RWKV_IN_W), F32)],
        compiler_params=_params("parallel", "arbitrary"),
        name="rwkv_prep",
    )(rw.reshape(nseq, t, RWKV_IN_W), prev, lw["mu"], lw["w0"], lw["a0"], lw["k_k"], lw["k_a"],
      lw["w_up"], lw["a_up"], lw["g_up"])


RWKV_CHUNKS_PER_STEP = 2


def _rwkv_chunk_kernel(*refs, c, has_s0):
    r_ref, lw_ref, k_ref, v_ref, kk_ref, lr_ref, g_ref, gng_ref, gnb_ref, rk_ref = refs[:10]
    s0_ref = refs[10] if has_s0 else None
    o_ref, sout_ref, s_ref = refs[-3:]
    ci = pl.program_id(1)
    pairs = range(RWKV_PAIRS)

    @pl.when(ci == 0)
    def _():
        if has_s0:
            zero = jnp.zeros((RWKV_HD, RWKV_HD), F32)
            for pr in pairs:
                top = jnp.concatenate([s0_ref[0, 0, 2 * pr], zero], axis=1)
                bot = jnp.concatenate([zero, s0_ref[0, 0, 2 * pr + 1]], axis=1)
                s_ref[pr] = jnp.concatenate([top, bot], axis=0)
        else:
            s_ref[...] = jnp.zeros_like(s_ref)

    c2 = 2 * c
    row = lax.broadcasted_iota(jnp.int32, (c2, c2), 0)
    col = lax.broadcasted_iota(jnp.int32, (c2, c2), 1)
    lower = col <= row
    strict = col < row
    eye = jnp.where(row == col, 1.0, 0.0)
    rc = lax.broadcasted_iota(jnp.int32, (c, c), 0)
    cc = lax.broadcasted_iota(jnp.int32, (c, c), 1)
    tri = jnp.where(cc <= rc, 1.0, 0.0).astype(BF16)
    lane = lax.broadcasted_iota(jnp.int32, (c, LANE), 1)
    first = lane < RWKV_HD
    steps = max(c.bit_length() - 2, 0)

    def stack(x):
        return jnp.concatenate([jnp.where(first, x, 0.0), jnp.where(first, 0.0, x)], axis=0)

    own = (lax.broadcasted_iota(jnp.int32, (c2, LANE), 1) // RWKV_HD
           == lax.broadcasted_iota(jnp.int32, (c2, LANE), 0) // c)

    n_sub = r_ref.shape[0] // c
    lanes = [slice(pr * LANE, (pr + 1) * LANE) for pr in pairs]
    units = [(slice(sb * c, (sb + 1) * c), ls) for sb in range(n_sub) for ls in lanes]
    lws = [lw_ref[rows, ls] for rows, ls in units]
    cum3 = [_dot(tri, jnp.concatenate(_split3(lw), axis=1)) for lw in lws]
    cums = [x[:, :LANE] + x[:, LANE:2 * LANE] + x[:, 2 * LANE:] for x in cum3]
    totals = [cum[c - 1:c, :] for cum in cums]

    lhs, rhs, bar, v_s, rk_s = [], [], [], [], []
    for (rows, ls), lw, cum, total in zip(units, lws, cums, totals):
        grow = jnp.exp(-cum)
        rest = jnp.exp(total - cum)
        r = r_ref[rows, ls].astype(F32)
        k = k_ref[rows, ls].astype(F32)
        kk = kk_ref[rows, ls].astype(F32)
        norm = lax.rsqrt(jnp.maximum(jnp.sum(stack(kk * kk), axis=-1, keepdims=True), 1e-24))
        kk = kk * jnp.where(first, norm[:c], norm[c:])
        b = kk * lr_ref[rows, ls].astype(F32)
        lhs.append(jnp.concatenate([stack(-kk * jnp.exp(cum - lw)), stack(r * jnp.exp(cum))], axis=0).astype(BF16))
        rhs.append(jnp.concatenate([stack(k * grow), stack(b * grow)], axis=0).astype(BF16))
        bar.append(jnp.concatenate([stack(k * rest), stack(b * rest)], axis=0).astype(BF16))
        v_s.append(stack(v_ref[rows, ls].astype(F32)))
        rk_s.append(stack(r * k * rk_ref[:, ls]))

    amats = [_dot(x, y, _NT) for x, y in zip(lhs, rhs)]
    a_ab = [jnp.where(strict, m[:c2, c2:], 0.0) for m in amats]
    invs = [eye + n for n in a_ab]
    powers = [_dot(n.astype(BF16), n.astype(BF16)) for n in a_ab]
    for i in range(steps):
        if i < steps - 1:
            xs = [_dot(jnp.concatenate([inv, pw], axis=0).astype(BF16), pw.astype(BF16))
                  for inv, pw in zip(invs, powers)]
            invs = [inv + x[:c2] for inv, x in zip(invs, xs)]
            powers = [x[c2:] for x in xs]
        else:
            invs = [inv + _dot(inv.astype(BF16), pw.astype(BF16)) for inv, pw in zip(invs, powers)]

    akv = [_dot(jnp.where(strict, m[:c2, :c2], 0.0).astype(BF16), v.astype(BF16)) for m, v in zip(amats, v_s)]
    a_rs = [jnp.concatenate([jnp.where(lower, m[c2:, :c2], 0.0), jnp.where(lower, m[c2:, c2:], 0.0)],
                            axis=1).astype(BF16) for m in amats]

    s_mats = [s_ref[pr] for pr in pairs]
    for sb in range(n_sub):
        base = sb * RWKV_PAIRS
        sxs = [_dot(lhs[base + pr], s_mats[pr].astype(BF16), _NT) for pr in pairs]
        us = [_dot(invs[base + pr].astype(BF16), (sxs[pr][:c2] + akv[base + pr]).astype(BF16)) for pr in pairs]
        vus = [jnp.concatenate([v_s[base + pr], us[pr]], axis=0).astype(BF16) for pr in pairs]
        for pr in pairs:
            i = base + pr
            rows, ls = units[i]
            y_s = sxs[pr][c2:] + _dot(a_rs[i], vus[pr])
            s_mats[pr] = s_mats[pr] * jnp.exp(totals[i]) + _dot(vus[pr], bar[i], _TN)
            mu = jnp.sum(y_s, axis=-1, keepdims=True) * (1.0 / RWKV_HD)
            d = jnp.where(own, y_s - mu, 0.0)
            var = jnp.sum(d * d, axis=-1, keepdims=True) * (1.0 / RWKV_HD)
            yn_s = d * lax.rsqrt(var + RWKV_GN_EPS)
            bonus_s = jnp.sum(rk_s[i], axis=-1, keepdims=True) * v_s[i]
            yn = (yn_s[:c] + yn_s[c:]) * gng_ref[:, ls] + gnb_ref[:, ls]
            o_ref[rows, ls] = ((yn + (bonus_s[:c] + bonus_s[c:])) * g_ref[rows, ls].astype(F32)).astype(o_ref.dtype)
    for pr in pairs:
        s_ref[pr] = s_mats[pr]

    @pl.when(ci == pl.num_programs(1) - 1)
    def _():
        for pr in pairs:
            s_mat = s_ref[pr]
            sout_ref[0, 2 * pr] = s_mat[:RWKV_HD, :RWKV_HD]
            sout_ref[0, 2 * pr + 1] = s_mat[RWKV_HD:, RWKV_HD:]


def _rwkv_chunks(r, lwd, k, v, kk, lr, g, lw, state, layer, nseq, t, c):
    rows = c * RWKV_CHUNKS_PER_STEP if t % (c * RWKV_CHUNKS_PER_STEP) == 0 else c
    nc = t // rows
    row_spec = pl.BlockSpec((rows, RWKV_W), lambda s, i: (s * nc + i, 0))
    par_spec = pl.BlockSpec((1, RWKV_W), lambda s, i: (0, 0))
    in_specs = [row_spec] * 7 + [par_spec] * 3
    args = [r, lwd, k, v, kk, lr, g, lw["gn_g"], lw["gn_b"], lw["r_k"]]
    if state is not None:
        in_specs.append(pl.BlockSpec((1, 1, RWKV_HEADS, RWKV_HD, RWKV_HD), lambda s, i: (layer, s, 0, 0, 0)))
        args.append(state)
    return pl.pallas_call(
        functools.partial(_rwkv_chunk_kernel, c=c, has_s0=state is not None),
        grid=(nseq, nc),
        in_specs=in_specs,
        out_specs=[row_spec, pl.BlockSpec((1, RWKV_HEADS, RWKV_HD, RWKV_HD), lambda s, i: (s, 0, 0, 0))],
        out_shape=[jax.ShapeDtypeStruct((nseq * t, RWKV_W), BF16),
                   jax.ShapeDtypeStruct((nseq, RWKV_HEADS, RWKV_HD, RWKV_HD), F32)],
        scratch_shapes=[pltpu.VMEM((RWKV_PAIRS, LANE, LANE), F32)],
        compiler_params=_params("parallel", "arbitrary"),
        name="rwkv_chunks",
    )(*args)


def _gmlp_kernel(gugv_ref, lng_ref, lnb_ref, ws_ref, bs_ref, o_ref, *vn_out):
    for blk in range(o_ref.shape[0] // GMLP_CHUNK):
        rows = slice(blk * GMLP_CHUNK, (blk + 1) * GMLP_CHUNK)
        u = _gelu_tanh(gugv_ref[rows, :GMLP_W].astype(F32))
        gv = _gelu_tanh(gugv_ref[rows, GMLP_W:].astype(F32))
        mu = jnp.mean(gv, axis=-1, keepdims=True)
        d = gv - mu
        var = jnp.mean(d * d, axis=-1, keepdims=True)
        vn = d * lax.rsqrt(var + LN_EPS) * lng_ref[...] + lnb_ref[...]
        if vn_out:
            vn_out[0][rows, :] = vn
        vb = vn.astype(BF16)
        for g in range(GMLP_GROUPS):
            gs = slice(g * GMLP_GC, (g + 1) * GMLP_GC)
            s = _dot(ws_ref[g], vb[:, gs]) + bs_ref[:, g:g + 1]
            o_ref[rows, gs] = (u[:, gs] * s).astype(o_ref.dtype)


def _gmlp(gugv, lw, w_eff, b_eff, emit_vn):
    m = gugv.shape[0]
    tr = _tile(m, 4 * GMLP_CHUNK)
    assert tr % GMLP_CHUNK == 0
    row_spec = pl.BlockSpec((tr, GMLP_W), lambda i: (i, 0))
    out_shape = [jax.ShapeDtypeStruct((m, GMLP_W), BF16)]
    out_specs = [row_spec]
    if emit_vn:
        out_shape.append(jax.ShapeDtypeStruct((m, GMLP_W), F32))
        out_specs.append(row_spec)
    res = pl.pallas_call(
        _gmlp_kernel,
        grid=(m // tr,),
        in_specs=[pl.BlockSpec((tr, 2 * GMLP_W), lambda i: (i, 0)),
                  pl.BlockSpec((1, GMLP_W), lambda i: (0, 0)),
                  pl.BlockSpec((1, GMLP_W), lambda i: (0, 0)),
                  pl.BlockSpec((GMLP_GROUPS, GMLP_CHUNK, GMLP_CHUNK), lambda i: (0, 0, 0)),
                  pl.BlockSpec((GMLP_CHUNK, GMLP_GROUPS), lambda i: (0, 0))],
        out_specs=out_specs,
        out_shape=out_shape,
        compiler_params=_params("parallel"),
        name="gmlp_sgu",
    )(gugv, lw["ln_g"], lw["ln_b"], w_eff, b_eff)
    return res if emit_vn else (res[0], None)


def _merge_kernel(oa_ref, ob_ref, oc_ref, ga_ref, gb_ref, gc_ref, wa_ref, wb_ref, wc_ref, o_ref):
    m = _sigmoid(ga_ref[...].astype(F32)) * _dot(oa_ref[...], wa_ref[...])
    m = m + _sigmoid(gb_ref[...].astype(F32)) * _dot(ob_ref[...], wb_ref[...])
    m = m + _sigmoid(gc_ref[...].astype(F32)) * _dot(oc_ref[...], wc_ref[...])
    o_ref[...] = m.astype(o_ref.dtype)


def _merge(o_a, o_b, o_c, gates, w_a, w_b, w_c, layer):
    m, kw = o_a.shape
    d = w_a.shape[2]
    tm = _tile(m, 512)
    tn = _col_tile(d, 1024)
    nj = d // tn
    o_spec = pl.BlockSpec((tm, kw), lambda j, i: (i, 0))
    w_spec = pl.BlockSpec((None, kw, tn), lambda j, i: (layer, 0, j))
    return pl.pallas_call(
        _merge_kernel,
        grid=(nj, m // tm),
        in_specs=[o_spec, o_spec, o_spec,
                  pl.BlockSpec((tm, tn), lambda j, i: (i, j)),
                  pl.BlockSpec((tm, tn), lambda j, i: (i, nj + j)),
                  pl.BlockSpec((tm, tn), lambda j, i: (i, 2 * nj + j)),
                  w_spec, w_spec, w_spec],
        out_specs=pl.BlockSpec((tm, tn), lambda j, i: (i, j)),
        out_shape=jax.ShapeDtypeStruct((m, d), BF16),
        compiler_params=_params("parallel", "parallel"),
        name="branch_merge",
    )(o_a, o_b, o_c, gates, gates, gates, w_a, w_b, w_c)


def _out_proj_kernel(m_ref, w_ref, x_ref, g_ref, o_ref):
    o_ref[...] = x_ref[...] + _rms(_dot(m_ref[...], w_ref[...]), g_ref[...])


def _out_proj(mm, w_out, layer, x, g):
    m, d = x.shape
    tm = _tile(m, 512)
    return pl.pallas_call(
        _out_proj_kernel,
        grid=(m // tm,),
        in_specs=[pl.BlockSpec((tm, d), lambda i: (i, 0)),
                  pl.BlockSpec((None, d, d), lambda i: (layer, 0, 0)),
                  pl.BlockSpec((tm, d), lambda i: (i, 0)),
                  pl.BlockSpec((1, d), lambda i: (0, 0))],
        out_specs=pl.BlockSpec((tm, d), lambda i: (i, 0)),
        out_shape=jax.ShapeDtypeStruct((m, d), F32),
        compiler_params=_params("parallel"),
        name="out_proj_residual",
    )(mm, w_out, x, g.reshape(1, d))


def _ffn_kernel(x_ref, gpre_ref, gpost_ref, gnext_ref, w1_ref, w3_ref, w2_ref, o_ref, hn_ref, h_ref, acc_ref):
    j = pl.program_id(1)

    @pl.when(j == 0)
    def _():
        h_ref[...] = _rms(x_ref[...], gpre_ref[...]).astype(BF16)
        acc_ref[...] = jnp.zeros_like(acc_ref)

    h = h_ref[...]
    a = _dot(h, w1_ref[...])
    b = _dot(h, w3_ref[...])
    gated = (a * _sigmoid(a)) * b
    acc_ref[...] += _dot(gated.astype(BF16), w2_ref[...])

    @pl.when(j == pl.num_programs(1) - 1)
    def _():
        x_new = x_ref[...] + _rms(acc_ref[...], gpost_ref[...])
        o_ref[...] = x_new
        hn_ref[...] = _rms(x_new, gnext_ref[...]).astype(hn_ref.dtype)


def _ffn(x, g_pre, g_post, g_next, w1, w3, w2, layer):
    m, d = x.shape
    dff = w1.shape[2]
    tm = _tile(m, 512)
    tf = _col_tile(dff, 512)
    vec = pl.BlockSpec((1, d), lambda i, j: (0, 0))
    row = pl.BlockSpec((tm, d), lambda i, j: (i, 0))
    return pl.pallas_call(
        _ffn_kernel,
        grid=(m // tm, dff // tf),
        in_specs=[row, vec, vec, vec,
                  pl.BlockSpec((None, d, tf), lambda i, j: (layer, 0, j)),
                  pl.BlockSpec((None, d, tf), lambda i, j: (layer, 0, j)),
                  pl.BlockSpec((None, tf, d), lambda i, j: (layer, j, 0))],
        out_specs=[row, row],
        out_shape=[jax.ShapeDtypeStruct((m, d), F32), jax.ShapeDtypeStruct((m, d), BF16)],
        scratch_shapes=[pltpu.VMEM((tm, d), BF16), pltpu.VMEM((tm, d), F32)],
        compiler_params=_params("parallel", "arbitrary"),
        name="ffn_swiglu",
    )(x, g_pre.reshape(1, d), g_post.reshape(1, d), g_next.reshape(1, d), w1, w3, w2)


def _pad_rows(w, rows, offset):
    out = jnp.zeros((rows, w.shape[1]), w.dtype)
    return out.at[offset:offset + w.shape[0]].set(w)


_O_FF = 3 * FOX_W
_O_RWKV = _O_FF + FOX_HEADS
_O_GMLP = _O_RWKV + RWKV_IN_W
_O_GATE = _O_GMLP + 2 * GMLP_W


def _stacked_weights(w_in, w_br_fox, w_br_rwkv, w_br_gmlp, w_out, ffn_w1, ffn_w3, ffn_w2):
    return {
        "wq": w_in[:, :, :FOX_W].astype(BF16),
        "wk": w_in[:, :, FOX_W:2 * FOX_W].astype(BF16),
        "wv": w_in[:, :, 2 * FOX_W:_O_FF].astype(BF16),
        "w_rwkv": w_in[:, :, _O_RWKV:_O_GMLP].astype(BF16),
        "w_gmlp": w_in[:, :, _O_GMLP:_O_GATE].astype(BF16),
        "w_gate": w_in[:, :, _O_GATE:].astype(BF16),
        "w_br_fox": w_br_fox.astype(BF16),
        "w_br_rwkv": w_br_rwkv.astype(BF16),
        "w_br_gmlp": w_br_gmlp.astype(BF16),
        "w_out": w_out.astype(BF16),
        "ffn_w1": ffn_w1.astype(BF16),
        "ffn_w3": ffn_w3.astype(BF16),
        "ffn_w2": ffn_w2.astype(BF16),
    }


def _layer_weights(l, d, w_in, fox_bf, rwkv_mu, rwkv_w0, rwkv_w_up, rwkv_a0, rwkv_a_up, rwkv_g_up, rwkv_k_k,
                   rwkv_k_a, rwkv_r_k, rwkv_gn_g, rwkv_gn_b, gmlp_ln_g, gmlp_ln_b, gmlp_w_s, gmlp_b_s, s_new):
    tril_full = jnp.tril(gmlp_w_s[l])
    blk = jnp.tril(gmlp_w_s[l][:, :s_new, :s_new])
    reps = GMLP_CHUNK // s_new
    eye = jnp.eye(reps, dtype=F32)
    w_sample = jnp.einsum("ab,gij->gaibj", eye, blk).reshape(GMLP_GROUPS, GMLP_CHUNK, GMLP_CHUNK)
    return {
        "w_ff": jnp.zeros((d, LANE), BF16).at[:, :FOX_HEADS].set(w_in[l, :, _O_FF:_O_RWKV].astype(BF16)),
        "b_ff": jnp.zeros((1, LANE), F32).at[0, :FOX_HEADS].set(fox_bf[l]),
        "rwkv": {
            "mu": rwkv_mu[l].reshape(1, RWKV_IN_W),
            "w0": rwkv_w0[l].reshape(1, RWKV_W),
            "a0": rwkv_a0[l].reshape(1, RWKV_W),
            "k_k": rwkv_k_k[l].reshape(1, RWKV_W),
            "k_a": rwkv_k_a[l].reshape(1, RWKV_W),
            "r_k": rwkv_r_k[l].reshape(1, RWKV_W),
            "gn_g": rwkv_gn_g[l].reshape(1, RWKV_W),
            "gn_b": rwkv_gn_b[l].reshape(1, RWKV_W),
            "w_up": _pad_rows(rwkv_w_up[l], LANE, 0).astype(BF16),
            "a_up": _pad_rows(rwkv_a_up[l], LANE, rwkv_w_up.shape[1]).astype(BF16),
            "g_up": rwkv_g_up[l].astype(BF16),
        },
        "gmlp": {"ln_g": gmlp_ln_g[l].reshape(1, GMLP_W), "ln_b": gmlp_ln_b[l].reshape(1, GMLP_W)},
        "gmlp_w_prompt": tril_full.astype(BF16),
        "gmlp_b_prompt": gmlp_b_s[l].T,
        "gmlp_w_sample": w_sample.astype(BF16),
        "gmlp_b_sample": jnp.tile(gmlp_b_s[l][:, :s_new].T, (reps, 1)),
    }


def _trunk_layer(x, h, big, lw, post_mix_g, pre_ffn_g, post_ffn_g, next_mix_g, nseq, t, layer, cache, kv_stacks):
    m, d = x.shape
    q = _matmul(h, big["wq"], layer, (BF16,), scale=FOX_Q_SCALE, tn_pref=1024)
    k, k_bf = _kv_proj(h, big["wk"], layer, None if kv_stacks is None else kv_stacks[0])
    v, v_bf = _kv_proj(h, big["wv"], layer, None if kv_stacks is None else kv_stacks[1])
    rw = _matmul(h, big["w_rwkv"], layer, tn_pref=1664)
    gugv = _matmul(h, big["w_gmlp"], layer, (BF16,), tn_pref=1024)
    gates = _matmul(h, big["w_gate"], layer, (BF16,), tn_pref=1536)

    if cache is None:
        logf, xq, xk = _fox_forget(h, lw["w_ff"], lw["b_ff"], nseq, t)
        logf = logf[:, :FOX_HEADS]
        o_a = _fox_prompt(q, xq, k_bf, xk, v_bf, nseq, t)
        prev = jnp.zeros((nseq, 1, RWKV_IN_W), F32)
        state = None
        chunk = 64
        gm_w, gm_b = lw["gmlp_w_prompt"], lw["gmlp_b_prompt"]
    else:
        past = cache["k"].shape[2]
        total = past + t
        tpad = -(-total // LANE) * LANE
        logf = _forget_logits(h, lw["w_ff"], lw["b_ff"])[:, :FOX_HEADS]
        logf_rows = logf.reshape(nseq, t, FOX_HEADS).transpose(0, 2, 1)
        lf_all = jnp.concatenate([cache["logf"][layer].astype(F32).transpose(0, 2, 1), logf_rows], axis=-1)
        lf_all = jnp.pad(lf_all, ((0, 0), (0, 0), (0, tpad - total)))
        c_rows = _cumsum_lanes(lf_all.reshape(nseq * FOX_HEADS, tpad), LANE, LOG2E).reshape(nseq, FOX_HEADS, tpad)
        c_tm = c_rows[:, :, past:total].transpose(0, 2, 1).reshape(m, FOX_HEADS)
        o_a = _fox_sample(q, k_bf, v_bf, cache["k"], cache["v"], layer, c_tm, c_rows, nseq, t, past)
        prev = cache["shift"][layer]
        state = cache["state"]
        chunk = t
        gm_w, gm_b = lw["gmlp_w_sample"], lw["gmlp_b_sample"]

    r, lwd, k2, v2, kk, lr, g = _rwkv_prep(rw, prev, lw["rwkv"], nseq, t)
    o_b, s_new = _rwkv_chunks(r, lwd, k2, v2, kk, lr, g, lw["rwkv"], state, layer, nseq, t, chunk)
    shift_new = rw.reshape(nseq, t, RWKV_IN_W)[:, t - 1:t, :]

    o_c, vn = _gmlp(gugv, lw["gmlp"], gm_w, gm_b, cache is not None)

    mm = _merge(o_a, o_b, o_c, gates, big["w_br_fox"], big["w_br_rwkv"], big["w_br_gmlp"], layer)
    x = _out_proj(mm, big["w_out"], layer, x, post_mix_g)
    x, h_next = _ffn(x, pre_ffn_g, post_ffn_g, next_mix_g, big["ffn_w1"], big["ffn_w3"], big["ffn_w2"], layer)
    return x, h_next, (k, v, logf, s_new, shift_new, vn)


def kernel(x_prompt, x_sample, cache_fox_k, cache_fox_v, cache_fox_logf, state_rwkv, state_rwkv_shift, pre_mix_g, w_in, fox_bf, rwkv_mu, rwkv_w0, rwkv_w_up, rwkv_a0, rwkv_a_up, rwkv_g_up, rwkv_k_k, rwkv_k_a, rwkv_r_k, rwkv_gn_g, rwkv_gn_b, gmlp_ln_g, gmlp_ln_b, gmlp_w_s, gmlp_b_s, w_br_fox, w_br_rwkv, w_br_gmlp, w_out, post_mix_g, pre_ffn_g, ffn_w1, ffn_w3, ffn_w2, post_ffn_g):
    batch, seq, d = x_prompt.shape
    dec_batch, dec_seq, _ = x_sample.shape
    depth = w_in.shape[0]
    assert rwkv_w_up.shape[1] + rwkv_a_up.shape[1] == LANE and rwkv_g_up.shape[1] == LANE
    assert seq % GMLP_CHUNK == 0 and GMLP_CHUNK % dec_seq == 0 and (dec_batch * dec_seq) % GMLP_CHUNK == 0

    xp = x_prompt.reshape(batch * seq, d)
    xs = x_sample.reshape(dec_batch * dec_seq, d)
    cache = {"k": cache_fox_k, "v": cache_fox_v, "logf": cache_fox_logf,
             "state": state_rwkv, "shift": state_rwkv_shift}
    big = _stacked_weights(w_in, w_br_fox, w_br_rwkv, w_br_gmlp, w_out, ffn_w1, ffn_w3, ffn_w2)
    outs_p = [[] for _ in range(3)]
    outs_s = [[] for _ in range(6)]
    k_p = jnp.zeros((depth, batch * seq, FOX_HEADS, FOX_HD), F32)
    v_p = jnp.zeros((depth, batch * seq, FOX_HEADS, FOX_HD), F32)
    hp = _rmsnorm_cast(xp, pre_mix_g[0])
    hs = _rmsnorm_cast(xs, pre_mix_g[0])
    for l in range(depth):
        lw = _layer_weights(l, d, w_in, fox_bf, rwkv_mu, rwkv_w0, rwkv_w_up, rwkv_a0, rwkv_a_up, rwkv_g_up,
                            rwkv_k_k, rwkv_k_a, rwkv_r_k, rwkv_gn_g, rwkv_gn_b, gmlp_ln_g, gmlp_ln_b, gmlp_w_s,
                            gmlp_b_s, dec_seq)
        norms = (post_mix_g[l], pre_ffn_g[l], post_ffn_g[l], pre_mix_g[(l + 1) % depth])
        xp, hp, (k_p, v_p, lf_p, s_p, sh_p, _) = _trunk_layer(xp, hp, big, lw, *norms, batch, seq, l, None, (k_p, v_p))
        xs, hs, (k_s, v_s, lf_s, s_s, sh_s, vn_s) = _trunk_layer(xs, hs, big, lw, *norms, dec_batch, dec_seq, l, cache,
                                                                 None)
        for acc, val in zip(outs_p, (lf_p.reshape(batch, seq, FOX_HEADS), s_p, sh_p)):
            acc.append(val)
        for acc, val in zip(outs_s, (k_s.reshape(dec_batch, dec_seq, FOX_HEADS, FOX_HD),
                                     v_s.reshape(dec_batch, dec_seq, FOX_HEADS, FOX_HD),
                                     lf_s.reshape(dec_batch, dec_seq, FOX_HEADS), s_s, sh_s,
                                     vn_s.reshape(dec_batch, dec_seq, GMLP_W))):
            acc.append(val)
    return (xp.reshape(batch, seq, d), xs.reshape(dec_batch, dec_seq, d),
            k_p.reshape(depth, batch, seq, FOX_HEADS, FOX_HD), v_p.reshape(depth, batch, seq, FOX_HEADS, FOX_HD),
            *[jnp.stack(o) for o in outs_p], *[jnp.stack(o) for o in outs_s])
```

```python
import functools
import math

import jax
import jax.numpy as jnp
from jax import lax
from jax.experimental import pallas as pl
from jax.experimental.pallas import tpu as pltpu

F32 = jnp.float32
BF16 = jnp.bfloat16

RMS_EPS = 1e-6
LN_EPS = 1e-5
RWKV_GN_EPS = 64e-5
NEG_INF = -1e30
LOG2E = math.log2(math.e)

LANE = 128
FOX_HEADS = 8
FOX_HD = 128
FOX_W = FOX_HEADS * FOX_HD
FOX_Q_SCALE = FOX_HD ** -0.5 * LOG2E
RWKV_HEADS = 16
RWKV_HD = 64
RWKV_W = RWKV_HEADS * RWKV_HD
RWKV_PAIRS = RWKV_W // LANE
RWKV_LORA_OFF = 3 * RWKV_W
RWKV_IN_W = 3 * RWKV_W + 2 * LANE
GMLP_W = 1024
GMLP_GROUPS = 8
GMLP_GC = GMLP_W // GMLP_GROUPS
GMLP_CHUNK = 128
N_BRANCH = 3

VMEM_LIMIT = 48 * 1024 * 1024


def _params(*sem):
    return pltpu.CompilerParams(dimension_semantics=sem, vmem_limit_bytes=VMEM_LIMIT)


def _tile(n, pref):
    if n <= pref:
        return n
    t = pref
    while n % t:
        t -= 8
    return t


def _col_tile(n, pref):
    assert n % LANE == 0
    k = n // LANE
    best = 1
    for d in range(1, k + 1):
        if k % d == 0 and d * LANE <= pref:
            best = d
    return best * LANE


def _split3(x):
    hi = x.astype(BF16)
    r = x - hi.astype(F32)
    mid = r.astype(BF16)
    lo = (r - mid.astype(F32)).astype(BF16)
    return hi, mid, lo


_NN = (((1,), (0,)), ((), ()))
_NT = (((1,), (1,)), ((), ()))
_TN = (((0,), (0,)), ((), ()))


def _dot(a, b, dims=_NN):
    return lax.dot_general(a, b, dims, preferred_element_type=F32)


def _dot_exact_rhs(x, e):
    hi, mid, lo = _split3(x)
    return _dot(hi, e) + _dot(mid, e) + _dot(lo, e)


def _sigmoid(x):
    return 1.0 / (1.0 + jnp.exp(-x))


def _log_sigmoid(x):
    return jnp.minimum(x, 0.0) - jnp.log(1.0 + jnp.exp(-jnp.abs(x)))


def _gelu_tanh(x):
    return 0.5 * x * (1.0 + jnp.tanh(0.7978845608028654 * (x + 0.044715 * (x * x * x))))


def _rms(x, g):
    ms = jnp.mean(x * x, axis=-1, keepdims=True)
    return x * lax.rsqrt(ms + RMS_EPS) * g


def _rmsnorm_cast_kernel(x_ref, g_ref, o_ref):
    o_ref[...] = _rms(x_ref[...], g_ref[...]).astype(o_ref.dtype)


def _rmsnorm_cast(x, g):
    m, d = x.shape
    tm = _tile(m, 512)
    return pl.pallas_call(
        _rmsnorm_cast_kernel,
        grid=(m // tm,),
        in_specs=[pl.BlockSpec((tm, d), lambda i: (i, 0)),
                  pl.BlockSpec((1, d), lambda i: (0, 0))],
        out_specs=pl.BlockSpec((tm, d), lambda i: (i, 0)),
        out_shape=jax.ShapeDtypeStruct((m, d), BF16),
        compiler_params=_params("parallel"),
        name="rmsnorm_cast",
    )(x, g.reshape(1, d))


def _matmul_kernel(a_ref, w_ref, *o_refs, scale):
    acc = _dot(a_ref[...], w_ref[...])
    if scale is not None:
        acc = acc * scale
    for o_ref in o_refs:
        o_ref[...] = acc.astype(o_ref.dtype)


def _matmul(a, w, layer, out_dtypes=(F32,), scale=None, tm_pref=1024, tn_pref=512):
    m, k = a.shape
    n = w.shape[2]
    tm = _tile(m, tm_pref)
    tn = _col_tile(n, tn_pref)
    o_spec = pl.BlockSpec((tm, tn), lambda i, j: (i, j))
    res = pl.pallas_call(
        functools.partial(_matmul_kernel, scale=scale),
        grid=(m // tm, n // tn),
        in_specs=[pl.BlockSpec((tm, k), lambda i, j: (i, 0)),
                  pl.BlockSpec((None, k, tn), lambda i, j: (layer, 0, j))],
        out_specs=[o_spec] * len(out_dtypes),
        out_shape=[jax.ShapeDtypeStruct((m, n), dt) for dt in out_dtypes],
        compiler_params=_params("parallel", "parallel"),
        name="proj_matmul",
    )(a, w)
    return res[0] if len(out_dtypes) == 1 else res


def _kv_proj_kernel(a_ref, w_ref, *refs):
    o_ref, obf_ref = refs[-2:]
    acc = _dot(a_ref[...], w_ref[...])
    obf_ref[...] = acc.astype(obf_ref.dtype)
    o_ref[...] = pltpu.einshape("t(hd)->thd", acc, h=FOX_HEADS)


def _kv_proj(a, w, layer, stacked=None):
    m, k = a.shape
    tm = _tile(m, 512)
    in_specs = [pl.BlockSpec((tm, k), lambda i: (i, 0)),
                pl.BlockSpec((None, k, FOX_W), lambda i: (layer, 0, 0))]
    args = [a, w]
    if stacked is None:
        o_spec = pl.BlockSpec((tm, FOX_HEADS, FOX_HD), lambda i: (i, 0, 0))
        o_shape = jax.ShapeDtypeStruct((m, FOX_HEADS, FOX_HD), F32)
        aliases = {}
    else:
        in_specs.append(pl.BlockSpec(memory_space=pl.ANY))
        args.append(stacked)
        o_spec = pl.BlockSpec((None, tm, FOX_HEADS, FOX_HD), lambda i: (layer, i, 0, 0))
        o_shape = jax.ShapeDtypeStruct(stacked.shape, F32)
        aliases = {2: 0}
    return pl.pallas_call(
        _kv_proj_kernel,
        grid=(m // tm,),
        in_specs=in_specs,
        out_specs=[o_spec, pl.BlockSpec((tm, FOX_W), lambda i: (i, 0))],
        out_shape=[o_shape, jax.ShapeDtypeStruct((m, FOX_W), BF16)],
        input_output_aliases=aliases,
        compiler_params=_params("parallel"),
        name="kv_proj",
    )(*args)


def _logf_kernel(a_ref, w_ref, b_ref, o_ref):
    o_ref[...] = _log_sigmoid(_dot(a_ref[...], w_ref[...]) + b_ref[...])


def _forget_logits(h, w_ff, b_ff):
    m, k = h.shape
    tm = _tile(m, 1024)
    return pl.pallas_call(
        _logf_kernel,
        grid=(m // tm,),
        in_specs=[pl.BlockSpec((tm, k), lambda i: (i, 0)),
                  pl.BlockSpec((k, LANE), lambda i: (0, 0)),
                  pl.BlockSpec((1, LANE), lambda i: (0, 0))],
        out_specs=pl.BlockSpec((tm, LANE), lambda i: (i, 0)),
        out_shape=jax.ShapeDtypeStruct((m, LANE), F32),
        compiler_params=_params("parallel"),
        name="forget_logits",
    )(h, w_ff, b_ff)


def _cumsum_kernel(x_ref, o_ref, carry_ref, *, tb, scale):
    @pl.when(pl.program_id(0) == 0)
    def _():
        carry_ref[...] = jnp.zeros_like(carry_ref)

    row = lax.broadcasted_iota(jnp.int32, (tb, tb), 0)
    col = lax.broadcasted_iota(jnp.int32, (tb, tb), 1)
    upper = jnp.where(row <= col, 1.0, 0.0).astype(BF16)
    c = _dot_exact_rhs(x_ref[...], upper) + carry_ref[:, :1]
    o_ref[...] = c * scale
    carry_ref[...] = jnp.broadcast_to(c[:, tb - 1:tb], carry_ref.shape)


def _cumsum_lanes(x, tb, scale):
    r, t = x.shape
    return pl.pallas_call(
        functools.partial(_cumsum_kernel, tb=tb, scale=scale),
        grid=(t // tb,),
        in_specs=[pl.BlockSpec((r, tb), lambda i: (0, i))],
        out_specs=pl.BlockSpec((r, tb), lambda i: (0, i)),
        out_shape=jax.ShapeDtypeStruct((r, t), F32),
        scratch_shapes=[pltpu.VMEM((r, LANE), F32)],
        compiler_params=_params("arbitrary"),
        name="logf_cumsum",
    )(x)


def _fox_bias_placement():
    part = jnp.arange(3 * LANE) // LANE
    head = jnp.arange(3 * LANE) % LANE
    out_head = jnp.arange(FOX_W) // FOX_HD
    out_lane = jnp.arange(FOX_W) % FOX_HD
    same_head = (head[:, None] == out_head[None, :]) & (head[:, None] < FOX_HEADS)
    pq = jnp.where(same_head & (out_lane[None, :] == part[:, None]), 1.0, 0.0).astype(BF16)
    pk = jnp.where(same_head & (out_lane[None, :] == part[:, None] + 3), -1.0, 0.0).astype(BF16)
    one_q = jnp.where((out_lane >= 3) & (out_lane < 6), 1.0, 0.0).reshape(1, FOX_W)
    one_k = jnp.where(out_lane < 3, 1.0, 0.0).reshape(1, FOX_W)
    return pq, pk, one_q, one_k


def _fox_forget_kernel(h_ref, w_ref, b_ref, pq_ref, pk_ref, oneq_ref, onek_ref, logf_ref, xq_ref, xk_ref, carry_ref,
                       *, tb):
    @pl.when(pl.program_id(1) == 0)
    def _():
        carry_ref[...] = jnp.zeros_like(carry_ref)

    logf = _log_sigmoid(_dot(h_ref[...], w_ref[...]) + b_ref[...])
    logf_ref[...] = logf
    row = lax.broadcasted_iota(jnp.int32, (tb, tb), 0)
    col = lax.broadcasted_iota(jnp.int32, (tb, tb), 1)
    lower = jnp.where(col <= row, 1.0, 0.0).astype(BF16)
    hi, mid, lo = _split3(logf)
    c = _dot(lower, hi) + _dot(lower, mid) + _dot(lower, lo) + carry_ref[:1, :]
    carry_ref[...] = jnp.broadcast_to(c[tb - 1:tb, :], carry_ref.shape)
    parts = jnp.concatenate(_split3(c * LOG2E), axis=1)
    xq_ref[...] = (_dot(parts, pq_ref[...]) + oneq_ref[...]).astype(BF16)
    xk_ref[...] = (_dot(parts, pk_ref[...]) + onek_ref[...]).astype(BF16)


def _fox_forget(h, w_ff, b_ff, nseq, t):
    m, k = h.shape
    tb = _tile(t, 256)
    nb = t // tb
    rows = lambda s, i: (s * nb + i, 0)
    const = lambda s, i: (0, 0)
    return pl.pallas_call(
        functools.partial(_fox_forget_kernel, tb=tb),
        grid=(nseq, nb),
        in_specs=[pl.BlockSpec((tb, k), rows),
                  pl.BlockSpec((k, LANE), const),
                  pl.BlockSpec((1, LANE), const),
                  pl.BlockSpec((3 * LANE, FOX_W), const),
                  pl.BlockSpec((3 * LANE, FOX_W), const),
                  pl.BlockSpec((1, FOX_W), const),
                  pl.BlockSpec((1, FOX_W), const)],
        out_specs=[pl.BlockSpec((tb, LANE), rows), pl.BlockSpec((tb, FOX_W), rows), pl.BlockSpec((tb, FOX_W), rows)],
        out_shape=[jax.ShapeDtypeStruct((m, LANE), F32),
                   jax.ShapeDtypeStruct((m, FOX_W), BF16),
                   jax.ShapeDtypeStruct((m, FOX_W), BF16)],
        scratch_shapes=[pltpu.VMEM((8, LANE), F32)],
        compiler_params=_params("parallel", "arbitrary"),
        name="fox_forget",
    )(h, w_ff, b_ff, *_fox_bias_placement())


FOX_KEYS_FULL, FOX_KEYS_DIAG_HALF, FOX_KEYS_FULL_DIAG = 0, 1, 2
FOX_HEAD_GROUP = 4


def _fox_prompt_kernel(qi_ref, kj_ref, kind_ref, q_ref, xq_ref, k_ref, xk_ref, v_ref, o_ref, m_ref, l_ref, acc_ref,
                       *, tq):
    step = pl.program_id(1)
    kj = kj_ref[step]
    kind = kind_ref[step]

    @pl.when(kj == 0)
    def _():
        m_ref[...] = jnp.full_like(m_ref, NEG_INF)
        l_ref[...] = jnp.zeros_like(l_ref)
        acc_ref[...] = jnp.zeros_like(acc_ref)

    def update(keys_kind):
        nk = tq if keys_kind == FOX_KEYS_DIAG_HALF else 2 * tq
        keys = slice(0, nk)
        if keys_kind != FOX_KEYS_FULL:
            row = lax.broadcasted_iota(jnp.int32, (tq, nk), 0)
            col = lax.broadcasted_iota(jnp.int32, (tq, nk), 1)
            causal = col <= row + (nk - tq)
        ones = jnp.ones((nk, LANE), BF16)
        for g0 in range(0, FOX_HEADS, FOX_HEAD_GROUP):
            heads = range(g0, g0 + FOX_HEAD_GROUP)
            cols = [slice(h * FOX_HD, (h + 1) * FOX_HD) for h in heads]
            s = [_dot(jnp.concatenate([q_ref[:, hs], xq_ref[:, hs]], axis=1),
                      jnp.concatenate([k_ref[keys, hs], xk_ref[keys, hs]], axis=1), _NT) for hs in cols]
            if keys_kind != FOX_KEYS_FULL:
                s = [jnp.where(causal, x, NEG_INF) for x in s]
            m_prev = [m_ref[h] for h in heads]
            m_new = [jnp.maximum(mp, jnp.max(x, axis=-1, keepdims=True)) for mp, x in zip(m_prev, s)]
            p = [jnp.exp2((x - jnp.concatenate([mn] * (nk // LANE), axis=1)).astype(BF16)) for x, mn in zip(s, m_new)]
            pv1 = [_dot(x, jnp.concatenate([v_ref[keys, hs], ones], axis=1)) for x, hs in zip(p, cols)]
            for i, h in enumerate(heads):
                alpha = jnp.exp2(m_prev[i] - m_new[i])
                l_ref[h] = alpha * l_ref[h] + pv1[i][:, LANE:]
                acc_ref[:, cols[i]] = alpha * acc_ref[:, cols[i]] + pv1[i][:, :LANE]
                m_ref[h] = m_new[i]

    def finish():
        for h in range(FOX_HEADS):
            hs = slice(h * FOX_HD, (h + 1) * FOX_HD)
            o_ref[:, hs] = (acc_ref[:, hs] / l_ref[h]).astype(o_ref.dtype)

    @pl.when(kind == FOX_KEYS_FULL)
    def _():
        update(FOX_KEYS_FULL)

    @pl.when(kind == FOX_KEYS_DIAG_HALF)
    def _():
        update(FOX_KEYS_DIAG_HALF)
        finish()

    @pl.when(kind == FOX_KEYS_FULL_DIAG)
    def _():
        update(FOX_KEYS_FULL_DIAG)
        finish()


def _fox_prompt(q, xq, k, xk, v, batch, seq):
    tq = _tile(seq // 2, 512)
    assert tq % LANE == 0 and seq % (2 * tq) == 0
    nq = seq // tq
    blocks = []
    for qi in range(nq):
        last = qi // 2
        blocks += [(qi, kj, FOX_KEYS_FULL) for kj in range(last)]
        blocks.append((qi, last, FOX_KEYS_DIAG_HALF if qi % 2 == 0 else FOX_KEYS_FULL_DIAG))
    tabs = [jnp.asarray([b[i] for b in blocks], jnp.int32) for i in range(3)]
    q_map = lambda b, s, qt, kt, kinds: (b * nq + qt[s], 0)
    kv_map = lambda b, s, qt, kt, kinds: (b * (nq // 2) + kt[s], 0)
    return pl.pallas_call(
        functools.partial(_fox_prompt_kernel, tq=tq),
        grid_spec=pltpu.PrefetchScalarGridSpec(
            num_scalar_prefetch=3,
            grid=(batch, len(blocks)),
            in_specs=[pl.BlockSpec((tq, FOX_W), q_map),
                      pl.BlockSpec((tq, FOX_W), q_map),
                      pl.BlockSpec((2 * tq, FOX_W), kv_map),
                      pl.BlockSpec((2 * tq, FOX_W), kv_map),
                      pl.BlockSpec((2 * tq, FOX_W), kv_map)],
            out_specs=pl.BlockSpec((tq, FOX_W), q_map),
            scratch_shapes=[pltpu.VMEM((FOX_HEADS, tq, LANE), F32),
                            pltpu.VMEM((FOX_HEADS, tq, LANE), F32),
                            pltpu.VMEM((tq, FOX_W), F32)]),
        out_shape=jax.ShapeDtypeStruct((batch * seq, FOX_W), BF16),
        compiler_params=_params("parallel", "arbitrary"),
        name="fox_prompt",
    )(*tabs, q, xq, k, xk, v)


def _fox_sample_kernel(q_ref, kn_ref, vn_ref, cq_ref, ck_ref, kc_ref, vc_ref, o_ref, *, past, s_new):
    kc = pltpu.einshape("thd->htd", kc_ref[0, 0])
    vc = pltpu.einshape("thd->htd", vc_ref[0, 0])
    row = lax.broadcasted_iota(jnp.int32, (s_new, s_new), 0)
    col = lax.broadcasted_iota(jnp.int32, (s_new, s_new), 1)
    causal = col <= row
    heads = range(FOX_HEADS)
    cols = [slice(h * FOX_HD, (h + 1) * FOX_HD) for h in heads]
    cq = [cq_ref[:, h:h + 1] for h in heads]
    s_c = [_dot(q_ref[:, hs], kc[h].astype(BF16), _NT) + (cq[h] - ck_ref[0, h:h + 1, :past])
           for h, hs in zip(heads, cols)]
    s_n = [jnp.where(causal, _dot(q_ref[:, hs], kn_ref[:, hs], _NT)
                     + (cq[h] - ck_ref[0, h:h + 1, past:past + s_new]), NEG_INF) for h, hs in zip(heads, cols)]
    m = [jnp.maximum(jnp.max(x, axis=-1, keepdims=True), jnp.max(y, axis=-1, keepdims=True)) for x, y in zip(s_c, s_n)]
    p_c = [jnp.exp2(x - mx) for x, mx in zip(s_c, m)]
    p_n = [jnp.exp2(x - mx) for x, mx in zip(s_n, m)]
    l = [jnp.sum(x, axis=-1, keepdims=True) + jnp.sum(y, axis=-1, keepdims=True) for x, y in zip(p_c, p_n)]
    o = [_dot(x.astype(BF16), vc[h].astype(BF16)) + _dot(y.astype(BF16), vn_ref[:, hs])
         for h, hs, x, y in zip(heads, cols, p_c, p_n)]
    for h in heads:
        o_ref[:, cols[h]] = (o[h] / l[h]).astype(o_ref.dtype)


def _fox_sample(q, k, v, k_cache, v_cache, layer, c_tm, c_rows, batch, s_new, past):
    tpad = c_rows.shape[-1]
    new_spec = pl.BlockSpec((s_new, FOX_W), lambda b: (b, 0))
    cache_spec = pl.BlockSpec((1, 1, past, FOX_HEADS, FOX_HD), lambda b: (layer, b, 0, 0, 0))
    return pl.pallas_call(
        functools.partial(_fox_sample_kernel, past=past, s_new=s_new),
        grid=(batch,),
        in_specs=[new_spec, new_spec, new_spec,
                  pl.BlockSpec((s_new, FOX_HEADS), lambda b: (b, 0)),
                  pl.BlockSpec((1, FOX_HEADS, tpad), lambda b: (b, 0, 0)), cache_spec, cache_spec],
        out_specs=new_spec,
        out_shape=jax.ShapeDtypeStruct((batch * s_new, FOX_W), BF16),
        compiler_params=_params("parallel"),
        name="fox_sample",
    )(q, k, v, c_tm, c_rows, k_cache, v_cache)


def _rwkv_prep_kernel(p_ref, prev_ref, mu_ref, w0_ref, a0_ref, kk_ref, ka_ref, wup_ref, aup_ref, gup_ref,
                      r_ref, lw_ref, k_ref, v_ref, kk_out_ref, a_ref, g_ref, carry_ref, *, tr):
    @pl.when(pl.program_id(1) == 0)
    def _():
        carry_ref[...] = prev_ref[0]

    p = p_ref[0]
    rows = lax.broadcasted_iota(jnp.int32, (tr, 1), 0)
    p_prev = jnp.where(rows == 0, carry_ref[...], pltpu.roll(p, shift=1, axis=0))
    carry_ref[...] = p[tr - 1:tr, :]
    xs = p + (p_prev - p) * mu_ref[...]

    r = xs[:, :RWKV_W]
    k = xs[:, RWKV_W:2 * RWKV_W]
    v = xs[:, 2 * RWKV_W:3 * RWKV_W]
    d_wa = xs[:, RWKV_LORA_OFF:RWKV_LORA_OFF + LANE]
    d_g = xs[:, RWKV_LORA_OFF + LANE:RWKV_LORA_OFF + 2 * LANE]

    z_w = w0_ref[...] + _dot(jnp.tanh(d_wa).astype(BF16), wup_ref[...])
    log_decay = -jnp.exp(_log_sigmoid(z_w) - 0.5)
    a = _sigmoid(a0_ref[...] + _dot(d_wa.astype(BF16), aup_ref[...]))
    g = _dot(_sigmoid(d_g).astype(BF16), gup_ref[...])

    r_ref[...] = r.astype(r_ref.dtype)
    lw_ref[...] = log_decay
    k_ref[...] = (k * (1.0 + (a - 1.0) * ka_ref[...])).astype(k_ref.dtype)
    v_ref[...] = v.astype(v_ref.dtype)
    kk_out_ref[...] = (k * kk_ref[...]).astype(kk_out_ref.dtype)
    a_ref[...] = a.astype(a_ref.dtype)
    g_ref[...] = g.astype(g_ref.dtype)


def _rwkv_prep(rw, prev, lw, nseq, t):
    tr = _tile(t, 256)
    nb = t // tr
    wide = lambda s, i: (0, 0)
    row_spec = pl.BlockSpec((tr, RWKV_W), lambda s, i: (s * nb + i, 0))
    out_dtypes = (BF16, F32, BF16, BF16, BF16, BF16, BF16)
    return pl.pallas_call(
        functools.partial(_rwkv_prep_kernel, tr=tr),
        grid=(nseq, nb),
        in_specs=[pl.BlockSpec((1, tr, RWKV_IN_W), lambda s, i: (s, i, 0)),
                  pl.BlockSpec((1, 1, RWKV_IN_W), lambda s, i: (s, 0, 0)),
                  pl.BlockSpec((1, RWKV_IN_W), wide),
                  pl.BlockSpec((1, RWKV_W), wide),
                  pl.BlockSpec((1, RWKV_W), wide),
                  pl.BlockSpec((1, RWKV_W), wide),
                  pl.BlockSpec((1, RWKV_W), wide),
                  pl.BlockSpec((LANE, RWKV_W), wide),
                  pl.BlockSpec((LANE, RWKV_W), wide),
                  pl.BlockSpec((LANE, RWKV_W), wide)],
        out_specs=[row_spec] * 7,
        out_shape=[jax.ShapeDtypeStruct((nseq * t, RWKV_W), dt) for dt in out_dtypes],
        scratch_shapes=[pltpu.VMEM((1, RWKV_IN_W), F32)],
        compiler_params=_params("parallel", "arbitrary"),
        name="rwkv_prep",
    )(rw.reshape(nseq, t, RWKV_IN_W), prev, lw["mu"], lw["w0"], lw["a0"], lw["k_k"], lw["k_a"],
      lw["w_up"], lw["a_up"], lw["g_up"])


RWKV_CHUNKS_PER_STEP = 2


def _rwkv_chunk_kernel(*refs, c, has_s0):
    r_ref, lw_ref, k_ref, v_ref, kk_ref, lr_ref, g_ref, gng_ref, gnb_ref, rk_ref = refs[:10]
    s0_ref = refs[10] if has_s0 else None
    o_ref, sout_ref, s_ref = refs[-3:]
    ci = pl.program_id(1)
    pairs = range(RWKV_PAIRS)

    @pl.when(ci == 0)
    def _():
        if has_s0:
            zero = jnp.zeros((RWKV_HD, RWKV_HD), F32)
            for pr in pairs:
                top = jnp.concatenate([s0_ref[0, 0, 2 * pr], zero], axis=1)
                bot = jnp.concatenate([zero, s0_ref[0, 0, 2 * pr + 1]], axis=1)
                s_ref[pr] = jnp.concatenate([top, bot], axis=0)
        else:
            s_ref[...] = jnp.zeros_like(s_ref)

    c2 = 2 * c
    row = lax.broadcasted_iota(jnp.int32, (c2, c2), 0)
    col = lax.broadcasted_iota(jnp.int32, (c2, c2), 1)
    lower = col <= row
    strict = col < row
    eye = jnp.where(row == col, 1.0, 0.0)
    rc = lax.broadcasted_iota(jnp.int32, (c, c), 0)
    cc = lax.broadcasted_iota(jnp.int32, (c, c), 1)
    tri = jnp.where(cc <= rc, 1.0, 0.0).astype(BF16)
    lane = lax.broadcasted_iota(jnp.int32, (c, LANE), 1)
    first = lane < RWKV_HD
    steps = max(c.bit_length() - 2, 0)

    def stack(x):
        return jnp.concatenate([jnp.where(first, x, 0.0), jnp.where(first, 0.0, x)], axis=0)

    own = (lax.broadcasted_iota(jnp.int32, (c2, LANE), 1) // RWKV_HD
           == lax.broadcasted_iota(jnp.int32, (c2, LANE), 0) // c)

    n_sub = r_ref.shape[0] // c
    lanes = [slice(pr * LANE, (pr + 1) * LANE) for pr in pairs]
    units = [(slice(sb * c, (sb + 1) * c), ls) for sb in range(n_sub) for ls in lanes]
    lws = [lw_ref[rows, ls] for rows, ls in units]
    cum3 = [_dot(tri, jnp.concatenate(_split3(lw), axis=1)) for lw in lws]
    cums = [x[:, :LANE] + x[:, LANE:2 * LANE] + x[:, 2 * LANE:] for x in cum3]
    totals = [cum[c - 1:c, :] for cum in cums]

    lhs, rhs, bar, v_s, rk_s = [], [], [], [], []
    for (rows, ls), lw, cum, total in zip(units, lws, cums, totals):
        grow = jnp.exp(-cum)
        rest = jnp.exp(total - cum)
        r = r_ref[rows, ls].astype(F32)
        k = k_ref[rows, ls].astype(F32)
        kk = kk_ref[rows, ls].astype(F32)
        norm = lax.rsqrt(jnp.maximum(jnp.sum(stack(kk * kk), axis=-1, keepdims=True), 1e-24))
        kk = kk * jnp.where(first, norm[:c], norm[c:])
        b = kk * lr_ref[rows, ls].astype(F32)
        lhs.append(jnp.concatenate([stack(-kk * jnp.exp(cum - lw)), stack(r * jnp.exp(cum))], axis=0).astype(BF16))
        rhs.append(jnp.concatenate([stack(k * grow), stack(b * grow)], axis=0).astype(BF16))
        bar.append(jnp.concatenate([stack(k * rest), stack(b * rest)], axis=0).astype(BF16))
        v_s.append(stack(v_ref[rows, ls].astype(F32)))
        rk_s.append(stack(r * k * rk_ref[:, ls]))

    amats = [_dot(x, y, _NT) for x, y in zip(lhs, rhs)]
    a_ab = [jnp.where(strict, m[:c2, c2:], 0.0) for m in amats]
    invs = [eye + n for n in a_ab]
    powers = [_dot(n.astype(BF16), n.astype(BF16)) for n in a_ab]
    for i in range(steps):
        if i < steps - 1:
            xs = [_dot(jnp.concatenate([inv, pw], axis=0).astype(BF16), pw.astype(BF16))
                  for inv, pw in zip(invs, powers)]
            invs = [inv + x[:c2] for inv, x in zip(invs, xs)]
            powers = [x[c2:] for x in xs]
        else:
            invs = [inv + _dot(inv.astype(BF16), pw.astype(BF16)) for inv, pw in zip(invs, powers)]

    akv = [_dot(jnp.where(strict, m[:c2, :c2], 0.0).astype(BF16), v.astype(BF16)) for m, v in zip(amats, v_s)]
    a_rs = [jnp.concatenate([jnp.where(lower, m[c2:, :c2], 0.0), jnp.where(lower, m[c2:, c2:], 0.0)],
                            axis=1).astype(BF16) for m in amats]

    s_mats = [s_ref[pr] for pr in pairs]
    for sb in range(n_sub):
        base = sb * RWKV_PAIRS
        sxs = [_dot(lhs[base + pr], s_mats[pr].astype(BF16), _NT) for pr in pairs]
        us = [_dot(invs[base + pr].astype(BF16), (sxs[pr][:c2] + akv[base + pr]).astype(BF16)) for pr in pairs]
        vus = [jnp.concatenate([v_s[base + pr], us[pr]], axis=0).astype(BF16) for pr in pairs]
        for pr in pairs:
            i = base + pr
            rows, ls = units[i]
            y_s = sxs[pr][c2:] + _dot(a_rs[i], vus[pr])
            s_mats[pr] = s_mats[pr] * jnp.exp(totals[i]) + _dot(vus[pr], bar[i], _TN)
            mu = jnp.sum(y_s, axis=-1, keepdims=True) * (1.0 / RWKV_HD)
            d = jnp.where(own, y_s - mu, 0.0)
            var = jnp.sum(d * d, axis=-1, keepdims=True) * (1.0 / RWKV_HD)
            yn_s = d * lax.rsqrt(var + RWKV_GN_EPS)
            bonus_s = jnp.sum(rk_s[i], axis=-1, keepdims=True) * v_s[i]
            yn = (yn_s[:c] + yn_s[c:]) * gng_ref[:, ls] + gnb_ref[:, ls]
            o_ref[rows, ls] = ((yn + (bonus_s[:c] + bonus_s[c:])) * g_ref[rows, ls].astype(F32)).astype(o_ref.dtype)
    for pr in pairs:
        s_ref[pr] = s_mats[pr]

    @pl.when(ci == pl.num_programs(1) - 1)
    def _():
        for pr in pairs:
            s_mat = s_ref[pr]
            sout_ref[0, 2 * pr] = s_mat[:RWKV_HD, :RWKV_HD]
            sout_ref[0, 2 * pr + 1] = s_mat[RWKV_HD:, RWKV_HD:]


def _rwkv_chunks(r, lwd, k, v, kk, lr, g, lw, state, layer, nseq, t, c):
    rows = c * RWKV_CHUNKS_PER_STEP if t % (c * RWKV_CHUNKS_PER_STEP) == 0 else c
    nc = t // rows
    row_spec = pl.BlockSpec((rows, RWKV_W), lambda s, i: (s * nc + i, 0))
    par_spec = pl.BlockSpec((1, RWKV_W), lambda s, i: (0, 0))
    in_specs = [row_spec] * 7 + [par_spec] * 3
    args = [r, lwd, k, v, kk, lr, g, lw["gn_g"], lw["gn_b"], lw["r_k"]]
    if state is not None:
        in_specs.append(pl.BlockSpec((1, 1, RWKV_HEADS, RWKV_HD, RWKV_HD), lambda s, i: (layer, s, 0, 0, 0)))
        args.append(state)
    return pl.pallas_call(
        functools.partial(_rwkv_chunk_kernel, c=c, has_s0=state is not None),
        grid=(nseq, nc),
        in_specs=in_specs,
        out_specs=[row_spec, pl.BlockSpec((1, RWKV_HEADS, RWKV_HD, RWKV_HD), lambda s, i: (s, 0, 0, 0))],
        out_shape=[jax.ShapeDtypeStruct((nseq * t, RWKV_W), BF16),
                   jax.ShapeDtypeStruct((nseq, RWKV_HEADS, RWKV_HD, RWKV_HD), F32)],
        scratch_shapes=[pltpu.VMEM((RWKV_PAIRS, LANE, LANE), F32)],
        compiler_params=_params("parallel", "arbitrary"),
        name="rwkv_chunks",
    )(*args)


def _gmlp_kernel(gugv_ref, lng_ref, lnb_ref, ws_ref, bs_ref, o_ref, *vn_out):
    for blk in range(o_ref.shape[0] // GMLP_CHUNK):
        rows = slice(blk * GMLP_CHUNK, (blk + 1) * GMLP_CHUNK)
        u = _gelu_tanh(gugv_ref[rows, :GMLP_W].astype(F32))
        gv = _gelu_tanh(gugv_ref[rows, GMLP_W:].astype(F32))
        mu = jnp.mean(gv, axis=-1, keepdims=True)
        d = gv - mu
        var = jnp.mean(d * d, axis=-1, keepdims=True)
        vn = d * lax.rsqrt(var + LN_EPS) * lng_ref[...] + lnb_ref[...]
        if vn_out:
            vn_out[0][rows, :] = vn
        vb = vn.astype(BF16)
        for g in range(GMLP_GROUPS):
            gs = slice(g * GMLP_GC, (g + 1) * GMLP_GC)
            s = _dot(ws_ref[g], vb[:, gs]) + bs_ref[:, g:g + 1]
            o_ref[rows, gs] = (u[:, gs] * s).astype(o_ref.dtype)


def _gmlp(gugv, lw, w_eff, b_eff, emit_vn):
    m = gugv.shape[0]
    tr = _tile(m, 4 * GMLP_CHUNK)
    assert tr % GMLP_CHUNK == 0
    row_spec = pl.BlockSpec((tr, GMLP_W), lambda i: (i, 0))
    out_shape = [jax.ShapeDtypeStruct((m, GMLP_W), BF16)]
    out_specs = [row_spec]
    if emit_vn:
        out_shape.append(jax.ShapeDtypeStruct((m, GMLP_W), F32))
        out_specs.append(row_spec)
    res = pl.pallas_call(
        _gmlp_kernel,
        grid=(m // tr,),
        in_specs=[pl.BlockSpec((tr, 2 * GMLP_W), lambda i: (i, 0)),
                  pl.BlockSpec((1, GMLP_W), lambda i: (0, 0)),
                  pl.BlockSpec((1, GMLP_W), lambda i: (0, 0)),
                  pl.BlockSpec((GMLP_GROUPS, GMLP_CHUNK, GMLP_CHUNK), lambda i: (0, 0, 0)),
                  pl.BlockSpec((GMLP_CHUNK, GMLP_GROUPS), lambda i: (0, 0))],
        out_specs=out_specs,
        out_shape=out_shape,
        compiler_params=_params("parallel"),
        name="gmlp_sgu",
    )(gugv, lw["ln_g"], lw["ln_b"], w_eff, b_eff)
    return res if emit_vn else (res[0], None)


def _merge_kernel(oa_ref, ob_ref, oc_ref, ga_ref, gb_ref, gc_ref, wa_ref, wb_ref, wc_ref, o_ref):
    m = _sigmoid(ga_ref[...].astype(F32)) * _dot(oa_ref[...], wa_ref[...])
    m = m + _sigmoid(gb_ref[...].astype(F32)) * _dot(ob_ref[...], wb_ref[...])
    m = m + _sigmoid(gc_ref[...].astype(F32)) * _dot(oc_ref[...], wc_ref[...])
    o_ref[...] = m.astype(o_ref.dtype)


def _merge(o_a, o_b, o_c, gates, w_a, w_b, w_c, layer):
    m, kw = o_a.shape
    d = w_a.shape[2]
    tm = _tile(m, 512)
    tn = _col_tile(d, 1024)
    nj = d // tn
    o_spec = pl.BlockSpec((tm, kw), lambda j, i: (i, 0))
    w_spec = pl.BlockSpec((None, kw, tn), lambda j, i: (layer, 0, j))
    return pl.pallas_call(
        _merge_kernel,
        grid=(nj, m // tm),
        in_specs=[o_spec, o_spec, o_spec,
                  pl.BlockSpec((tm, tn), lambda j, i: (i, j)),
                  pl.BlockSpec((tm, tn), lambda j, i: (i, nj + j)),
                  pl.BlockSpec((tm, tn), lambda j, i: (i, 2 * nj + j)),
                  w_spec, w_spec, w_spec],
        out_specs=pl.BlockSpec((tm, tn), lambda j, i: (i, j)),
        out_shape=jax.ShapeDtypeStruct((m, d), BF16),
        compiler_params=_params("parallel", "parallel"),
        name="branch_merge",
    )(o_a, o_b, o_c, gates, gates, gates, w_a, w_b, w_c)


def _out_proj_kernel(m_ref, w_ref, x_ref, g_ref, gffn_ref, o_ref, h_ref):
    x_new = x_ref[...] + _rms(_dot(m_ref[...], w_ref[...]), g_ref[...])
    o_ref[...] = x_new
    h_ref[...] = _rms(x_new, gffn_ref[...]).astype(h_ref.dtype)


def _out_proj(mm, w_out, layer, x, g, g_ffn):
    m, d = x.shape
    tm = _tile(m, 512)
    row = pl.BlockSpec((tm, d), lambda i: (i, 0))
    vec = pl.BlockSpec((1, d), lambda i: (0, 0))
    return pl.pallas_call(
        _out_proj_kernel,
        grid=(m // tm,),
        in_specs=[row, pl.BlockSpec((None, d, d), lambda i: (layer, 0, 0)), row, vec, vec],
        out_specs=[row, row],
        out_shape=[jax.ShapeDtypeStruct((m, d), F32), jax.ShapeDtypeStruct((m, d), BF16)],
        compiler_params=_params("parallel"),
        name="out_proj_residual",
    )(mm, w_out, x, g.reshape(1, d), g_ffn.reshape(1, d))


def _ffn_core_kernel(h_ref, w1_ref, w3_ref, w2_ref, f_ref, acc_ref):
    j = pl.program_id(1)

    @pl.when(j == 0)
    def _():
        acc_ref[...] = jnp.zeros_like(acc_ref)

    h = h_ref[...]
    a = _dot(h, w1_ref[...])
    b = _dot(h, w3_ref[...])
    gated = (a * _sigmoid(a)) * b
    acc_ref[...] += _dot(gated.astype(BF16), w2_ref[...])

    @pl.when(j == pl.num_programs(1) - 1)
    def _():
        f_ref[...] = acc_ref[...].astype(f_ref.dtype)


def _ffn_residual_kernel(f_ref, x_ref, gpost_ref, gnext_ref, o_ref, hn_ref):
    x_new = x_ref[...] + _rms(f_ref[...].astype(F32), gpost_ref[...])
    o_ref[...] = x_new
    hn_ref[...] = _rms(x_new, gnext_ref[...]).astype(hn_ref.dtype)


def _ffn(x, h, g_post, g_next, w1, w3, w2, layer):
    m, d = x.shape
    dff = w1.shape[2]
    tm = _tile(m, 1024)
    tf = _col_tile(dff, 512)
    row = pl.BlockSpec((tm, d), lambda i, j: (i, 0))
    f = pl.pallas_call(
        _ffn_core_kernel,
        grid=(m // tm, dff // tf),
        in_specs=[row,
                  pl.BlockSpec((None, d, tf), lambda i, j: (layer, 0, j)),
                  pl.BlockSpec((None, d, tf), lambda i, j: (layer, 0, j)),
                  pl.BlockSpec((None, tf, d), lambda i, j: (layer, j, 0))],
        out_specs=row,
        out_shape=jax.ShapeDtypeStruct((m, d), BF16),
        scratch_shapes=[pltpu.VMEM((tm, d), F32)],
        compiler_params=_params("parallel", "arbitrary"),
        name="ffn_swiglu",
    )(h, w1, w3, w2)
    tr = _tile(m, 512)
    rows = pl.BlockSpec((tr, d), lambda i: (i, 0))
    vec = pl.BlockSpec((1, d), lambda i: (0, 0))
    return pl.pallas_call(
        _ffn_residual_kernel,
        grid=(m // tr,),
        in_specs=[rows, rows, vec, vec],
        out_specs=[rows, rows],
        out_shape=[jax.ShapeDtypeStruct((m, d), F32), jax.ShapeDtypeStruct((m, d), BF16)],
        compiler_params=_params("parallel"),
        name="ffn_residual",
    )(f, x, g_post.reshape(1, d), g_next.reshape(1, d))


def _pad_rows(w, rows, offset):
    out = jnp.zeros((rows, w.shape[1]), w.dtype)
    return out.at[offset:offset + w.shape[0]].set(w)


_O_FF = 3 * FOX_W
_O_RWKV = _O_FF + FOX_HEADS
_O_GMLP = _O_RWKV + RWKV_IN_W
_O_GATE = _O_GMLP + 2 * GMLP_W


def _stacked_weights(w_in, w_br_fox, w_br_rwkv, w_br_gmlp, w_out, ffn_w1, ffn_w3, ffn_w2):
    return {
        "wq": w_in[:, :, :FOX_W].astype(BF16),
        "wk": w_in[:, :, FOX_W:2 * FOX_W].astype(BF16),
        "wv": w_in[:, :, 2 * FOX_W:_O_FF].astype(BF16),
        "w_rwkv": w_in[:, :, _O_RWKV:_O_GMLP].astype(BF16),
        "w_gmlp": w_in[:, :, _O_GMLP:_O_GATE].astype(BF16),
        "w_gate": w_in[:, :, _O_GATE:].astype(BF16),
        "w_br_fox": w_br_fox.astype(BF16),
        "w_br_rwkv": w_br_rwkv.astype(BF16),
        "w_br_gmlp": w_br_gmlp.astype(BF16),
        "w_out": w_out.astype(BF16),
        "ffn_w1": ffn_w1.astype(BF16),
        "ffn_w3": ffn_w3.astype(BF16),
        "ffn_w2": ffn_w2.astype(BF16),
    }


def _layer_weights(l, d, w_in, fox_bf, rwkv_mu, rwkv_w0, rwkv_w_up, rwkv_a0, rwkv_a_up, rwkv_g_up, rwkv_k_k,
                   rwkv_k_a, rwkv_r_k, rwkv_gn_g, rwkv_gn_b, gmlp_ln_g, gmlp_ln_b, gmlp_w_s, gmlp_b_s, s_new):
    tril_full = jnp.tril(gmlp_w_s[l])
    blk = jnp.tril(gmlp_w_s[l][:, :s_new, :s_new])
    reps = GMLP_CHUNK // s_new
    eye = jnp.eye(reps, dtype=F32)
    w_sample = jnp.einsum("ab,gij->gaibj", eye, blk).reshape(GMLP_GROUPS, GMLP_CHUNK, GMLP_CHUNK)
    return {
        "w_ff": jnp.zeros((d, LANE), BF16).at[:, :FOX_HEADS].set(w_in[l, :, _O_FF:_O_RWKV].astype(BF16)),
        "b_ff": jnp.zeros((1, LANE), F32).at[0, :FOX_HEADS].set(fox_bf[l]),
        "rwkv": {
            "mu": rwkv_mu[l].reshape(1, RWKV_IN_W),
            "w0": rwkv_w0[l].reshape(1, RWKV_W),
            "a0": rwkv_a0[l].reshape(1, RWKV_W),
            "k_k": rwkv_k_k[l].reshape(1, RWKV_W),
            "k_a": rwkv_k_a[l].reshape(1, RWKV_W),
            "r_k": rwkv_r_k[l].reshape(1, RWKV_W),
            "gn_g": rwkv_gn_g[l].reshape(1, RWKV_W),
            "gn_b": rwkv_gn_b[l].reshape(1, RWKV_W),
            "w_up": _pad_rows(rwkv_w_up[l], LANE, 0).astype(BF16),
            "a_up": _pad_rows(rwkv_a_up[l], LANE, rwkv_w_up.shape[1]).astype(BF16),
            "g_up": rwkv_g_up[l].astype(BF16),
        },
        "gmlp": {"ln_g": gmlp_ln_g[l].reshape(1, GMLP_W), "ln_b": gmlp_ln_b[l].reshape(1, GMLP_W)},
        "gmlp_w_prompt": tril_full.astype(BF16),
        "gmlp_b_prompt": gmlp_b_s[l].T,
        "gmlp_w_sample": w_sample.astype(BF16),
        "gmlp_b_sample": jnp.tile(gmlp_b_s[l][:, :s_new].T, (reps, 1)),
    }


def _trunk_layer(x, h, big, lw, post_mix_g, pre_ffn_g, post_ffn_g, next_mix_g, nseq, t, layer, cache, kv_stacks):
    m, d = x.shape
    q = _matmul(h, big["wq"], layer, (BF16,), scale=FOX_Q_SCALE, tn_pref=1024)
    k, k_bf = _kv_proj(h, big["wk"], layer, None if kv_stacks is None else kv_stacks[0])
    v, v_bf = _kv_proj(h, big["wv"], layer, None if kv_stacks is None else kv_stacks[1])
    rw = _matmul(h, big["w_rwkv"], layer, tn_pref=1664)
    gugv = _matmul(h, big["w_gmlp"], layer, (BF16,), tn_pref=1024)
    gates = _matmul(h, big["w_gate"], layer, (BF16,), tn_pref=1536)

    if cache is None:
        logf, xq, xk = _fox_forget(h, lw["w_ff"], lw["b_ff"], nseq, t)
        logf = logf[:, :FOX_HEADS]
        o_a = _fox_prompt(q, xq, k_bf, xk, v_bf, nseq, t)
        prev = jnp.zeros((nseq, 1, RWKV_IN_W), F32)
        state = None
        chunk = 64
        gm_w, gm_b = lw["gmlp_w_prompt"], lw["gmlp_b_prompt"]
    else:
        past = cache["k"].shape[2]
        total = past + t
        tpad = -(-total // LANE) * LANE
        logf = _forget_logits(h, lw["w_ff"], lw["b_ff"])[:, :FOX_HEADS]
        logf_rows = logf.reshape(nseq, t, FOX_HEADS).transpose(0, 2, 1)
        lf_all = jnp.concatenate([cache["logf"][layer].astype(F32).transpose(0, 2, 1), logf_rows], axis=-1)
        lf_all = jnp.pad(lf_all, ((0, 0), (0, 0), (0, tpad - total)))
        c_rows = _cumsum_lanes(lf_all.reshape(nseq * FOX_HEADS, tpad), LANE, LOG2E).reshape(nseq, FOX_HEADS, tpad)
        c_tm = c_rows[:, :, past:total].transpose(0, 2, 1).reshape(m, FOX_HEADS)
        o_a = _fox_sample(q, k_bf, v_bf, cache["k"], cache["v"], layer, c_tm, c_rows, nseq, t, past)
        prev = cache["shift"][layer]
        state = cache["state"]
        chunk = t
        gm_w, gm_b = lw["gmlp_w_sample"], lw["gmlp_b_sample"]

    r, lwd, k2, v2, kk, lr, g = _rwkv_prep(rw, prev, lw["rwkv"], nseq, t)
    o_b, s_new = _rwkv_chunks(r, lwd, k2, v2, kk, lr, g, lw["rwkv"], state, layer, nseq, t, chunk)
    shift_new = rw.reshape(nseq, t, RWKV_IN_W)[:, t - 1:t, :]

    o_c, vn = _gmlp(gugv, lw["gmlp"], gm_w, gm_b, cache is not None)

    mm = _merge(o_a, o_b, o_c, gates, big["w_br_fox"], big["w_br_rwkv"], big["w_br_gmlp"], layer)
    x, h_ffn = _out_proj(mm, big["w_out"], layer, x, post_mix_g, pre_ffn_g)
    x, h_next = _ffn(x, h_ffn, post_ffn_g, next_mix_g, big["ffn_w1"], big["ffn_w3"], big["ffn_w2"], layer)
    return x, h_next, (k, v, logf, s_new, shift_new, vn)


def kernel(x_prompt, x_sample, cache_fox_k, cache_fox_v, cache_fox_logf, state_rwkv, state_rwkv_shift, pre_mix_g, w_in, fox_bf, rwkv_mu, rwkv_w0, rwkv_w_up, rwkv_a0, rwkv_a_up, rwkv_g_up, rwkv_k_k, rwkv_k_a, rwkv_r_k, rwkv_gn_g, rwkv_gn_b, gmlp_ln_g, gmlp_ln_b, gmlp_w_s, gmlp_b_s, w_br_fox, w_br_rwkv, w_br_gmlp, w_out, post_mix_g, pre_ffn_g, ffn_w1, ffn_w3, ffn_w2, post_ffn_g):
    batch, seq, d = x_prompt.shape
    dec_batch, dec_seq, _ = x_sample.shape
    depth = w_in.shape[0]
    assert rwkv_w_up.shape[1] + rwkv_a_up.shape[1] == LANE and rwkv_g_up.shape[1] == LANE
    assert seq % GMLP_CHUNK == 0 and GMLP_CHUNK % dec_seq == 0 and (dec_batch * dec_seq) % GMLP_CHUNK == 0

    xp = x_prompt.reshape(batch * seq, d)
    xs = x_sample.reshape(dec_batch * dec_seq, d)
    cache = {"k": cache_fox_k, "v": cache_fox_v, "logf": cache_fox_logf,
             "state": state_rwkv, "shift": state_rwkv_shift}
    big = _stacked_weights(w_in, w_br_fox, w_br_rwkv, w_br_gmlp, w_out, ffn_w1, ffn_w3, ffn_w2)
    outs_p = [[] for _ in range(3)]
    outs_s = [[] for _ in range(6)]
    k_p = jnp.zeros((depth, batch * seq, FOX_HEADS, FOX_HD), F32)
    v_p = jnp.zeros((depth, batch * seq, FOX_HEADS, FOX_HD), F32)
    hp = _rmsnorm_cast(xp, pre_mix_g[0])
    hs = _rmsnorm_cast(xs, pre_mix_g[0])
    for l in range(depth):
        lw = _layer_weights(l, d, w_in, fox_bf, rwkv_mu, rwkv_w0, rwkv_w_up, rwkv_a0, rwkv_a_up, rwkv_g_up,
                            rwkv_k_k, rwkv_k_a, rwkv_r_k, rwkv_gn_g, rwkv_gn_b, gmlp_ln_g, gmlp_ln_b, gmlp_w_s,
                            gmlp_b_s, dec_seq)
        norms = (post_mix_g[l], pre_ffn_g[l], post_ffn_g[l], pre_mix_g[(l + 1) % depth])
        xp, hp, (k_p, v_p, lf_p, s_p, sh_p, _) = _trunk_layer(xp, hp, big, lw, *norms, batch, seq, l, None, (k_p, v_p))
        xs, hs, (k_s, v_s, lf_s, s_s, sh_s, vn_s) = _trunk_layer(xs, hs, big, lw, *norms, dec_batch, dec_seq, l, cache,
                                                                 None)
        for acc, val in zip(outs_p, (lf_p.reshape(batch, seq, FOX_HEADS), s_p, sh_p)):
            acc.append(val)
        for acc, val in zip(outs_s, (k_s.reshape(dec_batch, dec_seq, FOX_HEADS, FOX_HD),
                                     v_s.reshape(dec_batch, dec_seq, FOX_HEADS, FOX_HD),
                                     lf_s.reshape(dec_batch, dec_seq, FOX_HEADS), s_s, sh_s,
                                     vn_s.reshape(dec_batch, dec_seq, GMLP_W))):
            acc.append(val)
    return (xp.reshape(batch, seq, d), xs.reshape(dec_batch, dec_seq, d),
            k_p.reshape(depth, batch, seq, FOX_HEADS, FOX_HD), v_p.reshape(depth, batch, seq, FOX_HEADS, FOX_HD),
            *[jnp.stack(o) for o in outs_p], *[jnp.stack(o) for o in outs_s])
```

```python
import functools
import math

import jax
import jax.numpy as jnp
from jax import lax
from jax.experimental import pallas as pl
from jax.experimental.pallas import tpu as pltpu

F32 = jnp.float32
BF16 = jnp.bfloat16

RMS_EPS = 1e-6
LN_EPS = 1e-5
RWKV_GN_EPS = 64e-5
NEG_INF = -1e30
LOG2E = math.log2(math.e)

LANE = 128
FOX_HEADS = 8
FOX_HD = 128
FOX_W = FOX_HEADS * FOX_HD
FOX_Q_SCALE = FOX_HD ** -0.5 * LOG2E
RWKV_HEADS = 16
RWKV_HD = 64
RWKV_W = RWKV_HEADS * RWKV_HD
RWKV_PAIRS = RWKV_W // LANE
RWKV_LORA_OFF = 3 * RWKV_W
RWKV_IN_W = 3 * RWKV_W + 2 * LANE
GMLP_W = 1024
GMLP_GROUPS = 8
GMLP_GC = GMLP_W // GMLP_GROUPS
GMLP_CHUNK = 128
N_BRANCH = 3

VMEM_LIMIT = 48 * 1024 * 1024


def _params(*sem):
    return pltpu.CompilerParams(dimension_semantics=sem, vmem_limit_bytes=VMEM_LIMIT)


def _tile(n, pref):
    if n <= pref:
        return n
    t = pref
    while n % t:
        t -= 8
    return t


def _col_tile(n, pref):
    assert n % LANE == 0
    k = n // LANE
    best = 1
    for d in range(1, k + 1):
        if k % d == 0 and d * LANE <= pref:
            best = d
    return best * LANE


def _split3(x):
    hi = x.astype(BF16)
    r = x - hi.astype(F32)
    mid = r.astype(BF16)
    lo = (r - mid.astype(F32)).astype(BF16)
    return hi, mid, lo


_NN = (((1,), (0,)), ((), ()))
_NT = (((1,), (1,)), ((), ()))
_TN = (((0,), (0,)), ((), ()))


def _dot(a, b, dims=_NN):
    return lax.dot_general(a, b, dims, preferred_element_type=F32)


def _dot_exact_rhs(x, e):
    hi, mid, lo = _split3(x)
    return _dot(hi, e) + _dot(mid, e) + _dot(lo, e)


def _sigmoid(x):
    return 1.0 / (1.0 + jnp.exp(-x))


def _log_sigmoid(x):
    return jnp.minimum(x, 0.0) - jnp.log(1.0 + jnp.exp(-jnp.abs(x)))


def _gelu_tanh(x):
    return 0.5 * x * (1.0 + jnp.tanh(0.7978845608028654 * (x + 0.044715 * (x * x * x))))


def _rms(x, g):
    ms = jnp.mean(x * x, axis=-1, keepdims=True)
    return x * lax.rsqrt(ms + RMS_EPS) * g


def _rmsnorm_cast_kernel(x_ref, g_ref, o_ref):
    o_ref[...] = _rms(x_ref[...], g_ref[...]).astype(o_ref.dtype)


def _rmsnorm_cast(x, g):
    m, d = x.shape
    tm = _tile(m, 512)
    return pl.pallas_call(
        _rmsnorm_cast_kernel,
        grid=(m // tm,),
        in_specs=[pl.BlockSpec((tm, d), lambda i: (i, 0)),
                  pl.BlockSpec((1, d), lambda i: (0, 0))],
        out_specs=pl.BlockSpec((tm, d), lambda i: (i, 0)),
        out_shape=jax.ShapeDtypeStruct((m, d), BF16),
        compiler_params=_params("parallel"),
        name="rmsnorm_cast",
    )(x, g.reshape(1, d))


def _matmul_kernel(a_ref, w_ref, *o_refs, scale):
    acc = _dot(a_ref[...], w_ref[...])
    if scale is not None:
        acc = acc * scale
    for o_ref in o_refs:
        o_ref[...] = acc.astype(o_ref.dtype)


def _matmul(a, w, layer, out_dtypes=(F32,), scale=None, tm_pref=1024, tn_pref=512):
    m, k = a.shape
    n = w.shape[2]
    tm = _tile(m, tm_pref)
    tn = _col_tile(n, tn_pref)
    o_spec = pl.BlockSpec((tm, tn), lambda i, j: (i, j))
    res = pl.pallas_call(
        functools.partial(_matmul_kernel, scale=scale),
        grid=(m // tm, n // tn),
        in_specs=[pl.BlockSpec((tm, k), lambda i, j: (i, 0)),
                  pl.BlockSpec((None, k, tn), lambda i, j: (layer, 0, j))],
        out_specs=[o_spec] * len(out_dtypes),
        out_shape=[jax.ShapeDtypeStruct((m, n), dt) for dt in out_dtypes],
        compiler_params=_params("parallel", "parallel"),
        name="proj_matmul",
    )(a, w)
    return res[0] if len(out_dtypes) == 1 else res


def _kv_proj_kernel(a_ref, w_ref, *refs):
    o_ref, obf_ref = refs[-2:]
    acc = _dot(a_ref[...], w_ref[...])
    obf_ref[...] = acc.astype(obf_ref.dtype)
    o_ref[...] = pltpu.einshape("t(hd)->thd", acc, h=FOX_HEADS)


def _kv_proj(a, w, layer, stacked=None):
    m, k = a.shape
    tm = _tile(m, 512)
    in_specs = [pl.BlockSpec((tm, k), lambda i: (i, 0)),
                pl.BlockSpec((None, k, FOX_W), lambda i: (layer, 0, 0))]
    args = [a, w]
    if stacked is None:
        o_spec = pl.BlockSpec((tm, FOX_HEADS, FOX_HD), lambda i: (i, 0, 0))
        o_shape = jax.ShapeDtypeStruct((m, FOX_HEADS, FOX_HD), F32)
        aliases = {}
    else:
        in_specs.append(pl.BlockSpec(memory_space=pl.ANY))
        args.append(stacked)
        o_spec = pl.BlockSpec((None, tm, FOX_HEADS, FOX_HD), lambda i: (layer, i, 0, 0))
        o_shape = jax.ShapeDtypeStruct(stacked.shape, F32)
        aliases = {2: 0}
    return pl.pallas_call(
        _kv_proj_kernel,
        grid=(m // tm,),
        in_specs=in_specs,
        out_specs=[o_spec, pl.BlockSpec((tm, FOX_W), lambda i: (i, 0))],
        out_shape=[o_shape, jax.ShapeDtypeStruct((m, FOX_W), BF16)],
        input_output_aliases=aliases,
        compiler_params=_params("parallel"),
        name="kv_proj",
    )(*args)


def _logf_kernel(a_ref, w_ref, b_ref, o_ref):
    o_ref[...] = _log_sigmoid(_dot(a_ref[...], w_ref[...]) + b_ref[...])


def _forget_logits(h, w_ff, b_ff):
    m, k = h.shape
    tm = _tile(m, 1024)
    return pl.pallas_call(
        _logf_kernel,
        grid=(m // tm,),
        in_specs=[pl.BlockSpec((tm, k), lambda i: (i, 0)),
                  pl.BlockSpec((k, LANE), lambda i: (0, 0)),
                  pl.BlockSpec((1, LANE), lambda i: (0, 0))],
        out_specs=pl.BlockSpec((tm, LANE), lambda i: (i, 0)),
        out_shape=jax.ShapeDtypeStruct((m, LANE), F32),
        compiler_params=_params("parallel"),
        name="forget_logits",
    )(h, w_ff, b_ff)


def _cumsum_kernel(x_ref, o_ref, carry_ref, *, tb, scale):
    @pl.when(pl.program_id(0) == 0)
    def _():
        carry_ref[...] = jnp.zeros_like(carry_ref)

    row = lax.broadcasted_iota(jnp.int32, (tb, tb), 0)
    col = lax.broadcasted_iota(jnp.int32, (tb, tb), 1)
    upper = jnp.where(row <= col, 1.0, 0.0).astype(BF16)
    c = _dot_exact_rhs(x_ref[...], upper) + carry_ref[:, :1]
    o_ref[...] = c * scale
    carry_ref[...] = jnp.broadcast_to(c[:, tb - 1:tb], carry_ref.shape)


def _cumsum_lanes(x, tb, scale):
    r, t = x.shape
    return pl.pallas_call(
        functools.partial(_cumsum_kernel, tb=tb, scale=scale),
        grid=(t // tb,),
        in_specs=[pl.BlockSpec((r, tb), lambda i: (0, i))],
        out_specs=pl.BlockSpec((r, tb), lambda i: (0, i)),
        out_shape=jax.ShapeDtypeStruct((r, t), F32),
        scratch_shapes=[pltpu.VMEM((r, LANE), F32)],
        compiler_params=_params("arbitrary"),
        name="logf_cumsum",
    )(x)


def _fox_bias_placement():
    part = jnp.arange(3 * LANE) // LANE
    head = jnp.arange(3 * LANE) % LANE
    out_head = jnp.arange(FOX_W) // FOX_HD
    out_lane = jnp.arange(FOX_W) % FOX_HD
    same_head = (head[:, None] == out_head[None, :]) & (head[:, None] < FOX_HEADS)
    pq = jnp.where(same_head & (out_lane[None, :] == part[:, None]), 1.0, 0.0).astype(BF16)
    pk = jnp.where(same_head & (out_lane[None, :] == part[:, None] + 3), -1.0, 0.0).astype(BF16)
    one_q = jnp.where((out_lane >= 3) & (out_lane < 6), 1.0, 0.0).reshape(1, FOX_W)
    one_k = jnp.where(out_lane < 3, 1.0, 0.0).reshape(1, FOX_W)
    return pq, pk, one_q, one_k


def _fox_forget_kernel(h_ref, w_ref, b_ref, pq_ref, pk_ref, oneq_ref, onek_ref, logf_ref, xq_ref, xk_ref, carry_ref,
                       *, tb):
    @pl.when(pl.program_id(1) == 0)
    def _():
        carry_ref[...] = jnp.zeros_like(carry_ref)

    logf = _log_sigmoid(_dot(h_ref[...], w_ref[...]) + b_ref[...])
    logf_ref[...] = logf
    row = lax.broadcasted_iota(jnp.int32, (tb, tb), 0)
    col = lax.broadcasted_iota(jnp.int32, (tb, tb), 1)
    lower = jnp.where(col <= row, 1.0, 0.0).astype(BF16)
    hi, mid, lo = _split3(logf)
    c = _dot(lower, hi) + _dot(lower, mid) + _dot(lower, lo) + carry_ref[:1, :]
    carry_ref[...] = jnp.broadcast_to(c[tb - 1:tb, :], carry_ref.shape)
    parts = jnp.concatenate(_split3(c * LOG2E), axis=1)
    xq_ref[...] = (_dot(parts, pq_ref[...]) + oneq_ref[...]).astype(BF16)
    xk_ref[...] = (_dot(parts, pk_ref[...]) + onek_ref[...]).astype(BF16)


def _fox_forget(h, w_ff, b_ff, nseq, t):
    m, k = h.shape
    tb = _tile(t, 256)
    nb = t // tb
    rows = lambda s, i: (s * nb + i, 0)
    const = lambda s, i: (0, 0)
    return pl.pallas_call(
        functools.partial(_fox_forget_kernel, tb=tb),
        grid=(nseq, nb),
        in_specs=[pl.BlockSpec((tb, k), rows),
                  pl.BlockSpec((k, LANE), const),
                  pl.BlockSpec((1, LANE), const),
                  pl.BlockSpec((3 * LANE, FOX_W), const),
                  pl.BlockSpec((3 * LANE, FOX_W), const),
                  pl.BlockSpec((1, FOX_W), const),
                  pl.BlockSpec((1, FOX_W), const)],
        out_specs=[pl.BlockSpec((tb, LANE), rows), pl.BlockSpec((tb, FOX_W), rows), pl.BlockSpec((tb, FOX_W), rows)],
        out_shape=[jax.ShapeDtypeStruct((m, LANE), F32),
                   jax.ShapeDtypeStruct((m, FOX_W), BF16),
                   jax.ShapeDtypeStruct((m, FOX_W), BF16)],
        scratch_shapes=[pltpu.VMEM((8, LANE), F32)],
        compiler_params=_params("parallel", "arbitrary"),
        name="fox_forget",
    )(h, w_ff, b_ff, *_fox_bias_placement())


FOX_KEYS_FULL, FOX_KEYS_DIAG_HALF, FOX_KEYS_FULL_DIAG = 0, 1, 2
FOX_HEAD_GROUP = 4


def _fox_prompt_kernel(qi_ref, kj_ref, kind_ref, q_ref, xq_ref, k_ref, xk_ref, v_ref, o_ref, m_ref, l_ref, acc_ref,
                       *, tq):
    step = pl.program_id(1)
    kj = kj_ref[step]
    kind = kind_ref[step]

    @pl.when(kj == 0)
    def _():
        m_ref[...] = jnp.full_like(m_ref, NEG_INF)
        l_ref[...] = jnp.zeros_like(l_ref)
        acc_ref[...] = jnp.zeros_like(acc_ref)

    def update(keys_kind):
        nk = tq if keys_kind == FOX_KEYS_DIAG_HALF else 2 * tq
        keys = slice(0, nk)
        if keys_kind != FOX_KEYS_FULL:
            row = lax.broadcasted_iota(jnp.int32, (tq, nk), 0)
            col = lax.broadcasted_iota(jnp.int32, (tq, nk), 1)
            causal = col <= row + (nk - tq)
        ones = jnp.ones((nk, LANE), BF16)
        for g0 in range(0, FOX_HEADS, FOX_HEAD_GROUP):
            heads = range(g0, g0 + FOX_HEAD_GROUP)
            cols = [slice(h * FOX_HD, (h + 1) * FOX_HD) for h in heads]
            s = [_dot(jnp.concatenate([q_ref[:, hs], xq_ref[:, hs]], axis=1),
                      jnp.concatenate([k_ref[keys, hs], xk_ref[keys, hs]], axis=1), _NT) for hs in cols]
            if keys_kind != FOX_KEYS_FULL:
                s = [jnp.where(causal, x, NEG_INF) for x in s]
            m_prev = [m_ref[h] for h in heads]
            m_new = [jnp.maximum(mp, jnp.max(x, axis=-1, keepdims=True)) for mp, x in zip(m_prev, s)]
            p = [jnp.exp2((x - jnp.concatenate([mn] * (nk // LANE), axis=1)).astype(BF16)) for x, mn in zip(s, m_new)]
            pv1 = [_dot(x, jnp.concatenate([v_ref[keys, hs], ones], axis=1)) for x, hs in zip(p, cols)]
            for i, h in enumerate(heads):
                alpha = jnp.exp2(m_prev[i] - m_new[i])
                l_ref[h] = alpha * l_ref[h] + pv1[i][:, LANE:]
                acc_ref[:, cols[i]] = alpha * acc_ref[:, cols[i]] + pv1[i][:, :LANE]
                m_ref[h] = m_new[i]

    def finish():
        for h in range(FOX_HEADS):
            hs = slice(h * FOX_HD, (h + 1) * FOX_HD)
            o_ref[:, hs] = (acc_ref[:, hs] / l_ref[h]).astype(o_ref.dtype)

    @pl.when(kind == FOX_KEYS_FULL)
    def _():
        update(FOX_KEYS_FULL)

    @pl.when(kind == FOX_KEYS_DIAG_HALF)
    def _():
        update(FOX_KEYS_DIAG_HALF)
        finish()

    @pl.when(kind == FOX_KEYS_FULL_DIAG)
    def _():
        update(FOX_KEYS_FULL_DIAG)
        finish()


def _fox_prompt(q, xq, k, xk, v, batch, seq):
    tq = _tile(seq // 2, 512)
    assert tq % LANE == 0 and seq % (2 * tq) == 0
    nq = seq // tq
    blocks = []
    for qi in range(nq):
        last = qi // 2
        blocks += [(qi, kj, FOX_KEYS_FULL) for kj in range(last)]
        blocks.append((qi, last, FOX_KEYS_DIAG_HALF if qi % 2 == 0 else FOX_KEYS_FULL_DIAG))
    tabs = [jnp.asarray([b[i] for b in blocks], jnp.int32) for i in range(3)]
    q_map = lambda b, s, qt, kt, kinds: (b * nq + qt[s], 0)
    kv_map = lambda b, s, qt, kt, kinds: (b * (nq // 2) + kt[s], 0)
    return pl.pallas_call(
        functools.partial(_fox_prompt_kernel, tq=tq),
        grid_spec=pltpu.PrefetchScalarGridSpec(
            num_scalar_prefetch=3,
            grid=(batch, len(blocks)),
            in_specs=[pl.BlockSpec((tq, FOX_W), q_map),
                      pl.BlockSpec((tq, FOX_W), q_map),
                      pl.BlockSpec((2 * tq, FOX_W), kv_map),
                      pl.BlockSpec((2 * tq, FOX_W), kv_map),
                      pl.BlockSpec((2 * tq, FOX_W), kv_map)],
            out_specs=pl.BlockSpec((tq, FOX_W), q_map),
            scratch_shapes=[pltpu.VMEM((FOX_HEADS, tq, LANE), F32),
                            pltpu.VMEM((FOX_HEADS, tq, LANE), F32),
                            pltpu.VMEM((tq, FOX_W), F32)]),
        out_shape=jax.ShapeDtypeStruct((batch * seq, FOX_W), BF16),
        compiler_params=_params("parallel", "arbitrary"),
        name="fox_prompt",
    )(*tabs, q, xq, k, xk, v)


def _fox_sample_kernel(q_ref, kn_ref, vn_ref, cq_ref, ck_ref, kc_ref, vc_ref, o_ref, *, past, s_new):
    kc = pltpu.einshape("thd->htd", kc_ref[0, 0])
    vc = pltpu.einshape("thd->htd", vc_ref[0, 0])
    row = lax.broadcasted_iota(jnp.int32, (s_new, s_new), 0)
    col = lax.broadcasted_iota(jnp.int32, (s_new, s_new), 1)
    causal = col <= row
    heads = range(FOX_HEADS)
    cols = [slice(h * FOX_HD, (h + 1) * FOX_HD) for h in heads]
    cq = [cq_ref[:, h:h + 1] for h in heads]
    s_c = [_dot(q_ref[:, hs], kc[h].astype(BF16), _NT) + (cq[h] - ck_ref[0, h:h + 1, :past])
           for h, hs in zip(heads, cols)]
    s_n = [jnp.where(causal, _dot(q_ref[:, hs], kn_ref[:, hs], _NT)
                     + (cq[h] - ck_ref[0, h:h + 1, past:past + s_new]), NEG_INF) for h, hs in zip(heads, cols)]
    m = [jnp.maximum(jnp.max(x, axis=-1, keepdims=True), jnp.max(y, axis=-1, keepdims=True)) for x, y in zip(s_c, s_n)]
    p_c = [jnp.exp2(x - mx) for x, mx in zip(s_c, m)]
    p_n = [jnp.exp2(x - mx) for x, mx in zip(s_n, m)]
    l = [jnp.sum(x, axis=-1, keepdims=True) + jnp.sum(y, axis=-1, keepdims=True) for x, y in zip(p_c, p_n)]
    o = [_dot(x.astype(BF16), vc[h].astype(BF16)) + _dot(y.astype(BF16), vn_ref[:, hs])
         for h, hs, x, y in zip(heads, cols, p_c, p_n)]
    for h in heads:
        o_ref[:, cols[h]] = (o[h] / l[h]).astype(o_ref.dtype)


def _fox_sample(q, k, v, k_cache, v_cache, layer, c_tm, c_rows, batch, s_new, past):
    tpad = c_rows.shape[-1]
    new_spec = pl.BlockSpec((s_new, FOX_W), lambda b: (b, 0))
    cache_spec = pl.BlockSpec((1, 1, past, FOX_HEADS, FOX_HD), lambda b: (layer, b, 0, 0, 0))
    return pl.pallas_call(
        functools.partial(_fox_sample_kernel, past=past, s_new=s_new),
        grid=(batch,),
        in_specs=[new_spec, new_spec, new_spec,
                  pl.BlockSpec((s_new, FOX_HEADS), lambda b: (b, 0)),
                  pl.BlockSpec((1, FOX_HEADS, tpad), lambda b: (b, 0, 0)), cache_spec, cache_spec],
        out_specs=new_spec,
        out_shape=jax.ShapeDtypeStruct((batch * s_new, FOX_W), BF16),
        compiler_params=_params("parallel"),
        name="fox_sample",
    )(q, k, v, c_tm, c_rows, k_cache, v_cache)


def _rwkv_prep_kernel(p_ref, prev_ref, mu_ref, w0_ref, a0_ref, kk_ref, ka_ref, wup_ref, aup_ref, gup_ref,
                      r_ref, lw_ref, k_ref, v_ref, kk_out_ref, a_ref, g_ref, carry_ref, *, tr):
    @pl.when(pl.program_id(1) == 0)
    def _():
        carry_ref[...] = prev_ref[0]

    p = p_ref[0]
    rows = lax.broadcasted_iota(jnp.int32, (tr, 1), 0)
    p_prev = jnp.where(rows == 0, carry_ref[...], pltpu.roll(p, shift=1, axis=0))
    carry_ref[...] = p[tr - 1:tr, :]
    xs = p + (p_prev - p) * mu_ref[...]

    r = xs[:, :RWKV_W]
    k = xs[:, RWKV_W:2 * RWKV_W]
    v = xs[:, 2 * RWKV_W:3 * RWKV_W]
    d_wa = xs[:, RWKV_LORA_OFF:RWKV_LORA_OFF + LANE]
    d_g = xs[:, RWKV_LORA_OFF + LANE:RWKV_LORA_OFF + 2 * LANE]

    z_w = w0_ref[...] + _dot(jnp.tanh(d_wa).astype(BF16), wup_ref[...])
    log_decay = -jnp.exp(_log_sigmoid(z_w) - 0.5)
    a = _sigmoid(a0_ref[...] + _dot(d_wa.astype(BF16), aup_ref[...]))
    g = _dot(_sigmoid(d_g).astype(BF16), gup_ref[...])

    r_ref[...] = r.astype(r_ref.dtype)
    lw_ref[...] = log_decay
    k_ref[...] = (k * (1.0 + (a - 1.0) * ka_ref[...])).astype(k_ref.dtype)
    v_ref[...] = v.astype(v_ref.dtype)
    kk_out_ref[...] = (k * kk_ref[...]).astype(kk_out_ref.dtype)
    a_ref[...] = a.astype(a_ref.dtype)
    g_ref[...] = g.astype(g_ref.dtype)


def _rwkv_prep(rw, prev, lw, nseq, t):
    tr = _tile(t, 256)
    nb = t // tr
    wide = lambda s, i: (0, 0)
    row_spec = pl.BlockSpec((tr, RWKV_W), lambda s, i: (s * nb + i, 0))
    out_dtypes = (BF16, F32, BF16, BF16, BF16, BF16, BF16)
    return pl.pallas_call(
        functools.partial(_rwkv_prep_kernel, tr=tr),
        grid=(nseq, nb),
        in_specs=[pl.BlockSpec((1, tr, RWKV_IN_W), lambda s, i: (s, i, 0)),
                  pl.BlockSpec((1, 1, RWKV_IN_W), lambda s, i: (s, 0, 0)),
                  pl.BlockSpec((1, RWKV_IN_W), wide),
                  pl.BlockSpec((1, RWKV_W), wide),
                  pl.BlockSpec((1, RWKV_W), wide),
                  pl.BlockSpec((1, RWKV_W), wide),
                  pl.BlockSpec((1, RWKV_W), wide),
                  pl.BlockSpec((LANE, RWKV_W), wide),
                  pl.BlockSpec((LANE, RWKV_W), wide),
                  pl.BlockSpec((LANE, RWKV_W), wide)],
        out_specs=[row_spec] * 7,
        out_shape=[jax.ShapeDtypeStruct((nseq * t, RWKV_W), dt) for dt in out_dtypes],
        scratch_shapes=[pltpu.VMEM((1, RWKV_IN_W), F32)],
        compiler_params=_params("parallel", "arbitrary"),
        name="rwkv_prep",
    )(rw.reshape(nseq, t, RWKV_IN_W), prev, lw["mu"], lw["w0"], lw["a0"], lw["k_k"], lw["k_a"],
      lw["w_up"], lw["a_up"], lw["g_up"])


RWKV_CHUNKS_PER_STEP = 2


def _rwkv_chunk_kernel(*refs, c, has_s0):
    r_ref, lw_ref, k_ref, v_ref, kk_ref, lr_ref, g_ref, gng_ref, gnb_ref, rk_ref = refs[:10]
    s0_ref = refs[10] if has_s0 else None
    o_ref, sout_ref, s_ref = refs[-3:]
    ci = pl.program_id(1)
    pairs = range(RWKV_PAIRS)

    @pl.when(ci == 0)
    def _():
        if has_s0:
            zero = jnp.zeros((RWKV_HD, RWKV_HD), F32)
            for pr in pairs:
                top = jnp.concatenate([s0_ref[0, 0, 2 * pr], zero], axis=1)
                bot = jnp.concatenate([zero, s0_ref[0, 0, 2 * pr + 1]], axis=1)
                s_ref[pr] = jnp.concatenate([top, bot], axis=0)
        else:
            s_ref[...] = jnp.zeros_like(s_ref)

    c2 = 2 * c
    row = lax.broadcasted_iota(jnp.int32, (c2, c2), 0)
    col = lax.broadcasted_iota(jnp.int32, (c2, c2), 1)
    lower = col <= row
    strict = col < row
    eye = jnp.where(row == col, 1.0, 0.0)
    rc = lax.broadcasted_iota(jnp.int32, (c, c), 0)
    cc = lax.broadcasted_iota(jnp.int32, (c, c), 1)
    tri = jnp.where(cc <= rc, 1.0, 0.0).astype(BF16)
    lane = lax.broadcasted_iota(jnp.int32, (c, LANE), 1)
    first = lane < RWKV_HD
    steps = max(c.bit_length() - 2, 0)

    def stack(x):
        return jnp.concatenate([jnp.where(first, x, 0.0), jnp.where(first, 0.0, x)], axis=0)

    own = (lax.broadcasted_iota(jnp.int32, (c2, LANE), 1) // RWKV_HD
           == lax.broadcasted_iota(jnp.int32, (c2, LANE), 0) // c)

    n_sub = r_ref.shape[0] // c
    lanes = [slice(pr * LANE, (pr + 1) * LANE) for pr in pairs]
    units = [(slice(sb * c, (sb + 1) * c), ls) for sb in range(n_sub) for ls in lanes]
    lws = [lw_ref[rows, ls] for rows, ls in units]
    cum3 = [_dot(tri, jnp.concatenate(_split3(lw), axis=1)) for lw in lws]
    cums = [x[:, :LANE] + x[:, LANE:2 * LANE] + x[:, 2 * LANE:] for x in cum3]
    totals = [cum[c - 1:c, :] for cum in cums]

    lhs, rhs, bar, v_s, rk_s = [], [], [], [], []
    for (rows, ls), lw, cum, total in zip(units, lws, cums, totals):
        grow = jnp.exp(-cum)
        rest = jnp.exp(total - cum)
        r = r_ref[rows, ls].astype(F32)
        k = k_ref[rows, ls].astype(F32)
        kk = kk_ref[rows, ls].astype(F32)
        norm = lax.rsqrt(jnp.maximum(jnp.sum(stack(kk * kk), axis=-1, keepdims=True), 1e-24))
        kk = kk * jnp.where(first, norm[:c], norm[c:])
        b = kk * lr_ref[rows, ls].astype(F32)
        lhs.append(jnp.concatenate([stack(-kk * jnp.exp(cum - lw)), stack(r * jnp.exp(cum))], axis=0).astype(BF16))
        rhs.append(jnp.concatenate([stack(k * grow), stack(b * grow)], axis=0).astype(BF16))
        bar.append(jnp.concatenate([stack(k * rest), stack(b * rest)], axis=0).astype(BF16))
        v_s.append(stack(v_ref[rows, ls].astype(F32)))
        rk_s.append(stack(r * k * rk_ref[:, ls]))

    amats = [_dot(x, y, _NT) for x, y in zip(lhs, rhs)]
    a_ab = [jnp.where(strict, m[:c2, c2:], 0.0) for m in amats]
    invs = [eye + n for n in a_ab]
    powers = [_dot(n.astype(BF16), n.astype(BF16)) for n in a_ab]
    for i in range(steps):
        if i < steps - 1:
            xs = [_dot(jnp.concatenate([inv, pw], axis=0).astype(BF16), pw.astype(BF16))
                  for inv, pw in zip(invs, powers)]
            invs = [inv + x[:c2] for inv, x in zip(invs, xs)]
            powers = [x[c2:] for x in xs]
        else:
            invs = [inv + _dot(inv.astype(BF16), pw.astype(BF16)) for inv, pw in zip(invs, powers)]

    akv = [_dot(jnp.where(strict, m[:c2, :c2], 0.0).astype(BF16), v.astype(BF16)) for m, v in zip(amats, v_s)]
    a_rs = [jnp.concatenate([jnp.where(lower, m[c2:, :c2], 0.0), jnp.where(lower, m[c2:, c2:], 0.0)],
                            axis=1).astype(BF16) for m in amats]

    s_mats = [s_ref[pr] for pr in pairs]
    for sb in range(n_sub):
        base = sb * RWKV_PAIRS
        sxs = [_dot(lhs[base + pr], s_mats[pr].astype(BF16), _NT) for pr in pairs]
        us = [_dot(invs[base + pr].astype(BF16), (sxs[pr][:c2] + akv[base + pr]).astype(BF16)) for pr in pairs]
        vus = [jnp.concatenate([v_s[base + pr], us[pr]], axis=0).astype(BF16) for pr in pairs]
        for pr in pairs:
            i = base + pr
            rows, ls = units[i]
            y_s = sxs[pr][c2:] + _dot(a_rs[i], vus[pr])
            s_mats[pr] = s_mats[pr] * jnp.exp(totals[i]) + _dot(vus[pr], bar[i], _TN)
            mu = jnp.sum(y_s, axis=-1, keepdims=True) * (1.0 / RWKV_HD)
            d = jnp.where(own, y_s - mu, 0.0)
            var = jnp.sum(d * d, axis=-1, keepdims=True) * (1.0 / RWKV_HD)
            yn_s = d * lax.rsqrt(var + RWKV_GN_EPS)
            bonus_s = jnp.sum(rk_s[i], axis=-1, keepdims=True) * v_s[i]
            yn = (yn_s[:c] + yn_s[c:]) * gng_ref[:, ls] + gnb_ref[:, ls]
            o_ref[rows, ls] = ((yn + (bonus_s[:c] + bonus_s[c:])) * g_ref[rows, ls].astype(F32)).astype(o_ref.dtype)
    for pr in pairs:
        s_ref[pr] = s_mats[pr]

    @pl.when(ci == pl.num_programs(1) - 1)
    def _():
        for pr in pairs:
            s_mat = s_ref[pr]
            sout_ref[0, 2 * pr] = s_mat[:RWKV_HD, :RWKV_HD]
            sout_ref[0, 2 * pr + 1] = s_mat[RWKV_HD:, RWKV_HD:]


def _rwkv_chunks(r, lwd, k, v, kk, lr, g, lw, state, layer, nseq, t, c):
    rows = c * RWKV_CHUNKS_PER_STEP if t % (c * RWKV_CHUNKS_PER_STEP) == 0 else c
    nc = t // rows
    row_spec = pl.BlockSpec((rows, RWKV_W), lambda s, i: (s * nc + i, 0))
    par_spec = pl.BlockSpec((1, RWKV_W), lambda s, i: (0, 0))
    in_specs = [row_spec] * 7 + [par_spec] * 3
    args = [r, lwd, k, v, kk, lr, g, lw["gn_g"], lw["gn_b"], lw["r_k"]]
    if state is not None:
        in_specs.append(pl.BlockSpec((1, 1, RWKV_HEADS, RWKV_HD, RWKV_HD), lambda s, i: (layer, s, 0, 0, 0)))
        args.append(state)
    return pl.pallas_call(
        functools.partial(_rwkv_chunk_kernel, c=c, has_s0=state is not None),
        grid=(nseq, nc),
        in_specs=in_specs,
        out_specs=[row_spec, pl.BlockSpec((1, RWKV_HEADS, RWKV_HD, RWKV_HD), lambda s, i: (s, 0, 0, 0))],
        out_shape=[jax.ShapeDtypeStruct((nseq * t, RWKV_W), BF16),
                   jax.ShapeDtypeStruct((nseq, RWKV_HEADS, RWKV_HD, RWKV_HD), F32)],
        scratch_shapes=[pltpu.VMEM((RWKV_PAIRS, LANE, LANE), F32)],
        compiler_params=_params("parallel", "arbitrary"),
        name="rwkv_chunks",
    )(*args)


def _gmlp_kernel(gugv_ref, lng_ref, lnb_ref, ws_ref, bs_ref, o_ref, *vn_out):
    for blk in range(o_ref.shape[0] // GMLP_CHUNK):
        rows = slice(blk * GMLP_CHUNK, (blk + 1) * GMLP_CHUNK)
        u = _gelu_tanh(gugv_ref[rows, :GMLP_W].astype(F32))
        gv = _gelu_tanh(gugv_ref[rows, GMLP_W:].astype(F32))
        mu = jnp.mean(gv, axis=-1, keepdims=True)
        d = gv - mu
        var = jnp.mean(d * d, axis=-1, keepdims=True)
        vn = d * lax.rsqrt(var + LN_EPS) * lng_ref[...] + lnb_ref[...]
        if vn_out:
            vn_out[0][rows, :] = vn
        vb = vn.astype(BF16)
        for g in range(GMLP_GROUPS):
            gs = slice(g * GMLP_GC, (g + 1) * GMLP_GC)
            s = _dot(ws_ref[g], vb[:, gs]) + bs_ref[:, g:g + 1]
            o_ref[rows, gs] = (u[:, gs] * s).astype(o_ref.dtype)


def _gmlp(gugv, lw, w_eff, b_eff, emit_vn):
    m = gugv.shape[0]
    tr = _tile(m, 4 * GMLP_CHUNK)
    assert tr % GMLP_CHUNK == 0
    row_spec = pl.BlockSpec((tr, GMLP_W), lambda i: (i, 0))
    out_shape = [jax.ShapeDtypeStruct((m, GMLP_W), BF16)]
    out_specs = [row_spec]
    if emit_vn:
        out_shape.append(jax.ShapeDtypeStruct((m, GMLP_W), F32))
        out_specs.append(row_spec)
    res = pl.pallas_call(
        _gmlp_kernel,
        grid=(m // tr,),
        in_specs=[pl.BlockSpec((tr, 2 * GMLP_W), lambda i: (i, 0)),
                  pl.BlockSpec((1, GMLP_W), lambda i: (0, 0)),
                  pl.BlockSpec((1, GMLP_W), lambda i: (0, 0)),
                  pl.BlockSpec((GMLP_GROUPS, GMLP_CHUNK, GMLP_CHUNK), lambda i: (0, 0, 0)),
                  pl.BlockSpec((GMLP_CHUNK, GMLP_GROUPS), lambda i: (0, 0))],
        out_specs=out_specs,
        out_shape=out_shape,
        compiler_params=_params("parallel"),
        name="gmlp_sgu",
    )(gugv, lw["ln_g"], lw["ln_b"], w_eff, b_eff)
    return res if emit_vn else (res[0], None)


def _merge_kernel(oa_ref, ob_ref, oc_ref, ga_ref, gb_ref, gc_ref, wa_ref, wb_ref, wc_ref, o_ref):
    m = _sigmoid(ga_ref[...].astype(F32)) * _dot(oa_ref[...], wa_ref[...])
    m = m + _sigmoid(gb_ref[...].astype(F32)) * _dot(ob_ref[...], wb_ref[...])
    m = m + _sigmoid(gc_ref[...].astype(F32)) * _dot(oc_ref[...], wc_ref[...])
    o_ref[...] = m.astype(o_ref.dtype)


def _merge(o_a, o_b, o_c, gates, w_a, w_b, w_c, layer):
    m, kw = o_a.shape
    d = w_a.shape[2]
    tm = _tile(m, 512)
    tn = _col_tile(d, 1024)
    nj = d // tn
    o_spec = pl.BlockSpec((tm, kw), lambda j, i: (i, 0))
    w_spec = pl.BlockSpec((None, kw, tn), lambda j, i: (layer, 0, j))
    return pl.pallas_call(
        _merge_kernel,
        grid=(nj, m // tm),
        in_specs=[o_spec, o_spec, o_spec,
                  pl.BlockSpec((tm, tn), lambda j, i: (i, j)),
                  pl.BlockSpec((tm, tn), lambda j, i: (i, nj + j)),
                  pl.BlockSpec((tm, tn), lambda j, i: (i, 2 * nj + j)),
                  w_spec, w_spec, w_spec],
        out_specs=pl.BlockSpec((tm, tn), lambda j, i: (i, j)),
        out_shape=jax.ShapeDtypeStruct((m, d), BF16),
        compiler_params=_params("parallel", "parallel"),
        name="branch_merge",
    )(o_a, o_b, o_c, gates, gates, gates, w_a, w_b, w_c)


def _out_proj_kernel(m_ref, w_ref, x_ref, g_ref, o_ref):
    o_ref[...] = x_ref[...] + _rms(_dot(m_ref[...], w_ref[...]), g_ref[...])


def _out_proj(mm, w_out, layer, x, g):
    m, d = x.shape
    tm = _tile(m, 512)
    return pl.pallas_call(
        _out_proj_kernel,
        grid=(m // tm,),
        in_specs=[pl.BlockSpec((tm, d), lambda i: (i, 0)),
                  pl.BlockSpec((None, d, d), lambda i: (layer, 0, 0)),
                  pl.BlockSpec((tm, d), lambda i: (i, 0)),
                  pl.BlockSpec((1, d), lambda i: (0, 0))],
        out_specs=pl.BlockSpec((tm, d), lambda i: (i, 0)),
        out_shape=jax.ShapeDtypeStruct((m, d), F32),
        compiler_params=_params("parallel"),
        name="out_proj_residual",
    )(mm, w_out, x, g.reshape(1, d))


FFN_VMEM_LIMIT = 56 * 1024 * 1024


def _ffn_kernel(x_ref, gpre_ref, gpost_ref, gnext_ref, w1_ref, w3_ref, w2_ref, o_ref, hn_ref, h_ref):
    j = pl.program_id(1)

    @pl.when(j == 0)
    def _():
        h_ref[...] = _rms(x_ref[...], gpre_ref[...]).astype(BF16)
        o_ref[...] = jnp.zeros_like(o_ref)

    h = h_ref[...]
    a = _dot(h, w1_ref[...])
    b = _dot(h, w3_ref[...])
    gated = (a * _sigmoid(a)) * b
    o_ref[...] += _dot(gated.astype(BF16), w2_ref[...])

    @pl.when(j == pl.num_programs(1) - 1)
    def _():
        o_ref[...] = x_ref[...] + _rms(o_ref[...], gpost_ref[...])
        hn_ref[...] = _rms(o_ref[...], gnext_ref[...]).astype(hn_ref.dtype)


def _ffn(x, g_pre, g_post, g_next, w1, w3, w2, layer):
    m, d = x.shape
    dff = w1.shape[2]
    tm = _tile(m, 1024)
    tf = _col_tile(dff, 512)
    vec = pl.BlockSpec((1, d), lambda i, j: (0, 0))
    row = pl.BlockSpec((tm, d), lambda i, j: (i, 0), pipeline_mode=pl.Buffered(1))
    return pl.pallas_call(
        _ffn_kernel,
        grid=(m // tm, dff // tf),
        in_specs=[row, vec, vec, vec,
                  pl.BlockSpec((None, d, tf), lambda i, j: (layer, 0, j)),
                  pl.BlockSpec((None, d, tf), lambda i, j: (layer, 0, j)),
                  pl.BlockSpec((None, tf, d), lambda i, j: (layer, j, 0))],
        out_specs=[row, row],
        out_shape=[jax.ShapeDtypeStruct((m, d), F32), jax.ShapeDtypeStruct((m, d), BF16)],
        scratch_shapes=[pltpu.VMEM((tm, d), BF16)],
        compiler_params=pltpu.CompilerParams(dimension_semantics=("parallel", "arbitrary"),
                                             vmem_limit_bytes=FFN_VMEM_LIMIT),
        name="ffn_swiglu",
    )(x, g_pre.reshape(1, d), g_post.reshape(1, d), g_next.reshape(1, d), w1, w3, w2)


def _pad_rows(w, rows, offset):
    out = jnp.zeros((rows, w.shape[1]), w.dtype)
    return out.at[offset:offset + w.shape[0]].set(w)


_O_FF = 3 * FOX_W
_O_RWKV = _O_FF + FOX_HEADS
_O_GMLP = _O_RWKV + RWKV_IN_W
_O_GATE = _O_GMLP + 2 * GMLP_W


def _stacked_weights(w_in, w_br_fox, w_br_rwkv, w_br_gmlp, w_out, ffn_w1, ffn_w3, ffn_w2):
    return {
        "wq": w_in[:, :, :FOX_W].astype(BF16),
        "wk": w_in[:, :, FOX_W:2 * FOX_W].astype(BF16),
        "wv": w_in[:, :, 2 * FOX_W:_O_FF].astype(BF16),
        "w_rwkv": w_in[:, :, _O_RWKV:_O_GMLP].astype(BF16),
        "w_gmlp": w_in[:, :, _O_GMLP:_O_GATE].astype(BF16),
        "w_gate": w_in[:, :, _O_GATE:].astype(BF16),
        "w_br_fox": w_br_fox.astype(BF16),
        "w_br_rwkv": w_br_rwkv.astype(BF16),
        "w_br_gmlp": w_br_gmlp.astype(BF16),
        "w_out": w_out.astype(BF16),
        "ffn_w1": ffn_w1.astype(BF16),
        "ffn_w3": ffn_w3.astype(BF16),
        "ffn_w2": ffn_w2.astype(BF16),
    }


def _layer_weights(l, d, w_in, fox_bf, rwkv_mu, rwkv_w0, rwkv_w_up, rwkv_a0, rwkv_a_up, rwkv_g_up, rwkv_k_k,
                   rwkv_k_a, rwkv_r_k, rwkv_gn_g, rwkv_gn_b, gmlp_ln_g, gmlp_ln_b, gmlp_w_s, gmlp_b_s, s_new):
    tril_full = jnp.tril(gmlp_w_s[l])
    blk = jnp.tril(gmlp_w_s[l][:, :s_new, :s_new])
    reps = GMLP_CHUNK // s_new
    eye = jnp.eye(reps, dtype=F32)
    w_sample = jnp.einsum("ab,gij->gaibj", eye, blk).reshape(GMLP_GROUPS, GMLP_CHUNK, GMLP_CHUNK)
    return {
        "w_ff": jnp.zeros((d, LANE), BF16).at[:, :FOX_HEADS].set(w_in[l, :, _O_FF:_O_RWKV].astype(BF16)),
        "b_ff": jnp.zeros((1, LANE), F32).at[0, :FOX_HEADS].set(fox_bf[l]),
        "rwkv": {
            "mu": rwkv_mu[l].reshape(1, RWKV_IN_W),
            "w0": rwkv_w0[l].reshape(1, RWKV_W),
            "a0": rwkv_a0[l].reshape(1, RWKV_W),
            "k_k": rwkv_k_k[l].reshape(1, RWKV_W),
            "k_a": rwkv_k_a[l].reshape(1, RWKV_W),
            "r_k": rwkv_r_k[l].reshape(1, RWKV_W),
            "gn_g": rwkv_gn_g[l].reshape(1, RWKV_W),
            "gn_b": rwkv_gn_b[l].reshape(1, RWKV_W),
            "w_up": _pad_rows(rwkv_w_up[l], LANE, 0).astype(BF16),
            "a_up": _pad_rows(rwkv_a_up[l], LANE, rwkv_w_up.shape[1]).astype(BF16),
            "g_up": rwkv_g_up[l].astype(BF16),
        },
        "gmlp": {"ln_g": gmlp_ln_g[l].reshape(1, GMLP_W), "ln_b": gmlp_ln_b[l].reshape(1, GMLP_W)},
        "gmlp_w_prompt": tril_full.astype(BF16),
        "gmlp_b_prompt": gmlp_b_s[l].T,
        "gmlp_w_sample": w_sample.astype(BF16),
        "gmlp_b_sample": jnp.tile(gmlp_b_s[l][:, :s_new].T, (reps, 1)),
    }


def _trunk_layer(x, h, big, lw, post_mix_g, pre_ffn_g, post_ffn_g, next_mix_g, nseq, t, layer, cache, kv_stacks):
    m, d = x.shape
    q = _matmul(h, big["wq"], layer, (BF16,), scale=FOX_Q_SCALE, tn_pref=1024)
    k, k_bf = _kv_proj(h, big["wk"], layer, None if kv_stacks is None else kv_stacks[0])
    v, v_bf = _kv_proj(h, big["wv"], layer, None if kv_stacks is None else kv_stacks[1])
    rw = _matmul(h, big["w_rwkv"], layer, tn_pref=1664)
    gugv = _matmul(h, big["w_gmlp"], layer, (BF16,), tn_pref=1024)
    gates = _matmul(h, big["w_gate"], layer, (BF16,), tn_pref=1536)

    if cache is None:
        logf, xq, xk = _fox_forget(h, lw["w_ff"], lw["b_ff"], nseq, t)
        logf = logf[:, :FOX_HEADS]
        o_a = _fox_prompt(q, xq, k_bf, xk, v_bf, nseq, t)
        prev = jnp.zeros((nseq, 1, RWKV_IN_W), F32)
        state = None
        chunk = 64
        gm_w, gm_b = lw["gmlp_w_prompt"], lw["gmlp_b_prompt"]
    else:
        past = cache["k"].shape[2]
        total = past + t
        tpad = -(-total // LANE) * LANE
        logf = _forget_logits(h, lw["w_ff"], lw["b_ff"])[:, :FOX_HEADS]
        logf_rows = logf.reshape(nseq, t, FOX_HEADS).transpose(0, 2, 1)
        lf_all = jnp.concatenate([cache["logf"][layer].astype(F32).transpose(0, 2, 1), logf_rows], axis=-1)
        lf_all = jnp.pad(lf_all, ((0, 0), (0, 0), (0, tpad - total)))
        c_rows = _cumsum_lanes(lf_all.reshape(nseq * FOX_HEADS, tpad), LANE, LOG2E).reshape(nseq, FOX_HEADS, tpad)
        c_tm = c_rows[:, :, past:total].transpose(0, 2, 1).reshape(m, FOX_HEADS)
        o_a = _fox_sample(q, k_bf, v_bf, cache["k"], cache["v"], layer, c_tm, c_rows, nseq, t, past)
        prev = cache["shift"][layer]
        state = cache["state"]
        chunk = t
        gm_w, gm_b = lw["gmlp_w_sample"], lw["gmlp_b_sample"]

    r, lwd, k2, v2, kk, lr, g = _rwkv_prep(rw, prev, lw["rwkv"], nseq, t)
    o_b, s_new = _rwkv_chunks(r, lwd, k2, v2, kk, lr, g, lw["rwkv"], state, layer, nseq, t, chunk)
    shift_new = rw.reshape(nseq, t, RWKV_IN_W)[:, t - 1:t, :]

    o_c, vn = _gmlp(gugv, lw["gmlp"], gm_w, gm_b, cache is not None)

    mm = _merge(o_a, o_b, o_c, gates, big["w_br_fox"], big["w_br_rwkv"], big["w_br_gmlp"], layer)
    x = _out_proj(mm, big["w_out"], layer, x, post_mix_g)
    x, h_next = _ffn(x, pre_ffn_g, post_ffn_g, next_mix_g, big["ffn_w1"], big["ffn_w3"], big["ffn_w2"], layer)
    return x, h_next, (k, v, logf, s_new, shift_new, vn)


def kernel(x_prompt, x_sample, cache_fox_k, cache_fox_v, cache_fox_logf, state_rwkv, state_rwkv_shift, pre_mix_g, w_in, fox_bf, rwkv_mu, rwkv_w0, rwkv_w_up, rwkv_a0, rwkv_a_up, rwkv_g_up, rwkv_k_k, rwkv_k_a, rwkv_r_k, rwkv_gn_g, rwkv_gn_b, gmlp_ln_g, gmlp_ln_b, gmlp_w_s, gmlp_b_s, w_br_fox, w_br_rwkv, w_br_gmlp, w_out, post_mix_g, pre_ffn_g, ffn_w1, ffn_w3, ffn_w2, post_ffn_g):
    batch, seq, d = x_prompt.shape
    dec_batch, dec_seq, _ = x_sample.shape
    depth = w_in.shape[0]
    assert rwkv_w_up.shape[1] + rwkv_a_up.shape[1] == LANE and rwkv_g_up.shape[1] == LANE
    assert seq % GMLP_CHUNK == 0 and GMLP_CHUNK % dec_seq == 0 and (dec_batch * dec_seq) % GMLP_CHUNK == 0

    xp = x_prompt.reshape(batch * seq, d)
    xs = x_sample.reshape(dec_batch * dec_seq, d)
    cache = {"k": cache_fox_k, "v": cache_fox_v, "logf": cache_fox_logf,
             "state": state_rwkv, "shift": state_rwkv_shift}
    big = _stacked_weights(w_in, w_br_fox, w_br_rwkv, w_br_gmlp, w_out, ffn_w1, ffn_w3, ffn_w2)
    outs_p = [[] for _ in range(3)]
    outs_s = [[] for _ in range(6)]
    k_p = jnp.zeros((depth, batch * seq, FOX_HEADS, FOX_HD), F32)
    v_p = jnp.zeros((depth, batch * seq, FOX_HEADS, FOX_HD), F32)
    hp = _rmsnorm_cast(xp, pre_mix_g[0])
    hs = _rmsnorm_cast(xs, pre_mix_g[0])
    for l in range(depth):
        lw = _layer_weights(l, d, w_in, fox_bf, rwkv_mu, rwkv_w0, rwkv_w_up, rwkv_a0, rwkv_a_up, rwkv_g_up,
                            rwkv_k_k, rwkv_k_a, rwkv_r_k, rwkv_gn_g, rwkv_gn_b, gmlp_ln_g, gmlp_ln_b, gmlp_w_s,
                            gmlp_b_s, dec_seq)
        norms = (post_mix_g[l], pre_ffn_g[l], post_ffn_g[l], pre_mix_g[(l + 1) % depth])
        xp, hp, (k_p, v_p, lf_p, s_p, sh_p, _) = _trunk_layer(xp, hp, big, lw, *norms, batch, seq, l, None, (k_p, v_p))
        xs, hs, (k_s, v_s, lf_s, s_s, sh_s, vn_s) = _trunk_layer(xs, hs, big, lw, *norms, dec_batch, dec_seq, l, cache,
                                                                 None)
        for acc, val in zip(outs_p, (lf_p.reshape(batch, seq, FOX_HEADS), s_p, sh_p)):
            acc.append(val)
        for acc, val in zip(outs_s, (k_s.reshape(dec_batch, dec_seq, FOX_HEADS, FOX_HD),
                                     v_s.reshape(dec_batch, dec_seq, FOX_HEADS, FOX_HD),
                                     lf_s.reshape(dec_batch, dec_seq, FOX_HEADS), s_s, sh_s,
                                     vn_s.reshape(dec_batch, dec_seq, GMLP_W))):
            acc.append(val)
    return (xp.reshape(batch, seq, d), xs.reshape(dec_batch, dec_seq, d),
            k_p.reshape(depth, batch, seq, FOX_HEADS, FOX_HD), v_p.reshape(depth, batch, seq, FOX_HEADS, FOX_HD),
            *[jnp.stack(o) for o in outs_p], *[jnp.stack(o) for o in outs_s])
```
